```python
import jax, jax.numpy as jnp
from jax import lax
import numpy as np

D_MODEL = 1024
BATCH = 32
SEQ = 2048
DEPTH = 4
DEC_BATCH = 32
DEC_SEQ = 64
PAST_LEN = 1024

CHUNK = 64
N_EVEN = (DEPTH + 1) // 2
N_ODD = DEPTH // 2
H_A = 4
DK_A = 128
DV_A = 128
CONV_A = 4
H_B = 8
HD_B = 64
H_IDX = 4
D_IDX = 64
TOPK_MAX = 256
Q_BLOCK = 128
H_C = 16
HD_C = 64
BAND_CHUNKS = 8
WINDOW = BAND_CHUNKS * CHUNK
REL_CLIP = 128
N_REL = CHUNK + REL_CLIP
D_FF = 2816
CONV_F = 3
EPS = 1e-6

A_COLS = (H_A * DK_A, H_A * DK_A, H_A * DV_A, H_A * DV_A, H_A, H_A)
B_COLS = (H_B * HD_B, HD_B, HD_B, H_IDX * D_IDX, D_IDX, H_IDX)
IN_EVEN = sum(A_COLS) + sum(B_COLS)
MIX_EVEN = H_A * DV_A + H_B * HD_B
MIX_ODD = H_C * HD_C
STATE_KEYS = ('state_dn_S', 'state_dn_conv', 'cache_dsa_k', 'cache_dsa_v', 'cache_dsa_kidx',
              'cache_band_k', 'cache_band_v', 'state_ffn_conv')

kernel_name = 'hybrid_streaming_encoder_step'


def rms_norm(x, g):
    xf = x.astype(jnp.float32)
    y = xf * lax.rsqrt(jnp.mean(xf * xf, -1, keepdims=True) + EPS)
    return (y * g.astype(jnp.float32)).astype(x.dtype)


def l2_norm(x):
    xf = x.astype(jnp.float32)
    return (xf * lax.rsqrt(jnp.sum(xf * xf, -1, keepdims=True) + EPS)).astype(x.dtype)


def split_cols(h, sizes):
    offs = np.cumsum(sizes)[:-1].tolist()
    return jnp.split(h, offs, axis=-1)


def causal_dwconv(x, buf, w):
    k, L = w.shape[0], x.shape[1]
    xp = jnp.concatenate([buf.astype(x.dtype), x], axis=1)
    y = xp[:, 0:L] * w[0]
    for j in range(1, k):
        y = y + xp[:, j:j + L] * w[j]
    return y, xp[:, L:]


def gated_delta_chunked(q, k, v, g, beta, s0, c):
    B, L, H, DK = q.shape
    DV = v.shape[-1]
    n = L // c
    f32 = jnp.float32

    def blocks(t):
        t = t.astype(f32).reshape((B, n, c, H) + t.shape[3:])
        return jnp.transpose(t, (1, 0, 3, 2) + tuple(range(4, t.ndim)))

    qc = blocks(q) * DK ** -0.5
    kc = blocks(k)
    vc = blocks(v)
    gcum = jnp.cumsum(blocks(g), -1)
    bc = blocks(beta)
    tril = jnp.tril(jnp.ones((c, c), bool))
    strict = jnp.tril(jnp.ones((c, c), bool), -1)
    decay = jnp.exp(jnp.where(tril, gcum[..., :, None] - gcum[..., None, :], -jnp.inf))
    kb = kc * bc[..., None]
    m = jnp.where(strict, jnp.einsum('nbhid,nbhjd->nbhij', kb, kc) * decay, 0.0)
    eye = jnp.eye(c, dtype=f32)
    tmat = lax.linalg.triangular_solve(eye + m, jnp.broadcast_to(eye, m.shape), left_side=True, lower=True)
    u = jnp.einsum('nbhij,nbhjd->nbhid', tmat, vc * bc[..., None])
    w = jnp.einsum('nbhij,nbhjd->nbhid', tmat, kb * jnp.exp(gcum)[..., None])

    def step(S, inp):
        qi, ki, ui, wi, gi, di = inp
        v_new = ui - jnp.einsum('bhcd,bhde->bhce', wi, S)
        attn = jnp.einsum('bhid,bhjd->bhij', qi, ki) * di
        o = jnp.einsum('bhcd,bhde->bhce', qi * jnp.exp(gi)[..., None], S) + jnp.einsum('bhij,bhje->bhie', attn, v_new)
        glast = gi[..., -1]
        S = S * jnp.exp(glast)[..., None, None] + jnp.einsum(
            'bhcd,bhce->bhde', ki * jnp.exp(glast[..., None] - gi)[..., None], v_new)
        return S, o

    S, o = lax.scan(step, s0.astype(f32), (qc, kc, u, w, gcum, decay))
    o = jnp.transpose(o, (1, 0, 3, 2, 4)).reshape(B, L, H, DV)
    return o.astype(v.dtype), S


def deltanet_mixer(qa, ka, va, za, ba, aa, conv_buf, s0, conv_w, a_log, dt_bias, o_gain, chunk):
    B, L = qa.shape[:2]
    qkv, new_buf = causal_dwconv(jnp.concatenate([qa, ka, va], -1), conv_buf, conv_w)
    qkv = jax.nn.silu(qkv)
    q, k, v = split_cols(qkv, (H_A * DK_A, H_A * DK_A, H_A * DV_A))
    q = l2_norm(q.reshape(B, L, H_A, DK_A))
    k = l2_norm(k.reshape(B, L, H_A, DK_A))
    v = v.reshape(B, L, H_A, DV_A)
    beta = jax.nn.sigmoid(ba.astype(jnp.float32))
    g = -jnp.exp(a_log.astype(jnp.float32)) * jax.nn.softplus(aa.astype(jnp.float32) + dt_bias.astype(jnp.float32))
    o, s_new = gated_delta_chunked(q, k, v, g, beta, s0, chunk)
    o = rms_norm(o, o_gain) * jax.nn.silu(za.reshape(B, L, H_A, DV_A))
    return o.reshape(B, L, H_A * DV_A), new_buf, s_new


def indexer_scores(qi, ki, wi):
    dots = jnp.einsum('bqhd,bsd->bqhs', qi, ki).astype(jnp.float32) * D_IDX ** -0.5
    return jnp.einsum('bqhs,bqh->bqs', jax.nn.relu(dots), wi.astype(jnp.float32) * H_IDX ** -0.5)


def sparse_attend(q, k, v, sel, valid):
    gather = jax.vmap(lambda t, i: t[i])
    ks = gather(k, sel)
    vs = gather(v, sel)
    s = jnp.einsum('bqhd,bqkd->bqhk', q, ks).astype(jnp.float32) * HD_B ** -0.5
    s = jnp.where(valid[:, :, None, :], s, -jnp.inf)
    p = jax.nn.softmax(s, -1).astype(v.dtype)
    return jnp.einsum('bqhk,bqkd->bqhd', p, vs)


def dsa_prompt(q, k, v, qi, ki, wi):
    B, S = q.shape[:2]
    topk = min(TOPK_MAX, S // 4)
    key_chunk = jnp.arange(S) // CHUNK

    def block(i):
        q0 = i * Q_BLOCK
        qb = lax.dynamic_slice_in_dim(q, q0, Q_BLOCK, 1)
        qib = lax.dynamic_slice_in_dim(qi, q0, Q_BLOCK, 1)
        wib = lax.dynamic_slice_in_dim(wi, q0, Q_BLOCK, 1)
        q_chunk = (q0 + jnp.arange(Q_BLOCK)) // CHUNK
        score = indexer_scores(qib, ki, wib)
        score = jnp.where(key_chunk[None, None, :] <= q_chunk[None, :, None], score, -jnp.inf)
        _, sel = lax.top_k(score, topk)
        valid = key_chunk[sel] <= q_chunk[None, :, None]
        return sparse_attend(qb, k, v, sel, valid)

    out = lax.map(block, jnp.arange(S // Q_BLOCK))
    return jnp.moveaxis(out, 0, 1).reshape(B, S, H_B, HD_B)


def dsa_sample(q, k, v, qi, ki, wi, ck, cv, cki):
    k_all = jnp.concatenate([ck.astype(k.dtype), k], 1)
    v_all = jnp.concatenate([cv.astype(v.dtype), v], 1)
    ki_all = jnp.concatenate([cki.astype(ki.dtype), ki], 1)
    topk = min(TOPK_MAX, k_all.shape[1] // 4)
    _, sel = lax.top_k(indexer_scores(qi, ki_all, wi), topk)
    return sparse_attend(q, k_all, v_all, sel, jnp.ones(sel.shape, bool))


def rel_bias_block(rel_bias, n_q, n_k, offset):
    dist = (offset + jnp.arange(n_q))[:, None] - jnp.arange(n_k)[None, :]
    idx = jnp.clip(dist, -(CHUNK - 1), REL_CLIP) + (CHUNK - 1)
    return rel_bias[:, idx].astype(jnp.float32)


def band_attend(q, k, v, bias, valid):
    s = jnp.einsum('bqhd,bkhd->bhqk', q, k).astype(jnp.float32) * HD_C ** -0.5 + bias
    s = jnp.where(valid, s, -jnp.inf)
    p = jax.nn.softmax(s, -1).astype(v.dtype)
    return jnp.einsum('bhqk,bkhd->bqhd', p, v)


def band_prompt(q, k, v, rel_bias):
    B, S, H, D = q.shape
    pad = jnp.zeros((B, WINDOW, H, D), k.dtype)
    kp = jnp.concatenate([pad, k], 1)
    vp = jnp.concatenate([pad, v], 1)
    bias = rel_bias_block(rel_bias, CHUNK, WINDOW + CHUNK, WINDOW)

    def one(n):
        start = n * CHUNK
        qc = lax.dynamic_slice_in_dim(q, start, CHUNK, 1)
        kc = lax.dynamic_slice_in_dim(kp, start, WINDOW + CHUNK, 1)
        vc = lax.dynamic_slice_in_dim(vp, start, WINDOW + CHUNK, 1)
        valid = (start - WINDOW + jnp.arange(WINDOW + CHUNK)) >= 0
        return band_attend(qc, kc, vc, bias, valid[None, None, None, :])

    out = lax.map(one, jnp.arange(S // CHUNK))
    return jnp.moveaxis(out, 0, 1).reshape(B, S, H, D)


def band_sample(q, k, v, ck, cv, rel_bias):
    T, Wc = q.shape[1], ck.shape[1]
    k_all = jnp.concatenate([ck.astype(k.dtype), k], 1)
    v_all = jnp.concatenate([cv.astype(v.dtype), v], 1)
    bias = rel_bias_block(rel_bias, T, Wc + T, Wc)
    return band_attend(q, k_all, v_all, bias, jnp.ones((1, 1, 1, Wc + T), bool))


def conv_ffn(h, buf, w_a, w_g, conv_w, conv_b, w_down):
    a, new_buf = causal_dwconv(h @ w_a, buf, conv_w)
    return (jax.nn.silu(a + conv_b) * (h @ w_g)) @ w_down, new_buf


def trunk(x, past, p):
    B, L, _ = x.shape
    first = past is None
    chunk = CHUNK if first else L
    out = {name: [] for name in STATE_KEYS}
    for layer in range(DEPTH):
        h = rms_norm(x, p['norm_mix'][layer])
        if layer % 2 == 0:
            e = layer // 2
            qa, ka, va, za, ba, aa, qb, kb, vb, qi, ki, wi = split_cols(h @ p['w_in_even'][e], A_COLS + B_COLS)
            if first:
                conv_buf = jnp.zeros((B, CONV_A - 1, 3 * H_A * DK_A), x.dtype)
                s0 = jnp.zeros((B, H_A, DK_A, DV_A), jnp.float32)
            else:
                conv_buf = past['state_dn_conv'][e]
                s0 = past['state_dn_S'][e]
            o_a, buf_a, s_a = deltanet_mixer(qa, ka, va, za, ba, aa, conv_buf, s0, p['dn_conv_w'][e],
                                             p['dn_a_log'][e], p['dn_dt_bias'][e], p['dn_o_gain'][e], chunk)
            q_b = rms_norm(qb.reshape(B, L, H_B, HD_B), p['dsa_q_gain'][e])
            k_b = rms_norm(kb, p['dsa_k_gain'][e])
            qi = qi.reshape(B, L, H_IDX, D_IDX)
            if first:
                o_b = dsa_prompt(q_b, k_b, vb, qi, ki, wi)
            else:
                o_b = dsa_sample(q_b, k_b, vb, qi, ki, wi, past['cache_dsa_k'][e], past['cache_dsa_v'][e],
                                 past['cache_dsa_kidx'][e])
            mix = jnp.concatenate([o_a, o_b.reshape(B, L, H_B * HD_B)], -1) @ p['w_out_even'][e]
            out['state_dn_S'].append(s_a)
            out['state_dn_conv'].append(buf_a)
            out['cache_dsa_k'].append(k_b)
            out['cache_dsa_v'].append(vb)
            out['cache_dsa_kidx'].append(ki)
        else:
            j = layer // 2
            qc, kc, vc = split_cols(h @ p['w_in_odd'][j], (MIX_ODD, MIX_ODD, MIX_ODD))
            qc = rms_norm(qc.reshape(B, L, H_C, HD_C), p['band_q_gain'][j])
            kc = rms_norm(kc.reshape(B, L, H_C, HD_C), p['band_k_gain'][j])
            vc = vc.reshape(B, L, H_C, HD_C)
            if first:
                o_c = band_prompt(qc, kc, vc, p['band_rel_bias'][j])
                keep = min(WINDOW, L)
                new_k, new_v = kc[:, L - keep:], vc[:, L - keep:]
            else:
                o_c = band_sample(qc, kc, vc, past['cache_band_k'][j], past['cache_band_v'][j], p['band_rel_bias'][j])
                new_k, new_v = kc, vc
            mix = o_c.reshape(B, L, MIX_ODD) @ p['w_out_odd'][j]
            out['cache_band_k'].append(new_k)
            out['cache_band_v'].append(new_v)
        x = x + mix
        h = rms_norm(x, p['norm_ffn'][layer])
        fbuf = jnp.zeros((B, CONV_F - 1, D_FF), x.dtype) if first else past['state_ffn_conv'][layer]
        f, fbuf_new = conv_ffn(h, fbuf, p['ffn_w_a'][layer], p['ffn_w_g'][layer], p['ffn_conv_w'][layer],
                               p['ffn_conv_b'][layer], p['ffn_w_down'][layer])
        x = x + f
        out['state_ffn_conv'].append(fbuf_new)
    y = rms_norm(x, p['norm_final'])
    return y, {name: jnp.stack(v) for name, v in out.items()}


def setup_inputs(seed: int = 0) -> dict:
    key = jax.random.key(seed)
    ks = iter(jax.random.split(key, 40))

    def nrm(shape, scale=1.0):
        return jax.random.normal(next(ks), shape, jnp.float32) * scale

    def gain(shape):
        return 1.0 + 0.1 * nrm(shape)

    band_len = min(WINDOW, PAST_LEN)
    c_a = 3 * H_A * DK_A
    return {
        'x_prompt': nrm((BATCH, SEQ, D_MODEL)),
        'x_sample': nrm((DEC_BATCH, DEC_SEQ, D_MODEL)),
        'state_dn_S': nrm((N_EVEN, DEC_BATCH, H_A, DK_A, DV_A), 0.1),
        'state_dn_conv': nrm((N_EVEN, DEC_BATCH, CONV_A - 1, c_a)),
        'cache_dsa_k': nrm((N_EVEN, DEC_BATCH, PAST_LEN, HD_B)),
        'cache_dsa_v': nrm((N_EVEN, DEC_BATCH, PAST_LEN, HD_B)),
        'cache_dsa_kidx': nrm((N_EVEN, DEC_BATCH, PAST_LEN, D_IDX)),
        'cache_band_k': nrm((N_ODD, DEC_BATCH, band_len, H_C, HD_C)),
        'cache_band_v': nrm((N_ODD, DEC_BATCH, band_len, H_C, HD_C)),
        'state_ffn_conv': nrm((DEPTH, DEC_BATCH, CONV_F - 1, D_FF)),
        'norm_mix': gain((DEPTH, D_MODEL)),
        'norm_ffn': gain((DEPTH, D_MODEL)),
        'norm_final': gain((D_MODEL,)),
        'w_in_even': nrm((N_EVEN, D_MODEL, IN_EVEN), D_MODEL ** -0.5),
        'dn_conv_w': nrm((N_EVEN, CONV_A, c_a), CONV_A ** -0.5),
        'dn_a_log': jnp.log(jax.random.uniform(next(ks), (N_EVEN, H_A), jnp.float32, 1.0, 16.0)),
        'dn_dt_bias': jax.random.uniform(next(ks), (N_EVEN, H_A), jnp.float32, -4.0, -2.0),
        'dn_o_gain': gain((N_EVEN, DV_A)),
        'dsa_q_gain': gain((N_EVEN, HD_B)),
        'dsa_k_gain': gain((N_EVEN, HD_B)),
        'w_out_even': nrm((N_EVEN, MIX_EVEN, D_MODEL), 0.5 * MIX_EVEN ** -0.5),
        'w_in_odd': nrm((N_ODD, D_MODEL, 3 * MIX_ODD), D_MODEL ** -0.5),
        'band_q_gain': gain((N_ODD, HD_C)),
        'band_k_gain': gain((N_ODD, HD_C)),
        'band_rel_bias': nrm((N_ODD, H_C, N_REL), 0.2),
        'w_out_odd': nrm((N_ODD, MIX_ODD, D_MODEL), 0.5 * MIX_ODD ** -0.5),
        'ffn_w_a': nrm((DEPTH, D_MODEL, D_FF), D_MODEL ** -0.5),
        'ffn_w_g': nrm((DEPTH, D_MODEL, D_FF), D_MODEL ** -0.5),
        'ffn_conv_w': nrm((DEPTH, CONV_F, D_FF), CONV_F ** -0.5),
        'ffn_conv_b': nrm((DEPTH, D_FF), 0.02),
        'ffn_w_down': nrm((DEPTH, D_FF, D_MODEL), 0.5 * D_FF ** -0.5),
    }


def reference(x_prompt, x_sample, state_dn_S, state_dn_conv, cache_dsa_k, cache_dsa_v, cache_dsa_kidx,
              cache_band_k, cache_band_v, state_ffn_conv, norm_mix, norm_ffn, norm_final, w_in_even,
              dn_conv_w, dn_a_log, dn_dt_bias, dn_o_gain, dsa_q_gain, dsa_k_gain, w_out_even, w_in_odd,
              band_q_gain, band_k_gain, band_rel_bias, w_out_odd, ffn_w_a, ffn_w_g, ffn_conv_w, ffn_conv_b,
              ffn_w_down):
    p = dict(norm_mix=norm_mix, norm_ffn=norm_ffn, norm_final=norm_final, w_in_even=w_in_even,
             dn_conv_w=dn_conv_w, dn_a_log=dn_a_log, dn_dt_bias=dn_dt_bias, dn_o_gain=dn_o_gain,
             dsa_q_gain=dsa_q_gain, dsa_k_gain=dsa_k_gain, w_out_even=w_out_even, w_in_odd=w_in_odd,
             band_q_gain=band_q_gain, band_k_gain=band_k_gain, band_rel_bias=band_rel_bias,
             w_out_odd=w_out_odd, ffn_w_a=ffn_w_a, ffn_w_g=ffn_w_g, ffn_conv_w=ffn_conv_w,
             ffn_conv_b=ffn_conv_b, ffn_w_down=ffn_w_down)
    past = dict(state_dn_S=state_dn_S, state_dn_conv=state_dn_conv, cache_dsa_k=cache_dsa_k,
                cache_dsa_v=cache_dsa_v, cache_dsa_kidx=cache_dsa_kidx, cache_band_k=cache_band_k,
                cache_band_v=cache_band_v, state_ffn_conv=state_ffn_conv)
    y_prompt, sp = trunk(x_prompt, None, p)
    y_sample, ss = trunk(x_sample, past, p)
    return (y_prompt, y_sample,
            sp['state_dn_S'], ss['state_dn_S'],
            sp['state_dn_conv'], ss['state_dn_conv'],
            sp['cache_dsa_k'], ss['cache_dsa_k'],
            sp['cache_dsa_v'], ss['cache_dsa_v'],
            sp['cache_dsa_kidx'], ss['cache_dsa_kidx'],
            sp['cache_band_k'], ss['cache_band_k'],
            sp['cache_band_v'], ss['cache_band_v'],
            sp['state_ffn_conv'], ss['state_ffn_conv'])
```

```python
import functools
import math

import jax
import jax.numpy as jnp
from jax import lax
from jax.experimental import pallas as pl
from jax.experimental.pallas import tpu as pltpu

F32 = jnp.float32
BF16 = jnp.bfloat16
EPS = 1e-6
INT_MIN = -(2 ** 31)
LOG2E = math.log2(math.e)

CHUNK = 64
H_A, DK_A, DV_A, CONV_A = 4, 128, 128, 4
C_A = 3 * H_A * DK_A
H_B, HD_B, H_IDX, D_IDX, TOPK_MAX = 8, 64, 4, 64, 256
H_C, HD_C, BAND_CHUNKS, REL_CLIP = 16, 64, 8, 128
WINDOW = BAND_CHUNKS * CHUNK
CONV_F = 3

EVEN_COLS = 3072
COL_Z = 1536
COL_QB = 2048
COL_QI = 2560
COL_KV = 2816
COL_SM = 2944
SM_BETA, SM_A, SM_WI = 64, 68, 72

LANES = 128
VMEM_LIMIT = 48 * 1024 * 1024


def _cparams(sem):
    return pltpu.CompilerParams(dimension_semantics=sem, vmem_limit_bytes=VMEM_LIMIT)


def _dot(a, b):
    return jnp.dot(a.astype(BF16), b.astype(BF16), preferred_element_type=F32)


def _dot_nt(a, b):
    return lax.dot_general(a.astype(BF16), b.astype(BF16), (((1,), (1,)), ((), ())),
                           preferred_element_type=F32)


def _dot_f32(a, b):
    return jnp.dot(a, b, preferred_element_type=F32, precision=lax.Precision.HIGHEST)


def _dot_nt_f32(a, b):
    return lax.dot_general(a, b, (((1,), (1,)), ((), ())), preferred_element_type=F32,
                           precision=lax.Precision.HIGHEST)


def _rms(x, gain):
    return x * lax.rsqrt(jnp.mean(x * x, axis=-1, keepdims=True) + EPS) * gain


def _silu(x):
    return x * jax.nn.sigmoid(x)


def _low_half(shape):
    return lax.broadcasted_iota(jnp.int32, shape, len(shape) - 1) < (LANES // 2)


def _pair_rms(x, gain2):
    lo = _low_half(x.shape)
    sq = x * x
    ms_lo = jnp.sum(jnp.where(lo, sq, 0.0), axis=-1, keepdims=True) * (2.0 / LANES)
    ms_hi = jnp.sum(jnp.where(lo, 0.0, sq), axis=-1, keepdims=True) * (2.0 / LANES)
    return x * jnp.where(lo, lax.rsqrt(ms_lo + EPS), lax.rsqrt(ms_hi + EPS)) * gain2


def _split_halves(x):
    lo = _low_half(x.shape)
    return jnp.concatenate([jnp.where(lo, x, 0.0), jnp.where(lo, 0.0, x)], axis=0)


def _join_halves(o, rows):
    return jnp.where(_low_half((rows, LANES)), o[0:rows], o[rows:2 * rows])


def _softmax2_pv(s, v):
    m = jnp.max(s, axis=-1, keepdims=True)
    e = jnp.exp2(s - m)
    den = jnp.sum(e, axis=-1, keepdims=True)
    return _dot(e, v) / den


def _dup(x):
    return jnp.concatenate([x, x], axis=1)


def _norm_proj_kernel(x_ref, g_ref, w_ref, o_ref, h_scr):
    @pl.when(pl.program_id(1) == 0)
    def _():
        h_scr[...] = _rms(x_ref[...], g_ref[...]).astype(BF16)

    o_ref[...] = jnp.dot(h_scr[...], w_ref[...], preferred_element_type=F32)


def _norm_proj(x2d, gain, w_bf16, tm, tn):
    m, d = x2d.shape
    n = w_bf16.shape[1]
    return pl.pallas_call(
        _norm_proj_kernel,
        out_shape=jax.ShapeDtypeStruct((m, n), F32),
        grid=(m // tm, n // tn),
        in_specs=[pl.BlockSpec((tm, d), lambda i, j: (i, 0)),
                  pl.BlockSpec((1, d), lambda i, j: (0, 0)),
                  pl.BlockSpec((d, tn), lambda i, j: (0, j))],
        out_specs=pl.BlockSpec((tm, tn), lambda i, j: (i, j)),
        scratch_shapes=[pltpu.VMEM((tm, d), BF16)],
        compiler_params=_cparams(("parallel", "arbitrary")),
        name="norm_proj",
    )(x2d, gain.reshape(1, d), w_bf16)


def _tri_inv(m, eye_f):
    a = -m
    t = eye_f + a
    p = a
    for _ in range(5):
        p = _dot(p, p)
        t = t + _dot(t, p)
    return t


def _deltanet_kernel(qkv_ref, z_ref, sm_ref, conv0_ref, s0_ref, cw_ref, alog_ref, dtb_ref, og_ref,
                     o_ref, convout_ref, sout_ref, xbuf, s_scr, *, nc):
    c = pl.program_id(1)
    t = nc * CHUNK

    @pl.when(c == 0)
    def _():
        xbuf[0:8, :] = conv0_ref[0]
        for h in range(H_A):
            s_scr[:, h * DK_A:(h + 1) * DK_A] = jnp.transpose(s0_ref[0, h])

    x = qkv_ref[0]
    xbuf[8:8 + t, :] = x
    w = cw_ref[...]
    y = (x * w[3:4] + xbuf[7:7 + t, :] * w[2:3] + xbuf[6:6 + t, :] * w[1:2] + xbuf[5:5 + t, :] * w[0:1])
    convout_ref[0] = xbuf[t + 5:t + 8, :]
    xbuf[0:8, :] = xbuf[t:t + 8, :]
    y = _silu(y)

    sm = sm_ref[0]
    beta = jax.nn.sigmoid(sm)
    aa = sm + dtb_ref[...]
    softplus = jnp.maximum(aa, 0.0) + jnp.log1p(jnp.exp(-jnp.abs(aa)))
    g = -jnp.exp(alog_ref[...]) * softplus

    ht = H_A * CHUNK
    hk = H_A * DK_A
    ri = lax.broadcasted_iota(jnp.int32, (ht, ht), 0)
    ci = lax.broadcasted_iota(jnp.int32, (ht, ht), 1)
    same = jnp.right_shift(ri, 6) == jnp.right_shift(ci, 6)
    trilbd = jnp.logical_and(same, ri >= ci)
    strictbd = jnp.logical_and(same, ri > ci)
    eye_f = (ri == ci).astype(F32)
    r2 = lax.broadcasted_iota(jnp.int32, (2 * ht, hk), 0)
    c2 = lax.broadcasted_iota(jnp.int32, (2 * ht, hk), 1)
    bd2 = (jnp.right_shift(r2, 6) & (H_A - 1)) == jnp.right_shift(c2, 7)
    bd1 = bd2[0:ht]
    r64 = lax.broadcasted_iota(jnp.int32, (CHUNK, CHUNK), 0)
    c64 = lax.broadcasted_iota(jnp.int32, (CHUNK, CHUNK), 1)
    tril_f = (r64 >= c64).astype(F32)
    og = og_ref[...]

    def heads_on_rows(a, r0, off, width):
        return jnp.concatenate([a[r0:r0 + CHUNK, off + h * width:off + (h + 1) * width] for h in range(H_A)],
                               axis=0)

    for cidx in range(nc):
        r0 = cidx * CHUNK
        gcum = _dot_f32(tril_f, g[r0:r0 + CHUNK])
        q = heads_on_rows(y, r0, 0, DK_A)
        k = heads_on_rows(y, r0, hk, DK_A)
        v = heads_on_rows(y, r0, 2 * hk, DV_A)
        q = q * lax.rsqrt(jnp.sum(q * q, axis=-1, keepdims=True) + EPS) * (DK_A ** -0.5)
        k = k * lax.rsqrt(jnp.sum(k * k, axis=-1, keepdims=True) + EPS)
        bcol = heads_on_rows(beta, r0, SM_BETA, 1)
        gcol = heads_on_rows(gcum, 0, SM_A, 1)
        glast = [gcum[CHUNK - 1:CHUNK, SM_A + h:SM_A + h + 1] for h in range(H_A)]
        gl_col = jnp.concatenate([jnp.broadcast_to(x_, (CHUNK, 1)) for x_ in glast], axis=0)
        gl_row = jnp.concatenate([jnp.broadcast_to(x_, (1, DK_A)) for x_ in glast], axis=1)
        grow = jnp.transpose(jnp.broadcast_to(gcol, (ht, LANES)))[0:1, :]
        decay = jnp.exp(jnp.where(trilbd, gcol - grow, -jnp.inf))
        kb = k * bcol
        a_full = _dot_nt(jnp.concatenate([kb, q], axis=0), k)
        m = jnp.where(strictbd, a_full[0:ht] * decay, 0.0)
        attn = a_full[ht:2 * ht] * decay
        tmat = _tri_inv(m, eye_f)
        uw = _dot(tmat, jnp.concatenate([v * bcol, kb * jnp.exp(gcol)], axis=1))
        u = uw[:, 0:DV_A]
        wq = jnp.concatenate([uw[:, DV_A:DV_A + DK_A], q * jnp.exp(gcol)], axis=0)
        wq_bd = jnp.where(bd2, jnp.concatenate([wq] * H_A, axis=1), 0.0)
        st_old = s_scr[...]
        ws_qs = _dot_nt(wq_bd, st_old)
        v_new = u - ws_qs[0:ht]
        o = ws_qs[ht:2 * ht] + _dot(attn, v_new)
        kd = k * jnp.exp(gl_col - gcol)
        kd_bd = jnp.where(bd1, jnp.concatenate([kd] * H_A, axis=1), 0.0)
        s_scr[...] = st_old * jnp.exp(gl_row) + _dot(jnp.transpose(v_new), kd_bd)
        zz = heads_on_rows(z_ref[0], r0, 0, DV_A)
        res = (_rms(o, og) * _silu(zz)).astype(o_ref.dtype)
        for h in range(H_A):
            o_ref[0, r0:r0 + CHUNK, h * DV_A:(h + 1) * DV_A] = res[h * CHUNK:(h + 1) * CHUNK]

    for h in range(H_A):
        sout_ref[0, h] = jnp.transpose(s_scr[:, h * DK_A:(h + 1) * DK_A])


def _deltanet(proj, conv0, s0, conv_w, a_log, dt_bias, o_gain, nc):
    b, l, _ = proj.shape
    t = nc * CHUNK
    pad = jnp.zeros((1, LANES), F32)
    alog128 = lax.dynamic_update_slice(pad, a_log.reshape(1, H_A), (0, SM_A))
    dtb128 = lax.dynamic_update_slice(pad, dt_bias.reshape(1, H_A), (0, SM_A))
    return pl.pallas_call(
        functools.partial(_deltanet_kernel, nc=nc),
        out_shape=(jax.ShapeDtypeStruct((b, l, H_A * DV_A), BF16),
                   jax.ShapeDtypeStruct((b, CONV_A - 1, C_A), F32),
                   jax.ShapeDtypeStruct((b, H_A, DK_A, DV_A), F32)),
        grid=(b, l // t),
        in_specs=[pl.BlockSpec((1, t, C_A), lambda i, c: (i, c, 0)),
                  pl.BlockSpec((1, t, 512), lambda i, c: (i, c, COL_Z // 512)),
                  pl.BlockSpec((1, t, LANES), lambda i, c: (i, c, COL_SM // LANES)),
                  pl.BlockSpec((1, 8, C_A), lambda i, c: (i, 0, 0)),
                  pl.BlockSpec((1, H_A, DK_A, DV_A), lambda i, c: (i, 0, 0, 0)),
                  pl.BlockSpec((CONV_A, C_A), lambda i, c: (0, 0)),
                  pl.BlockSpec((1, LANES), lambda i, c: (0, 0)),
                  pl.BlockSpec((1, LANES), lambda i, c: (0, 0)),
                  pl.BlockSpec((1, DV_A), lambda i, c: (0, 0))],
        out_specs=(pl.BlockSpec((1, t, H_A * DV_A), lambda i, c: (i, c, 0)),
                   pl.BlockSpec((1, CONV_A - 1, C_A), lambda i, c: (i, 0, 0)),
                   pl.BlockSpec((1, H_A, DK_A, DV_A), lambda i, c: (i, 0, 0, 0))),
        scratch_shapes=[pltpu.VMEM((t + 8, C_A), F32), pltpu.VMEM((DV_A, H_A * DK_A), F32)],
        compiler_params=_cparams(("parallel", "arbitrary")),
        name="deltanet",
    )(proj, proj, proj, conv0, s0, conv_w, alog128, dtb128, o_gain.reshape(1, DV_A))


def _dsa_prep_kernel(*refs, l, p, s_pad):
    if p:
        kv_ref, sm_ref, kg_ref, ck_ref, cv_ref, cki_ref, kout_ref, vout_ref, kiout_ref, k2_ref, v2_ref, ki2_ref = refs
    else:
        kv_ref, sm_ref, kg_ref, kout_ref, vout_ref, kiout_ref, k2_ref, v2_ref, ki2_ref = refs
    kv = kv_ref[0]
    kn = _rms(kv[:, 0:HD_B], kg_ref[...])
    vraw = kv[:, HD_B:2 * HD_B]
    kidx = sm_ref[0][:, 0:D_IDX]
    kout_ref[0] = kn
    vout_ref[0] = vraw
    kiout_ref[0] = kidx
    if p:
        k2_ref[0, 0:p, :] = _dup(ck_ref[0]).astype(BF16)
        v2_ref[0, 0:p, :] = _dup(cv_ref[0]).astype(BF16)
        ki2_ref[0, 0:p, :] = _dup(cki_ref[0]).astype(BF16)
    k2_ref[0, p:p + l, :] = _dup(kn).astype(BF16)
    v2_ref[0, p:p + l, :] = _dup(vraw).astype(BF16)
    ki2_ref[0, p:p + l, :] = _dup(kidx).astype(BF16)
    if s_pad > p + l:
        zpad = jnp.zeros((s_pad - p - l, LANES), BF16)
        k2_ref[0, p + l:s_pad, :] = zpad
        v2_ref[0, p + l:s_pad, :] = zpad
        ki2_ref[0, p + l:s_pad, :] = zpad


def _colsum(x):
    s, n = x.shape
    return jnp.sum(jnp.sum(x.reshape(s // CHUNK, CHUNK, n), axis=0), axis=0, keepdims=True)


def _dsa_kernel(qb_ref, qi_ref, smq_ref, qg_ref, k2_ref, v2_ref, ki2_ref, o_ref, key_scr, eq_scr, lim_scr,
                *, tq, tile0, p, s_keys, k_sel):
    tg = tile0 + pl.program_id(1)

    smq = smq_ref[0]
    sel_r = lax.broadcasted_iota(jnp.int32, (8, LANES), 0)
    sel_c = lax.broadcasted_iota(jnp.int32, (8, LANES), 1)
    pick = (sel_c == sel_r + SM_WI).astype(F32)
    wi_t = _dot_nt_f32(pick, smq) * (H_IDX ** -0.5)
    qi = qi_ref[0]
    ki_all = ki2_ref[0]
    qheads = jnp.concatenate([_split_halves(qi[:, pr * LANES:(pr + 1) * LANES]) for pr in range(H_IDX // 2)],
                             axis=0)
    score = jnp.zeros((s_keys, tq), F32)
    if tq % LANES == 0:
        d_all = _dot_nt(ki_all, qheads)
        for h in range(H_IDX):
            score = score + jnp.maximum(d_all[:, h * tq:(h + 1) * tq] * (D_IDX ** -0.5), 0.0) * wi_t[h:h + 1, :]
    else:
        for h in range(H_IDX):
            d = _dot_nt(ki_all, qheads[h * tq:(h + 1) * tq])
            score = score + jnp.maximum(d * (D_IDX ** -0.5), 0.0) * wi_t[h:h + 1, :]

    bits = pltpu.bitcast(score + 0.0, jnp.int32)
    key = bits ^ (jnp.right_shift(bits, 31) & 0x7FFFFFFF)
    kpos = lax.broadcasted_iota(jnp.int32, (s_keys, tq), 0)
    qpos = tg * tq + lax.broadcasted_iota(jnp.int32, (s_keys, tq), 1)
    adm = kpos < p + (jnp.right_shift(qpos, 6) + 1) * CHUNK
    key_scr[...] = jnp.where(adm, key, INT_MIN)

    kf = float(k_sel)

    def count_ge(cand):
        return _colsum(jnp.where(key_scr[...] >= cand, 1.0, 0.0))

    thr0 = jnp.where(count_ge(jnp.zeros((1, tq), jnp.int32)) >= kf, 0, INT_MIN).astype(jnp.int32)

    def thr_body(i, thr):
        cand = thr | jnp.left_shift(jnp.int32(1), 30 - i)
        return jnp.where(count_ge(cand) >= kf, cand, thr)

    thr = lax.fori_loop(0, 31, thr_body, thr0)

    key = key_scr[...]
    gt = key > thr
    eq = jnp.logical_and(key == thr, adm)
    eqf = jnp.where(eq, 1.0, 0.0)
    need = kf - _colsum(jnp.where(gt, 1.0, 0.0))
    surplus = _colsum(eqf) - need
    nbits = int(s_keys).bit_length()
    lim_scr[...] = jnp.full(lim_scr.shape, 2 ** nbits - 1, jnp.int32)

    @pl.when(jnp.max(surplus) > 0.0)
    def _():
        eq_scr[...] = eqf

        def tie_body(i, lim):
            cand = lim | jnp.left_shift(jnp.int32(1), nbits - 1 - i)
            cnt = _colsum(jnp.where(kpos < cand, eq_scr[...], 0.0))
            return jnp.where(cnt <= need, cand, lim)

        lim = lax.fori_loop(0, nbits, tie_body, jnp.zeros((1, tq), jnp.int32))
        lim_scr[...] = jnp.broadcast_to(lim, lim_scr.shape)

    lim = lim_scr[0:1, :]
    sel_t = jnp.where(jnp.logical_or(gt, jnp.logical_and(eq, kpos < lim)), 1.0, 0.0).astype(BF16)

    er = lax.broadcasted_iota(jnp.int32, (tq, tq), 0)
    ec = lax.broadcasted_iota(jnp.int32, (tq, tq), 1)
    sel = _dot_nt((er == ec).astype(BF16), sel_t)
    mask_bias = jnp.where(sel > 0.5, 0.0, -jnp.inf)

    qb = qb_ref[0]
    k_all = k2_ref[0]
    v_all = v2_ref[0]
    outs = []
    for pr in range(H_B // 2):
        qn = _pair_rms(qb[:, pr * LANES:(pr + 1) * LANES], qg_ref[...]) * (HD_B ** -0.5 * LOG2E)
        s = _dot_nt(_split_halves(qn), k_all)
        s = (s.reshape(2, tq, s_keys) + mask_bias[None]).reshape(2 * tq, s_keys)
        outs.append(_join_halves(_softmax2_pv(s, v_all), tq))
    o_ref[0] = jnp.concatenate(outs, axis=1).astype(o_ref.dtype)


def _dsa(proj, q_gain, k_gain, cache):
    b, l, _ = proj.shape
    p = 0 if cache is None else cache[0].shape[1]
    tq = min(LANES, l)
    s_k = p + l
    s_pad = -(-s_k // LANES) * LANES
    k_sel = min(TOPK_MAX, s_k // 4)

    kv_specs = [pl.BlockSpec((1, l, LANES), lambda i: (i, 0, COL_KV // LANES)),
                pl.BlockSpec((1, l, LANES), lambda i: (i, 0, COL_SM // LANES)),
                pl.BlockSpec((1, HD_B), lambda i: (0, 0))]
    args = [proj, proj, k_gain.reshape(1, HD_B)]
    if p:
        kv_specs += [pl.BlockSpec((1, p, HD_B), lambda i: (i, 0, 0))] * 3
        args += list(cache)
    row_out = jax.ShapeDtypeStruct((b, l, HD_B), F32)
    dup_out = jax.ShapeDtypeStruct((b, s_pad, LANES), BF16)
    row_spec = pl.BlockSpec((1, l, HD_B), lambda i: (i, 0, 0))
    dup_spec = pl.BlockSpec((1, s_pad, LANES), lambda i: (i, 0, 0))
    k_b, v_b, ki_b, k2, v2, ki2 = pl.pallas_call(
        functools.partial(_dsa_prep_kernel, l=l, p=p, s_pad=s_pad),
        out_shape=(row_out, row_out, row_out, dup_out, dup_out, dup_out),
        grid=(b,),
        in_specs=kv_specs,
        out_specs=(row_spec, row_spec, row_spec, dup_spec, dup_spec, dup_spec),
        compiler_params=_cparams(("parallel",)),
        name="dsa_prep",
    )(*args)

    n_tiles = l // tq
    tiles_per_class = max(1, 256 // tq) if p == 0 else n_tiles
    qg2 = _dup(q_gain.reshape(1, HD_B))
    outs = []
    for tile0 in range(0, n_tiles, tiles_per_class):
        nt = min(tiles_per_class, n_tiles - tile0)
        s_keys = min(s_pad, -(-(p + (tile0 + nt) * tq) // LANES) * LANES)
        outs.append(pl.pallas_call(
            functools.partial(_dsa_kernel, tq=tq, tile0=tile0, p=p, s_keys=s_keys, k_sel=k_sel),
            out_shape=jax.ShapeDtypeStruct((b, nt * tq, H_B * HD_B), BF16),
            grid=(b, nt),
            in_specs=[pl.BlockSpec((1, tq, 512), lambda i, t, t0=tile0: (i, t0 + t, COL_QB // 512)),
                      pl.BlockSpec((1, tq, 256), lambda i, t, t0=tile0: (i, t0 + t, COL_QI // 256)),
                      pl.BlockSpec((1, tq, LANES), lambda i, t, t0=tile0: (i, t0 + t, COL_SM // LANES)),
                      pl.BlockSpec((1, LANES), lambda i, t: (0, 0)),
                      pl.BlockSpec((1, s_keys, LANES), lambda i, t: (i, 0, 0)),
                      pl.BlockSpec((1, s_keys, LANES), lambda i, t: (i, 0, 0)),
                      pl.BlockSpec((1, s_keys, LANES), lambda i, t: (i, 0, 0))],
            out_specs=pl.BlockSpec((1, tq, H_B * HD_B), lambda i, t: (i, t, 0)),
            scratch_shapes=[pltpu.VMEM((s_keys, tq), jnp.int32), pltpu.VMEM((s_keys, tq), F32),
                            pltpu.VMEM((8, tq), jnp.int32)],
            compiler_params=_cparams(("parallel", "arbitrary")),
            name="dsa",
        )(proj, proj, proj, qg2, k2, v2, ki2))
    o_b = outs[0] if len(outs) == 1 else jnp.concatenate(outs, axis=1)
    return o_b, k_b, v_b, ki_b


def _band_kernel(*refs, tb, cg, has_cache):
    if has_cache:
        q_ref, k_ref, v_ref, bias_ref, qg_ref, kg_ref, ck_ref, cv_ref, o_ref, kn_ref, kwin, vwin = refs
    else:
        q_ref, k_ref, v_ref, bias_ref, qg_ref, kg_ref, o_ref, kn_ref, kwin, vwin = refs
    j = pl.program_id(1)
    tbq = tb * CHUNK
    gq = cg * CHUNK
    wk = WINDOW + gq

    @pl.when(j == 0)
    def _():
        if has_cache:
            kwin[0:WINDOW, :] = ck_ref[0].astype(BF16)
            vwin[0:WINDOW, :] = cv_ref[0].astype(BF16)
        else:
            kwin[0:WINDOW, :] = jnp.zeros((WINDOW, H_C * HD_C), BF16)
            vwin[0:WINDOW, :] = jnp.zeros((WINDOW, H_C * HD_C), BF16)

    @pl.when(j > 0)
    def _():
        for i in range(WINDOW // tbq):
            kwin[i * tbq:(i + 1) * tbq, :] = kwin[(i + 1) * tbq:(i + 2) * tbq, :]
            vwin[i * tbq:(i + 1) * tbq, :] = vwin[(i + 1) * tbq:(i + 2) * tbq, :]

    vwin[WINDOW:WINDOW + tbq, :] = v_ref[0].astype(BF16)
    kcol = lax.broadcasted_iota(jnp.int32, (2 * gq, wk), 1)
    n_inv = jnp.maximum(WINDOW - j * tbq, 0)
    for i in range(H_C // 2):
        cols = slice(i * LANES, (i + 1) * LANES)
        kn = _pair_rms(k_ref[0, :, cols], kg_ref[...])
        kn_ref[0, :, cols] = kn
        kwin[WINDOW:WINDOW + tbq, cols] = kn.astype(BF16)
        qn = _pair_rms(q_ref[0, :, cols], qg_ref[...]) * (HD_C ** -0.5 * LOG2E)
        for g in range(tb // cg):
            r0 = g * gq
            s = _dot_nt(_split_halves(qn[r0:r0 + gq]), kwin[r0:r0 + wk, cols]) + bias_ref[i]
            if not has_cache:
                s = jnp.where(kcol >= n_inv - r0, s, -jnp.inf)
            o = _join_halves(_softmax2_pv(s, vwin[r0:r0 + wk, cols]), gq)
            o_ref[0, r0:r0 + gq, cols] = o.astype(o_ref.dtype)


def _band(proj, rel_bias, q_gain, k_gain, cache, tb, cg):
    b, l, _ = proj.shape
    d = H_C * HD_C
    tbq = tb * CHUNK
    gq = cg * CHUNK
    wk = WINDOW + gq
    has_cache = cache is not None
    in_specs = [pl.BlockSpec((1, tbq, d), lambda i, j: (i, j, 0)),
                pl.BlockSpec((1, tbq, d), lambda i, j: (i, j, 1)),
                pl.BlockSpec((1, tbq, d), lambda i, j: (i, j, 2)),
                pl.BlockSpec((H_C // 2, 2 * gq, wk), lambda i, j: (0, 0, 0)),
                pl.BlockSpec((1, LANES), lambda i, j: (0, 0)),
                pl.BlockSpec((1, LANES), lambda i, j: (0, 0))]
    args = [proj, proj, proj, _band_bias(rel_bias, cg), _dup(q_gain.reshape(1, HD_C)), _dup(k_gain.reshape(1, HD_C))]
    if has_cache:
        in_specs += [pl.BlockSpec((1, WINDOW, d), lambda i, j: (i, 0, 0))] * 2
        args += [cache[0].reshape(b, WINDOW, d), cache[1].reshape(b, WINDOW, d)]
    return pl.pallas_call(
        functools.partial(_band_kernel, tb=tb, cg=cg, has_cache=has_cache),
        out_shape=(jax.ShapeDtypeStruct((b, l, d), BF16), jax.ShapeDtypeStruct((b, l, d), F32)),
        grid=(b, l // tbq),
        in_specs=in_specs,
        out_specs=(pl.BlockSpec((1, tbq, d), lambda i, j: (i, j, 0)),
                   pl.BlockSpec((1, tbq, d), lambda i, j: (i, j, 0))),
        scratch_shapes=[pltpu.VMEM((WINDOW + tbq, d), BF16), pltpu.VMEM((WINDOW + tbq, d), BF16)],
        compiler_params=_cparams(("parallel", "arbitrary")),
        name="band",
    )(*args)


def _band_bias(rel_bias, cg):
    gq = cg * CHUNK
    wk = WINDOW + gq
    qi = jnp.arange(gq)
    kj = jnp.arange(wk)
    dist = (WINDOW + qi)[:, None] - kj[None, :]
    idx = jnp.clip(dist, -(CHUNK - 1), REL_CLIP) + (CHUNK - 1)
    qc = qi // CHUNK
    kc = kj // CHUNK
    inband = jnp.logical_and(kc[None, :] >= qc[:, None], kc[None, :] <= qc[:, None] + BAND_CHUNKS)
    bias = jnp.where(inband[None], rel_bias[:, idx].astype(F32) * LOG2E, -jnp.inf)
    return bias.reshape(H_C // 2, 2 * gq, wk)


def _ffn_kernel(*refs, n_mix, nf, nm, tm, tf, final):
    x_ref = refs[0]
    mix_refs = refs[1:1 + n_mix]
    wo_refs = refs[1 + n_mix:1 + 2 * n_mix]
    rest = refs[1 + 2 * n_mix:]
    if final:
        g_ref, wa_ref, wg_ref, cw_ref, cb_ref, wd_ref, st0_ref, gf_ref = rest[:8]
        rest = rest[8:]
    else:
        g_ref, wa_ref, wg_ref, cw_ref, cb_ref, wd_ref, st0_ref = rest[:7]
        gf_ref = None
        rest = rest[7:]
    out_ref, stout_ref, xres, hbuf, acc, abuf, cbuf = rest
    mi = pl.program_id(1)
    f = pl.program_id(2)

    @pl.when(f == 0)
    def _():
        xr = x_ref[0]
        for m_ref, w_ref in zip(mix_refs, wo_refs):
            xr = xr + jnp.dot(m_ref[0], w_ref[...], preferred_element_type=F32)
        xres[...] = xr
        hbuf[...] = _rms(xr, g_ref[...]).astype(BF16)
        acc[...] = jnp.zeros_like(acc)

    hb = hbuf[...]
    a = jnp.dot(hb, wa_ref[...], preferred_element_type=F32)
    gate = jnp.dot(hb, wg_ref[...], preferred_element_type=F32)

    @pl.when(mi == 0)
    def _():
        abuf[0:8, :] = st0_ref[0]

    @pl.when(mi > 0)
    def _():
        abuf[0:8, :] = cbuf[f]

    abuf[8:8 + tm, :] = a
    cw = cw_ref[...]
    ac = a * cw[2:3] + abuf[7:7 + tm, :] * cw[1:2] + abuf[6:6 + tm, :] * cw[0:1] + cb_ref[...]
    cbuf[f] = abuf[tm:tm + 8, :]
    u = _silu(ac) * gate
    acc[...] += jnp.dot(u.astype(BF16), wd_ref[...], preferred_element_type=F32)

    @pl.when(f == nf - 1)
    def _():
        yv = xres[...] + acc[...]
        if final:
            yv = _rms(yv, gf_ref[...])
        out_ref[0] = yv

    @pl.when(jnp.logical_and(f == nf - 1, mi == nm - 1))
    def _():
        for ff in range(nf):
            stout_ref[0, :, ff * tf:(ff + 1) * tf] = cbuf[ff, 8 - (CONV_F - 1):8, :]


def _ffn(x, mixes, wos, gain, wa, wg, conv_w, conv_b, wd, st0, gain_final, tm, tf):
    b, l, d = x.shape
    dff = wa.shape[1]
    nf = dff // tf
    nm = l // tm
    n_mix = len(mixes)
    final = gain_final is not None
    in_specs = [pl.BlockSpec((1, tm, d), lambda i, m, f: (i, m, 0))]
    in_specs += [pl.BlockSpec((1, tm, mx.shape[2]), lambda i, m, f: (i, m, 0)) for mx in mixes]
    in_specs += [pl.BlockSpec(w.shape, lambda i, m, f: (0, 0)) for w in wos]
    in_specs += [pl.BlockSpec((1, d), lambda i, m, f: (0, 0)),
                 pl.BlockSpec((d, tf), lambda i, m, f: (0, f)),
                 pl.BlockSpec((d, tf), lambda i, m, f: (0, f)),
                 pl.BlockSpec((CONV_F, tf), lambda i, m, f: (0, f)),
                 pl.BlockSpec((1, tf), lambda i, m, f: (0, f)),
                 pl.BlockSpec((tf, d), lambda i, m, f: (f, 0)),
                 pl.BlockSpec((1, 8, tf), lambda i, m, f: (i, 0, f))]
    args = [x, *mixes, *wos, gain.reshape(1, d), wa, wg, conv_w, conv_b.reshape(1, dff), wd, st0]
    if final:
        in_specs.append(pl.BlockSpec((1, d), lambda i, m, f: (0, 0)))
        args.append(gain_final.reshape(1, d))
    return pl.pallas_call(
        functools.partial(_ffn_kernel, n_mix=n_mix, nf=nf, nm=nm, tm=tm, tf=tf, final=final),
        out_shape=(jax.ShapeDtypeStruct((b, l, d), F32),
                   jax.ShapeDtypeStruct((b, CONV_F - 1, dff), F32)),
        grid=(b, nm, nf),
        in_specs=in_specs,
        out_specs=(pl.BlockSpec((1, tm, d), lambda i, m, f: (i, m, 0)),
                   pl.BlockSpec((1, CONV_F - 1, dff), lambda i, m, f: (i, 0, 0))),
        scratch_shapes=[pltpu.VMEM((tm, d), F32), pltpu.VMEM((tm, d), BF16), pltpu.VMEM((tm, d), F32),
                        pltpu.VMEM((tm + 8, tf), F32), pltpu.VMEM((nf, 8, tf), F32)],
        compiler_params=_cparams(("parallel", "arbitrary", "arbitrary")),
        name="ffn",
    )(*args)


def _pad_rows_front(a, rows):
    return jnp.pad(a, ((0, 0), (rows - a.shape[1], 0), (0, 0)))


def _trunk(x, past, w):
    b, l, d = x.shape
    depth = w["norm_mix"].shape[0]
    first = past is None
    n_chunks = l // CHUNK
    tm_proj = min(512, b * l)
    tm_ffn = min(512, l)
    tf = 256
    nc = min(4, n_chunks)
    tb = min(4, n_chunks)
    cg = min(2, tb)
    dff = w["ffn_w_a"].shape[2]
    out = {k: [] for k in ("dn_S", "dn_conv", "dsa_k", "dsa_v", "dsa_kidx", "band_k", "band_v", "ffn_conv")}
    for layer in range(depth):
        x2d = x.reshape(b * l, d)
        if layer % 2 == 0:
            e = layer // 2
            proj = _norm_proj(x2d, w["norm_mix"][layer], w["w_in_even"][e], tm_proj, 1024).reshape(b, l, EVEN_COLS)
            if first:
                conv0 = jnp.zeros((b, 8, C_A), F32)
                s0 = jnp.zeros((b, H_A, DK_A, DV_A), F32)
                cache = None
            else:
                conv0 = _pad_rows_front(past["state_dn_conv"][e], 8)
                s0 = past["state_dn_S"][e]
                cache = (past["cache_dsa_k"][e], past["cache_dsa_v"][e], past["cache_dsa_kidx"][e])
            o_a, buf_a, s_a = _deltanet(proj, conv0, s0, w["dn_conv_w"][e], w["dn_a_log"][e],
                                        w["dn_dt_bias"][e], w["dn_o_gain"][e], nc)
            o_b, k_b, v_b, ki_b = _dsa(proj, w["dsa_q_gain"][e], w["dsa_k_gain"][e], cache)
            mixes = [o_a, o_b]
            wos = [w["w_out_even"][e][:H_A * DV_A], w["w_out_even"][e][H_A * DV_A:]]
            out["dn_S"].append(s_a)
            out["dn_conv"].append(buf_a)
            out["dsa_k"].append(k_b)
            out["dsa_v"].append(v_b)
            out["dsa_kidx"].append(ki_b)
        else:
            jj = layer // 2
            proj = _norm_proj(x2d, w["norm_mix"][layer], w["w_in_odd"][jj], tm_proj, 1024).reshape(b, l, 3 * H_C * HD_C)
            cache = None if first else (past["cache_band_k"][jj], past["cache_band_v"][jj])
            o_c, kn = _band(proj, w["band_rel_bias"][jj], w["band_q_gain"][jj], w["band_k_gain"][jj], cache, tb, cg)
            keep = min(WINDOW, l) if first else l
            out["band_k"].append(kn[:, l - keep:].reshape(b, keep, H_C, HD_C))
            out["band_v"].append(proj[:, l - keep:, 2 * H_C * HD_C:].reshape(b, keep, H_C, HD_C))
            mixes = [o_c]
            wos = [w["w_out_odd"][jj]]
        st0 = (jnp.zeros((b, 8, dff), F32) if first else _pad_rows_front(past["state_ffn_conv"][layer], 8))
        gfin = w["norm_final"] if layer == depth - 1 else None
        x, fbuf = _ffn(x, mixes, wos, w["norm_ffn"][layer], w["ffn_w_a"][layer], w["ffn_w_g"][layer],
                       w["ffn_conv_w"][layer], w["ffn_conv_b"][layer], w["ffn_w_down"][layer], st0, gfin,
                       tm_ffn, tf)
        out["ffn_conv"].append(fbuf)
    return x, {k: jnp.stack(v) for k, v in out.items()}


def _prep_weights(norm_mix, norm_ffn, norm_final, w_in_even, dn_conv_w, dn_a_log, dn_dt_bias, dn_o_gain,
                  dsa_q_gain, dsa_k_gain, w_out_even, w_in_odd, band_q_gain, band_k_gain, band_rel_bias,
                  w_out_odd, ffn_w_a, ffn_w_g, ffn_conv_w, ffn_conv_b, ffn_w_down):
    o_ba = 4 * H_A * DK_A
    o_qb = o_ba + 2 * H_A
    o_kb = o_qb + H_B * HD_B
    o_qi = o_kb + 2 * HD_B
    o_ki = o_qi + H_IDX * D_IDX
    o_wi = o_ki + D_IDX
    o_end = o_wi + H_IDX
    wie = w_in_even
    pad = jnp.zeros(wie.shape[:2] + (EVEN_COLS - o_end,), wie.dtype)
    w_even = jnp.concatenate([wie[..., :o_ba], wie[..., o_qb:o_kb], wie[..., o_qi:o_ki], wie[..., o_kb:o_qi],
                              wie[..., o_ki:o_wi], wie[..., o_ba:o_qb], wie[..., o_wi:o_end], pad], axis=-1)
    return dict(norm_mix=norm_mix, norm_ffn=norm_ffn, norm_final=norm_final,
                w_in_even=w_even.astype(BF16), dn_conv_w=dn_conv_w, dn_a_log=dn_a_log, dn_dt_bias=dn_dt_bias,
                dn_o_gain=dn_o_gain, dsa_q_gain=dsa_q_gain, dsa_k_gain=dsa_k_gain,
                w_out_even=w_out_even.astype(BF16), w_in_odd=w_in_odd.astype(BF16),
                band_q_gain=band_q_gain, band_k_gain=band_k_gain, band_rel_bias=band_rel_bias,
                w_out_odd=w_out_odd.astype(BF16),
                ffn_w_a=ffn_w_a.astype(BF16), ffn_w_g=ffn_w_g.astype(BF16), ffn_conv_w=ffn_conv_w,
                ffn_conv_b=ffn_conv_b, ffn_w_down=ffn_w_down.astype(BF16))


def kernel(x_prompt, x_sample, state_dn_S, state_dn_conv, cache_dsa_k, cache_dsa_v, cache_dsa_kidx, cache_band_k, cache_band_v, state_ffn_conv, norm_mix, norm_ffn, norm_final, w_in_even, dn_conv_w, dn_a_log, dn_dt_bias, dn_o_gain, dsa_q_gain, dsa_k_gain, w_out_even, w_in_odd, band_q_gain, band_k_gain, band_rel_bias, w_out_odd, ffn_w_a, ffn_w_g, ffn_conv_w, ffn_conv_b, ffn_w_down):
    assert cache_band_k.shape[2] == WINDOW, "band cache must hold exactly one window"
    w = _prep_weights(norm_mix, norm_ffn, norm_final, w_in_even, dn_conv_w, dn_a_log, dn_dt_bias, dn_o_gain,
                      dsa_q_gain, dsa_k_gain, w_out_even, w_in_odd, band_q_gain, band_k_gain, band_rel_bias,
                      w_out_odd, ffn_w_a, ffn_w_g, ffn_conv_w, ffn_conv_b, ffn_w_down)
    past = dict(state_dn_S=state_dn_S, state_dn_conv=state_dn_conv, cache_dsa_k=cache_dsa_k,
                cache_dsa_v=cache_dsa_v, cache_dsa_kidx=cache_dsa_kidx, cache_band_k=cache_band_k,
                cache_band_v=cache_band_v, state_ffn_conv=state_ffn_conv)
    y_p, sp = _trunk(x_prompt, None, w)
    y_s, ss = _trunk(x_sample, past, w)
    return (y_p, y_s,
            sp["dn_S"], ss["dn_S"], sp["dn_conv"], ss["dn_conv"],
            sp["dsa_k"], ss["dsa_k"], sp["dsa_v"], ss["dsa_v"], sp["dsa_kidx"], ss["dsa_kidx"],
            sp["band_k"], ss["band_k"], sp["band_v"], ss["band_v"],
            sp["ffn_conv"], ss["ffn_conv"])
```

```python
import functools
import math

import jax
import jax.numpy as jnp
from jax import lax
from jax.experimental import pallas as pl
from jax.experimental.pallas import tpu as pltpu

F32 = jnp.float32
BF16 = jnp.bfloat16
EPS = 1e-6
INT_MIN = -(2 ** 31)
LOG2E = math.log2(math.e)

CHUNK = 64
H_A, DK_A, DV_A, CONV_A = 4, 128, 128, 4
C_A = 3 * H_A * DK_A
H_B, HD_B, H_IDX, D_IDX, TOPK_MAX = 8, 64, 4, 64, 256
H_C, HD_C, BAND_CHUNKS, REL_CLIP = 16, 64, 8, 128
WINDOW = BAND_CHUNKS * CHUNK
CONV_F = 3

EVEN_COLS = 3072
COL_Z = 1536
COL_QB = 2048
COL_QI = 2560
COL_KV = 2816
COL_SM = 2944
SM_BETA, SM_A, SM_WI = 64, 68, 72

LANES = 128
VMEM_LIMIT = 48 * 1024 * 1024


def _cparams(sem):
    return pltpu.CompilerParams(dimension_semantics=sem, vmem_limit_bytes=VMEM_LIMIT)


def _dot(a, b):
    return jnp.dot(a.astype(BF16), b.astype(BF16), preferred_element_type=F32)


def _dot_nt(a, b):
    return lax.dot_general(a.astype(BF16), b.astype(BF16), (((1,), (1,)), ((), ())),
                           preferred_element_type=F32)


def _dot_f32(a, b):
    return jnp.dot(a, b, preferred_element_type=F32, precision=lax.Precision.HIGHEST)


def _dot_nt_f32(a, b):
    return lax.dot_general(a, b, (((1,), (1,)), ((), ())), preferred_element_type=F32,
                           precision=lax.Precision.HIGHEST)


def _rms(x, gain):
    return x * lax.rsqrt(jnp.mean(x * x, axis=-1, keepdims=True) + EPS) * gain


def _silu(x):
    return x * jax.nn.sigmoid(x)


def _low_half(shape):
    return lax.broadcasted_iota(jnp.int32, shape, len(shape) - 1) < (LANES // 2)


def _pair_rms(x, gain2):
    lo = _low_half(x.shape)
    sq = x * x
    ms_lo = jnp.sum(jnp.where(lo, sq, 0.0), axis=-1, keepdims=True) * (2.0 / LANES)
    ms_hi = jnp.sum(jnp.where(lo, 0.0, sq), axis=-1, keepdims=True) * (2.0 / LANES)
    return x * jnp.where(lo, lax.rsqrt(ms_lo + EPS), lax.rsqrt(ms_hi + EPS)) * gain2


def _split_halves(x):
    lo = _low_half(x.shape)
    return jnp.concatenate([jnp.where(lo, x, 0.0), jnp.where(lo, 0.0, x)], axis=0)


def _join_halves(o, rows):
    return jnp.where(_low_half((rows, LANES)), o[0:rows], o[rows:2 * rows])


def _softmax2_pv(s, v):
    m = jnp.max(s, axis=-1, keepdims=True)
    e = jnp.exp2(s - m)
    den = jnp.sum(e, axis=-1, keepdims=True)
    return _dot(e, v) / den


def _dup(x):
    return jnp.concatenate([x, x], axis=1)


def _norm_proj_kernel(x_ref, g_ref, w_ref, o_ref):
    o_ref[...] = jnp.dot(_rms(x_ref[...], g_ref[...]).astype(BF16), w_ref[...], preferred_element_type=F32)


def _norm_proj(x2d, gain, w_bf16, tm):
    m, d = x2d.shape
    n = w_bf16.shape[1]
    return pl.pallas_call(
        _norm_proj_kernel,
        out_shape=jax.ShapeDtypeStruct((m, n), F32),
        grid=(m // tm,),
        in_specs=[pl.BlockSpec((tm, d), lambda i: (i, 0)),
                  pl.BlockSpec((1, d), lambda i: (0, 0)),
                  pl.BlockSpec((d, n), lambda i: (0, 0))],
        out_specs=pl.BlockSpec((tm, n), lambda i: (i, 0)),
        compiler_params=_cparams(("parallel",)),
        name="norm_proj",
    )(x2d, gain.reshape(1, d), w_bf16)


def _tri_inv(m, eye_f):
    a = -m
    t = eye_f + a
    p = a
    for _ in range(5):
        p = _dot(p, p)
        t = t + _dot(t, p)
    return t


def _deltanet_kernel(qkv_ref, z_ref, sm_ref, conv0_ref, s0_ref, cw_ref, alog_ref, dtb_ref, og_ref,
                     o_ref, convout_ref, sout_ref, xbuf, s_scr, *, nc):
    c = pl.program_id(1)
    t = nc * CHUNK

    @pl.when(c == 0)
    def _():
        xbuf[0:8, :] = conv0_ref[0]
        for h in range(H_A):
            s_scr[:, h * DK_A:(h + 1) * DK_A] = jnp.transpose(s0_ref[0, h])

    x = qkv_ref[0]
    xbuf[8:8 + t, :] = x
    w = cw_ref[...]
    y = (x * w[3:4] + xbuf[7:7 + t, :] * w[2:3] + xbuf[6:6 + t, :] * w[1:2] + xbuf[5:5 + t, :] * w[0:1])
    convout_ref[0] = xbuf[t + 5:t + 8, :]
    xbuf[0:8, :] = xbuf[t:t + 8, :]
    y = _silu(y)

    sm = sm_ref[0]
    beta = jax.nn.sigmoid(sm)
    aa = sm + dtb_ref[...]
    softplus = jnp.maximum(aa, 0.0) + jnp.log1p(jnp.exp(-jnp.abs(aa)))
    g = -jnp.exp(alog_ref[...]) * softplus

    ht = H_A * CHUNK
    hk = H_A * DK_A
    ri = lax.broadcasted_iota(jnp.int32, (ht, ht), 0)
    ci = lax.broadcasted_iota(jnp.int32, (ht, ht), 1)
    same = jnp.right_shift(ri, 6) == jnp.right_shift(ci, 6)
    trilbd = jnp.logical_and(same, ri >= ci)
    strictbd = jnp.logical_and(same, ri > ci)
    eye_f = (ri == ci).astype(F32)
    r2 = lax.broadcasted_iota(jnp.int32, (2 * ht, hk), 0)
    c2 = lax.broadcasted_iota(jnp.int32, (2 * ht, hk), 1)
    bd2 = (jnp.right_shift(r2, 6) & (H_A - 1)) == jnp.right_shift(c2, 7)
    bd1 = bd2[0:ht]
    r64 = lax.broadcasted_iota(jnp.int32, (CHUNK, CHUNK), 0)
    c64 = lax.broadcasted_iota(jnp.int32, (CHUNK, CHUNK), 1)
    tril_f = (r64 >= c64).astype(F32)
    og = og_ref[...]

    def heads_on_rows(a, r0, off, width):
        return jnp.concatenate([a[r0:r0 + CHUNK, off + h * width:off + (h + 1) * width] for h in range(H_A)],
                               axis=0)

    for cidx in range(nc):
        r0 = cidx * CHUNK
        gcum = _dot_f32(tril_f, g[r0:r0 + CHUNK])
        q = heads_on_rows(y, r0, 0, DK_A)
        k = heads_on_rows(y, r0, hk, DK_A)
        v = heads_on_rows(y, r0, 2 * hk, DV_A)
        q = q * lax.rsqrt(jnp.sum(q * q, axis=-1, keepdims=True) + EPS) * (DK_A ** -0.5)
        k = k * lax.rsqrt(jnp.sum(k * k, axis=-1, keepdims=True) + EPS)
        bcol = heads_on_rows(beta, r0, SM_BETA, 1)
        gcol = heads_on_rows(gcum, 0, SM_A, 1)
        glast = [gcum[CHUNK - 1:CHUNK, SM_A + h:SM_A + h + 1] for h in range(H_A)]
        gl_col = jnp.concatenate([jnp.broadcast_to(x_, (CHUNK, 1)) for x_ in glast], axis=0)
        gl_row = jnp.concatenate([jnp.broadcast_to(x_, (1, DK_A)) for x_ in glast], axis=1)
        grow = jnp.transpose(jnp.broadcast_to(gcol, (ht, LANES)))[0:1, :]
        decay = jnp.exp(jnp.where(trilbd, gcol - grow, -jnp.inf))
        kb = k * bcol
        a_full = _dot_nt(jnp.concatenate([kb, q], axis=0), k)
        m = jnp.where(strictbd, a_full[0:ht] * decay, 0.0)
        attn = a_full[ht:2 * ht] * decay
        tmat = _tri_inv(m, eye_f)
        uw = _dot(tmat, jnp.concatenate([v * bcol, kb * jnp.exp(gcol)], axis=1))
        u = uw[:, 0:DV_A]
        wq = jnp.concatenate([uw[:, DV_A:DV_A + DK_A], q * jnp.exp(gcol)], axis=0)
        wq_bd = jnp.where(bd2, jnp.concatenate([wq] * H_A, axis=1), 0.0)
        st_old = s_scr[...]
        ws_qs = _dot_nt(wq_bd, st_old)
        v_new = u - ws_qs[0:ht]
        o = ws_qs[ht:2 * ht] + _dot(attn, v_new)
        kd = k * jnp.exp(gl_col - gcol)
        kd_bd = jnp.where(bd1, jnp.concatenate([kd] * H_A, axis=1), 0.0)
        s_scr[...] = st_old * jnp.exp(gl_row) + _dot(jnp.transpose(v_new), kd_bd)
        zz = heads_on_rows(z_ref[0], r0, 0, DV_A)
        res = (_rms(o, og) * _silu(zz)).astype(o_ref.dtype)
        for h in range(H_A):
            o_ref[0, r0:r0 + CHUNK, h * DV_A:(h + 1) * DV_A] = res[h * CHUNK:(h + 1) * CHUNK]

    for h in range(H_A):
        sout_ref[0, h] = jnp.transpose(s_scr[:, h * DK_A:(h + 1) * DK_A])


def _deltanet(proj, conv0, s0, conv_w, a_log, dt_bias, o_gain, nc):
    b, l, _ = proj.shape
    t = nc * CHUNK
    pad = jnp.zeros((1, LANES), F32)
    alog128 = lax.dynamic_update_slice(pad, a_log.reshape(1, H_A), (0, SM_A))
    dtb128 = lax.dynamic_update_slice(pad, dt_bias.reshape(1, H_A), (0, SM_A))
    return pl.pallas_call(
        functools.partial(_deltanet_kernel, nc=nc),
        out_shape=(jax.ShapeDtypeStruct((b, l, H_A * DV_A), BF16),
                   jax.ShapeDtypeStruct((b, CONV_A - 1, C_A), F32),
                   jax.ShapeDtypeStruct((b, H_A, DK_A, DV_A), F32)),
        grid=(b, l // t),
        in_specs=[pl.BlockSpec((1, t, C_A), lambda i, c: (i, c, 0)),
                  pl.BlockSpec((1, t, 512), lambda i, c: (i, c, COL_Z // 512)),
                  pl.BlockSpec((1, t, LANES), lambda i, c: (i, c, COL_SM // LANES)),
                  pl.BlockSpec((1, 8, C_A), lambda i, c: (i, 0, 0)),
                  pl.BlockSpec((1, H_A, DK_A, DV_A), lambda i, c: (i, 0, 0, 0)),
                  pl.BlockSpec((CONV_A, C_A), lambda i, c: (0, 0)),
                  pl.BlockSpec((1, LANES), lambda i, c: (0, 0)),
                  pl.BlockSpec((1, LANES), lambda i, c: (0, 0)),
                  pl.BlockSpec((1, DV_A), lambda i, c: (0, 0))],
        out_specs=(pl.BlockSpec((1, t, H_A * DV_A), lambda i, c: (i, c, 0)),
                   pl.BlockSpec((1, CONV_A - 1, C_A), lambda i, c: (i, 0, 0)),
                   pl.BlockSpec((1, H_A, DK_A, DV_A), lambda i, c: (i, 0, 0, 0))),
        scratch_shapes=[pltpu.VMEM((t + 8, C_A), F32), pltpu.VMEM((DV_A, H_A * DK_A), F32)],
        compiler_params=_cparams(("parallel", "arbitrary")),
        name="deltanet",
    )(proj, proj, proj, conv0, s0, conv_w, alog128, dtb128, o_gain.reshape(1, DV_A))


def _dsa_prep_kernel(*refs, l, p, s_pad):
    if p:
        kv_ref, sm_ref, kg_ref, ck_ref, cv_ref, cki_ref, kout_ref, vout_ref, kiout_ref, k2_ref, v2_ref, ki2_ref = refs
    else:
        kv_ref, sm_ref, kg_ref, kout_ref, vout_ref, kiout_ref, k2_ref, v2_ref, ki2_ref = refs
    kv = kv_ref[0]
    kn = _rms(kv[:, 0:HD_B], kg_ref[...])
    vraw = kv[:, HD_B:2 * HD_B]
    kidx = sm_ref[0][:, 0:D_IDX]
    kout_ref[0] = kn
    vout_ref[0] = vraw
    kiout_ref[0] = kidx
    if p:
        k2_ref[0, 0:p, :] = _dup(ck_ref[0]).astype(BF16)
        v2_ref[0, 0:p, :] = _dup(cv_ref[0]).astype(BF16)
        ki2_ref[0, 0:p, :] = _dup(cki_ref[0]).astype(BF16)
    k2_ref[0, p:p + l, :] = _dup(kn).astype(BF16)
    v2_ref[0, p:p + l, :] = _dup(vraw).astype(BF16)
    ki2_ref[0, p:p + l, :] = _dup(kidx).astype(BF16)
    if s_pad > p + l:
        zpad = jnp.zeros((s_pad - p - l, LANES), BF16)
        k2_ref[0, p + l:s_pad, :] = zpad
        v2_ref[0, p + l:s_pad, :] = zpad
        ki2_ref[0, p + l:s_pad, :] = zpad


def _colsum(x):
    s, n = x.shape
    return jnp.sum(jnp.sum(x.reshape(s // CHUNK, CHUNK, n), axis=0), axis=0, keepdims=True)


def _dsa_kernel(qb_ref, qi_ref, smq_ref, qg_ref, k2_ref, v2_ref, ki2_ref, o_ref, key_scr, eq_scr, lim_scr,
                *, tq, tile0, p, s_keys, k_sel):
    tg = tile0 + pl.program_id(1)

    smq = smq_ref[0]
    sel_r = lax.broadcasted_iota(jnp.int32, (8, LANES), 0)
    sel_c = lax.broadcasted_iota(jnp.int32, (8, LANES), 1)
    pick = (sel_c == sel_r + SM_WI).astype(F32)
    wi_t = _dot_nt_f32(pick, smq) * (H_IDX ** -0.5)
    qi = qi_ref[0]
    ki_all = ki2_ref[0]
    qheads = jnp.concatenate([_split_halves(qi[:, pr * LANES:(pr + 1) * LANES]) for pr in range(H_IDX // 2)],
                             axis=0)
    score = jnp.zeros((s_keys, tq), F32)
    if tq % LANES == 0:
        d_all = _dot_nt(ki_all, qheads)
        for h in range(H_IDX):
            score = score + jnp.maximum(d_all[:, h * tq:(h + 1) * tq] * (D_IDX ** -0.5), 0.0) * wi_t[h:h + 1, :]
    else:
        for h in range(H_IDX):
            d = _dot_nt(ki_all, qheads[h * tq:(h + 1) * tq])
            score = score + jnp.maximum(d * (D_IDX ** -0.5), 0.0) * wi_t[h:h + 1, :]

    bits = pltpu.bitcast(score + 0.0, jnp.int32)
    key = bits ^ (jnp.right_shift(bits, 31) & 0x7FFFFFFF)
    kpos = lax.broadcasted_iota(jnp.int32, (s_keys, tq), 0)
    qpos = tg * tq + lax.broadcasted_iota(jnp.int32, (s_keys, tq), 1)
    adm = kpos < p + (jnp.right_shift(qpos, 6) + 1) * CHUNK
    key_scr[...] = jnp.where(adm, key, INT_MIN)

    kf = float(k_sel)

    def count_ge(cand):
        return _colsum(jnp.where(key_scr[...] >= cand, 1.0, 0.0))

    thr0 = jnp.where(count_ge(jnp.zeros((1, tq), jnp.int32)) >= kf, 0, INT_MIN).astype(jnp.int32)

    def thr_body(i, thr):
        cand = thr | jnp.left_shift(jnp.int32(1), 30 - i)
        return jnp.where(count_ge(cand) >= kf, cand, thr)

    thr = lax.fori_loop(0, 31, thr_body, thr0)

    key = key_scr[...]
    gt = key > thr
    eq = jnp.logical_and(key == thr, adm)
    eqf = jnp.where(eq, 1.0, 0.0)
    need = kf - _colsum(jnp.where(gt, 1.0, 0.0))
    surplus = _colsum(eqf) - need
    nbits = int(s_keys).bit_length()
    lim_scr[...] = jnp.full(lim_scr.shape, 2 ** nbits - 1, jnp.int32)

    @pl.when(jnp.max(surplus) > 0.0)
    def _():
        eq_scr[...] = eqf

        def tie_body(i, lim):
            cand = lim | jnp.left_shift(jnp.int32(1), nbits - 1 - i)
            cnt = _colsum(jnp.where(kpos < cand, eq_scr[...], 0.0))
            return jnp.where(cnt <= need, cand, lim)

        lim = lax.fori_loop(0, nbits, tie_body, jnp.zeros((1, tq), jnp.int32))
        lim_scr[...] = jnp.broadcast_to(lim, lim_scr.shape)

    lim = lim_scr[0:1, :]
    sel_t = jnp.where(jnp.logical_or(gt, jnp.logical_and(eq, kpos < lim)), 1.0, 0.0).astype(BF16)

    er = lax.broadcasted_iota(jnp.int32, (tq, tq), 0)
    ec = lax.broadcasted_iota(jnp.int32, (tq, tq), 1)
    sel = _dot_nt((er == ec).astype(BF16), sel_t)
    mask_bias = jnp.where(sel > 0.5, 0.0, -jnp.inf)

    qb = qb_ref[0]
    k_all = k2_ref[0]
    v_all = v2_ref[0]
    outs = []
    for pr in range(H_B // 2):
        qn = _pair_rms(qb[:, pr * LANES:(pr + 1) * LANES], qg_ref[...]) * (HD_B ** -0.5 * LOG2E)
        s = _dot_nt(_split_halves(qn), k_all)
        s = (s.reshape(2, tq, s_keys) + mask_bias[None]).reshape(2 * tq, s_keys)
        outs.append(_join_halves(_softmax2_pv(s, v_all), tq))
    o_ref[0] = jnp.concatenate(outs, axis=1).astype(o_ref.dtype)


def _dsa(proj, q_gain, k_gain, cache):
    b, l, _ = proj.shape
    p = 0 if cache is None else cache[0].shape[1]
    tq = min(LANES, l)
    s_k = p + l
    s_pad = -(-s_k // LANES) * LANES
    k_sel = min(TOPK_MAX, s_k // 4)

    kv_specs = [pl.BlockSpec((1, l, LANES), lambda i: (i, 0, COL_KV // LANES)),
                pl.BlockSpec((1, l, LANES), lambda i: (i, 0, COL_SM // LANES)),
                pl.BlockSpec((1, HD_B), lambda i: (0, 0))]
    args = [proj, proj, k_gain.reshape(1, HD_B)]
    if p:
        kv_specs += [pl.BlockSpec((1, p, HD_B), lambda i: (i, 0, 0))] * 3
        args += list(cache)
    row_out = jax.ShapeDtypeStruct((b, l, HD_B), F32)
    dup_out = jax.ShapeDtypeStruct((b, s_pad, LANES), BF16)
    row_spec = pl.BlockSpec((1, l, HD_B), lambda i: (i, 0, 0))
    dup_spec = pl.BlockSpec((1, s_pad, LANES), lambda i: (i, 0, 0))
    k_b, v_b, ki_b, k2, v2, ki2 = pl.pallas_call(
        functools.partial(_dsa_prep_kernel, l=l, p=p, s_pad=s_pad),
        out_shape=(row_out, row_out, row_out, dup_out, dup_out, dup_out),
        grid=(b,),
        in_specs=kv_specs,
        out_specs=(row_spec, row_spec, row_spec, dup_spec, dup_spec, dup_spec),
        compiler_params=_cparams(("parallel",)),
        name="dsa_prep",
    )(*args)

    n_tiles = l // tq
    tiles_per_class = max(1, 256 // tq) if p == 0 else n_tiles
    qg2 = _dup(q_gain.reshape(1, HD_B))
    outs = []
    for tile0 in range(0, n_tiles, tiles_per_class):
        nt = min(tiles_per_class, n_tiles - tile0)
        s_keys = min(s_pad, -(-(p + (tile0 + nt) * tq) // LANES) * LANES)
        outs.append(pl.pallas_call(
            functools.partial(_dsa_kernel, tq=tq, tile0=tile0, p=p, s_keys=s_keys, k_sel=k_sel),
            out_shape=jax.ShapeDtypeStruct((b, nt * tq, H_B * HD_B), BF16),
            grid=(b, nt),
            in_specs=[pl.BlockSpec((1, tq, 512), lambda i, t, t0=tile0: (i, t0 + t, COL_QB // 512)),
                      pl.BlockSpec((1, tq, 256), lambda i, t, t0=tile0: (i, t0 + t, COL_QI // 256)),
                      pl.BlockSpec((1, tq, LANES), lambda i, t, t0=tile0: (i, t0 + t, COL_SM // LANES)),
                      pl.BlockSpec((1, LANES), lambda i, t: (0, 0)),
                      pl.BlockSpec((1, s_keys, LANES), lambda i, t: (i, 0, 0)),
                      pl.BlockSpec((1, s_keys, LANES), lambda i, t: (i, 0, 0)),
                      pl.BlockSpec((1, s_keys, LANES), lambda i, t: (i, 0, 0))],
            out_specs=pl.BlockSpec((1, tq, H_B * HD_B), lambda i, t: (i, t, 0)),
            scratch_shapes=[pltpu.VMEM((s_keys, tq), jnp.int32), pltpu.VMEM((s_keys, tq), F32),
                            pltpu.VMEM((8, tq), jnp.int32)],
            compiler_params=_cparams(("parallel", "arbitrary")),
            name="dsa",
        )(proj, proj, proj, qg2, k2, v2, ki2))
    o_b = outs[0] if len(outs) == 1 else jnp.concatenate(outs, axis=1)
    return o_b, k_b, v_b, ki_b


def _band_kernel(*refs, tb, cg, has_cache):
    if has_cache:
        q_ref, k_ref, v_ref, bias_ref, qg_ref, kg_ref, ck_ref, cv_ref, o_ref, kn_ref, kwin, vwin = refs
    else:
        q_ref, k_ref, v_ref, bias_ref, qg_ref, kg_ref, o_ref, kn_ref, kwin, vwin = refs
    j = pl.program_id(1)
    tbq = tb * CHUNK
    gq = cg * CHUNK
    wk = WINDOW + gq

    @pl.when(j == 0)
    def _():
        if has_cache:
            kwin[0:WINDOW, :] = ck_ref[0].astype(BF16)
            vwin[0:WINDOW, :] = cv_ref[0].astype(BF16)
        else:
            kwin[0:WINDOW, :] = jnp.zeros((WINDOW, H_C * HD_C), BF16)
            vwin[0:WINDOW, :] = jnp.zeros((WINDOW, H_C * HD_C), BF16)

    @pl.when(j > 0)
    def _():
        for i in range(WINDOW // tbq):
            kwin[i * tbq:(i + 1) * tbq, :] = kwin[(i + 1) * tbq:(i + 2) * tbq, :]
            vwin[i * tbq:(i + 1) * tbq, :] = vwin[(i + 1) * tbq:(i + 2) * tbq, :]

    vwin[WINDOW:WINDOW + tbq, :] = v_ref[0].astype(BF16)
    kcol = lax.broadcasted_iota(jnp.int32, (2 * gq, wk), 1)
    n_inv = jnp.maximum(WINDOW - j * tbq, 0)
    for i in range(H_C // 2):
        cols = slice(i * LANES, (i + 1) * LANES)
        kn = _pair_rms(k_ref[0, :, cols], kg_ref[...])
        kn_ref[0, :, cols] = kn
        kwin[WINDOW:WINDOW + tbq, cols] = kn.astype(BF16)
        qn = _pair_rms(q_ref[0, :, cols], qg_ref[...]) * (HD_C ** -0.5 * LOG2E)
        for g in range(tb // cg):
            r0 = g * gq
            s = _dot_nt(_split_halves(qn[r0:r0 + gq]), kwin[r0:r0 + wk, cols]) + bias_ref[i]
            if not has_cache:
                s = jnp.where(kcol >= n_inv - r0, s, -jnp.inf)
            o = _join_halves(_softmax2_pv(s, vwin[r0:r0 + wk, cols]), gq)
            o_ref[0, r0:r0 + gq, cols] = o.astype(o_ref.dtype)


def _band(proj, rel_bias, q_gain, k_gain, cache, tb, cg):
    b, l, _ = proj.shape
    d = H_C * HD_C
    tbq = tb * CHUNK
    gq = cg * CHUNK
    wk = WINDOW + gq
    has_cache = cache is not None
    in_specs = [pl.BlockSpec((1, tbq, d), lambda i, j: (i, j, 0)),
                pl.BlockSpec((1, tbq, d), lambda i, j: (i, j, 1)),
                pl.BlockSpec((1, tbq, d), lambda i, j: (i, j, 2)),
                pl.BlockSpec((H_C // 2, 2 * gq, wk), lambda i, j: (0, 0, 0)),
                pl.BlockSpec((1, LANES), lambda i, j: (0, 0)),
                pl.BlockSpec((1, LANES), lambda i, j: (0, 0))]
    args = [proj, proj, proj, _band_bias(rel_bias, cg), _dup(q_gain.reshape(1, HD_C)), _dup(k_gain.reshape(1, HD_C))]
    if has_cache:
        in_specs += [pl.BlockSpec((1, WINDOW, d), lambda i, j: (i, 0, 0))] * 2
        args += [cache[0].reshape(b, WINDOW, d), cache[1].reshape(b, WINDOW, d)]
    return pl.pallas_call(
        functools.partial(_band_kernel, tb=tb, cg=cg, has_cache=has_cache),
        out_shape=(jax.ShapeDtypeStruct((b, l, d), BF16), jax.ShapeDtypeStruct((b, l, d), F32)),
        grid=(b, l // tbq),
        in_specs=in_specs,
        out_specs=(pl.BlockSpec((1, tbq, d), lambda i, j: (i, j, 0)),
                   pl.BlockSpec((1, tbq, d), lambda i, j: (i, j, 0))),
        scratch_shapes=[pltpu.VMEM((WINDOW + tbq, d), BF16), pltpu.VMEM((WINDOW + tbq, d), BF16)],
        compiler_params=_cparams(("parallel", "arbitrary")),
        name="band",
    )(*args)


def _band_bias(rel_bias, cg):
    gq = cg * CHUNK
    wk = WINDOW + gq
    dist = WINDOW + gq - 1 - jnp.arange(wk + gq - 1)
    seq = rel_bias[:, jnp.clip(dist, -(CHUNK - 1), REL_CLIP) + (CHUNK - 1)].astype(F32) * LOG2E
    table = jnp.stack([seq[:, gq - 1 - a:gq - 1 - a + wk] for a in range(gq)], axis=1)
    qc = jnp.arange(gq) // CHUNK
    kc = jnp.arange(wk) // CHUNK
    inband = jnp.logical_and(kc[None, :] >= qc[:, None], kc[None, :] <= qc[:, None] + BAND_CHUNKS)
    return jnp.where(inband[None], table, -jnp.inf).reshape(H_C // 2, 2 * gq, wk)


def _ffn_kernel(*refs, n_mix, nf, nm, tm, tf, final):
    x_ref = refs[0]
    mix_refs = refs[1:1 + n_mix]
    wo_refs = refs[1 + n_mix:1 + 2 * n_mix]
    rest = refs[1 + 2 * n_mix:]
    if final:
        g_ref, wa_ref, wg_ref, cw_ref, cb_ref, wd_ref, st0_ref, gf_ref = rest[:8]
        rest = rest[8:]
    else:
        g_ref, wa_ref, wg_ref, cw_ref, cb_ref, wd_ref, st0_ref = rest[:7]
        gf_ref = None
        rest = rest[7:]
    out_ref, stout_ref, abuf, cbuf = rest
    mi = pl.program_id(1)

    xr = x_ref[0]
    for m_ref, w_ref in zip(mix_refs, wo_refs):
        xr = xr + jnp.dot(m_ref[0], w_ref[...], preferred_element_type=F32)
    hb = _rms(xr, g_ref[...]).astype(BF16)

    @pl.when(mi == 0)
    def _():
        cbuf[...] = st0_ref[0]

    acc = jnp.zeros_like(xr)
    for f in range(nf):
        cols = slice(f * tf, (f + 1) * tf)
        slot = f % 2
        a = jnp.dot(hb, wa_ref[:, cols], preferred_element_type=F32)
        gate = jnp.dot(hb, wg_ref[:, cols], preferred_element_type=F32)
        abuf[slot, 0:8, :] = cbuf[:, cols]
        abuf[slot, 8:8 + tm, :] = a
        ac = (a * cw_ref[2:3, cols] + abuf[slot, 7:7 + tm, :] * cw_ref[1:2, cols]
              + abuf[slot, 6:6 + tm, :] * cw_ref[0:1, cols] + cb_ref[:, cols])
        cbuf[:, cols] = abuf[slot, tm:tm + 8, :]
        u = _silu(ac) * gate
        acc = acc + jnp.dot(u.astype(BF16), wd_ref[cols, :], preferred_element_type=F32)

    yv = xr + acc
    if final:
        yv = _rms(yv, gf_ref[...])
    out_ref[0] = yv

    @pl.when(mi == nm - 1)
    def _():
        stout_ref[0] = cbuf[8 - (CONV_F - 1):8, :]


def _ffn(x, mixes, wos, gain, wa, wg, conv_w, conv_b, wd, st0, gain_final, tm, tf):
    b, l, d = x.shape
    dff = wa.shape[1]
    nf = dff // tf
    nm = l // tm
    n_mix = len(mixes)
    final = gain_final is not None
    in_specs = [pl.BlockSpec((1, tm, d), lambda i, m: (i, m, 0))]
    in_specs += [pl.BlockSpec((1, tm, mx.shape[2]), lambda i, m: (i, m, 0)) for mx in mixes]
    in_specs += [pl.BlockSpec(w.shape, lambda i, m: (0, 0)) for w in wos]
    in_specs += [pl.BlockSpec((1, d), lambda i, m: (0, 0)),
                 pl.BlockSpec((d, dff), lambda i, m: (0, 0)),
                 pl.BlockSpec((d, dff), lambda i, m: (0, 0)),
                 pl.BlockSpec((CONV_F, dff), lambda i, m: (0, 0)),
                 pl.BlockSpec((1, dff), lambda i, m: (0, 0)),
                 pl.BlockSpec((dff, d), lambda i, m: (0, 0)),
                 pl.BlockSpec((1, 8, dff), lambda i, m: (i, 0, 0))]
    args = [x, *mixes, *wos, gain.reshape(1, d), wa, wg, conv_w, conv_b.reshape(1, dff), wd, st0]
    if final:
        in_specs.append(pl.BlockSpec((1, d), lambda i, m: (0, 0)))
        args.append(gain_final.reshape(1, d))
    return pl.pallas_call(
        functools.partial(_ffn_kernel, n_mix=n_mix, nf=nf, nm=nm, tm=tm, tf=tf, final=final),
        out_shape=(jax.ShapeDtypeStruct((b, l, d), F32),
                   jax.ShapeDtypeStruct((b, CONV_F - 1, dff), F32)),
        grid=(b, nm),
        in_specs=in_specs,
        out_specs=(pl.BlockSpec((1, tm, d), lambda i, m: (i, m, 0)),
                   pl.BlockSpec((1, CONV_F - 1, dff), lambda i, m: (i, 0, 0))),
        scratch_shapes=[pltpu.VMEM((2, tm + 8, tf), F32), pltpu.VMEM((8, dff), F32)],
        compiler_params=_cparams(("parallel", "arbitrary")),
        name="ffn",
    )(*args)


def _pad_rows_front(a, rows):
    return jnp.pad(a, ((0, 0), (rows - a.shape[1], 0), (0, 0)))


def _trunk(x, past, w):
    b, l, d = x.shape
    depth = w["norm_mix"].shape[0]
    first = past is None
    n_chunks = l // CHUNK
    tm_proj = min(512, b * l)
    tm_ffn = min(512, l)
    tf = 256
    nc = min(4, n_chunks)
    tb = min(4, n_chunks)
    cg = min(2, tb)
    dff = w["ffn_w_a"].shape[2]
    out = {k: [] for k in ("dn_S", "dn_conv", "dsa_k", "dsa_v", "dsa_kidx", "band_k", "band_v", "ffn_conv")}
    for layer in range(depth):
        x2d = x.reshape(b * l, d)
        if layer % 2 == 0:
            e = layer // 2
            proj = _norm_proj(x2d, w["norm_mix"][layer], w["w_in_even"][e], tm_proj).reshape(b, l, EVEN_COLS)
            if first:
                conv0 = jnp.zeros((b, 8, C_A), F32)
                s0 = jnp.zeros((b, H_A, DK_A, DV_A), F32)
                cache = None
            else:
                conv0 = _pad_rows_front(past["state_dn_conv"][e], 8)
                s0 = past["state_dn_S"][e]
                cache = (past["cache_dsa_k"][e], past["cache_dsa_v"][e], past["cache_dsa_kidx"][e])
            o_a, buf_a, s_a = _deltanet(proj, conv0, s0, w["dn_conv_w"][e], w["dn_a_log"][e],
                                        w["dn_dt_bias"][e], w["dn_o_gain"][e], nc)
            o_b, k_b, v_b, ki_b = _dsa(proj, w["dsa_q_gain"][e], w["dsa_k_gain"][e], cache)
            mixes = [o_a, o_b]
            wos = [w["w_out_even"][e][:H_A * DV_A], w["w_out_even"][e][H_A * DV_A:]]
            out["dn_S"].append(s_a)
            out["dn_conv"].append(buf_a)
            out["dsa_k"].append(k_b)
            out["dsa_v"].append(v_b)
            out["dsa_kidx"].append(ki_b)
        else:
            jj = layer // 2
            proj = _norm_proj(x2d, w["norm_mix"][layer], w["w_in_odd"][jj], tm_proj).reshape(b, l, 3 * H_C * HD_C)
            cache = None if first else (past["cache_band_k"][jj], past["cache_band_v"][jj])
            o_c, kn = _band(proj, w["band_rel_bias"][jj], w["band_q_gain"][jj], w["band_k_gain"][jj], cache, tb, cg)
            keep = min(WINDOW, l) if first else l
            out["band_k"].append(kn[:, l - keep:].reshape(b, keep, H_C, HD_C))
            out["band_v"].append(proj[:, l - keep:, 2 * H_C * HD_C:].reshape(b, keep, H_C, HD_C))
            mixes = [o_c]
            wos = [w["w_out_odd"][jj]]
        st0 = (jnp.zeros((b, 8, dff), F32) if first else _pad_rows_front(past["state_ffn_conv"][layer], 8))
        gfin = w["norm_final"] if layer == depth - 1 else None
        x, fbuf = _ffn(x, mixes, wos, w["norm_ffn"][layer], w["ffn_w_a"][layer], w["ffn_w_g"][layer],
                       w["ffn_conv_w"][layer], w["ffn_conv_b"][layer], w["ffn_w_down"][layer], st0, gfin,
                       tm_ffn, tf)
        out["ffn_conv"].append(fbuf)
    return x, {k: jnp.stack(v) for k, v in out.items()}


def _prep_weights(norm_mix, norm_ffn, norm_final, w_in_even, dn_conv_w, dn_a_log, dn_dt_bias, dn_o_gain,
                  dsa_q_gain, dsa_k_gain, w_out_even, w_in_odd, band_q_gain, band_k_gain, band_rel_bias,
                  w_out_odd, ffn_w_a, ffn_w_g, ffn_conv_w, ffn_conv_b, ffn_w_down):
    o_ba = 4 * H_A * DK_A
    o_qb = o_ba + 2 * H_A
    o_kb = o_qb + H_B * HD_B
    o_qi = o_kb + 2 * HD_B
    o_ki = o_qi + H_IDX * D_IDX
    o_wi = o_ki + D_IDX
    o_end = o_wi + H_IDX
    wie = w_in_even
    pad = jnp.zeros(wie.shape[:2] + (EVEN_COLS - o_end,), wie.dtype)
    w_even = jnp.concatenate([wie[..., :o_ba], wie[..., o_qb:o_kb], wie[..., o_qi:o_ki], wie[..., o_kb:o_qi],
                              wie[..., o_ki:o_wi], wie[..., o_ba:o_qb], wie[..., o_wi:o_end], pad], axis=-1)
    return dict(norm_mix=norm_mix, norm_ffn=norm_ffn, norm_final=norm_final,
                w_in_even=w_even.astype(BF16), dn_conv_w=dn_conv_w, dn_a_log=dn_a_log, dn_dt_bias=dn_dt_bias,
                dn_o_gain=dn_o_gain, dsa_q_gain=dsa_q_gain, dsa_k_gain=dsa_k_gain,
                w_out_even=w_out_even.astype(BF16), w_in_odd=w_in_odd.astype(BF16),
                band_q_gain=band_q_gain, band_k_gain=band_k_gain, band_rel_bias=band_rel_bias,
                w_out_odd=w_out_odd.astype(BF16),
                ffn_w_a=ffn_w_a.astype(BF16), ffn_w_g=ffn_w_g.astype(BF16), ffn_conv_w=ffn_conv_w,
                ffn_conv_b=ffn_conv_b, ffn_w_down=ffn_w_down.astype(BF16))


def kernel(x_prompt, x_sample, state_dn_S, state_dn_conv, cache_dsa_k, cache_dsa_v, cache_dsa_kidx, cache_band_k, cache_band_v, state_ffn_conv, norm_mix, norm_ffn, norm_final, w_in_even, dn_conv_w, dn_a_log, dn_dt_bias, dn_o_gain, dsa_q_gain, dsa_k_gain, w_out_even, w_in_odd, band_q_gain, band_k_gain, band_rel_bias, w_out_odd, ffn_w_a, ffn_w_g, ffn_conv_w, ffn_conv_b, ffn_w_down):
    assert cache_band_k.shape[2] == WINDOW, "band cache must hold exactly one window"
    w = _prep_weights(norm_mix, norm_ffn, norm_final, w_in_even, dn_conv_w, dn_a_log, dn_dt_bias, dn_o_gain,
                      dsa_q_gain, dsa_k_gain, w_out_even, w_in_odd, band_q_gain, band_k_gain, band_rel_bias,
                      w_out_odd, ffn_w_a, ffn_w_g, ffn_conv_w, ffn_conv_b, ffn_w_down)
    past = dict(state_dn_S=state_dn_S, state_dn_conv=state_dn_conv, cache_dsa_k=cache_dsa_k,
                cache_dsa_v=cache_dsa_v, cache_dsa_kidx=cache_dsa_kidx, cache_band_k=cache_band_k,
                cache_band_v=cache_band_v, state_ffn_conv=state_ffn_conv)
    y_p, sp = _trunk(x_prompt, None, w)
    y_s, ss = _trunk(x_sample, past, w)
    return (y_p, y_s,
            sp["dn_S"], ss["dn_S"], sp["dn_conv"], ss["dn_conv"],
            sp["dsa_k"], ss["dsa_k"], sp["dsa_v"], ss["dsa_v"], sp["dsa_kidx"], ss["dsa_kidx"],
            sp["band_k"], ss["band_k"], sp["band_v"], ss["band_v"],
            sp["ffn_conv"], ss["ffn_conv"])
```

```python
import functools
import math

import jax
import jax.numpy as jnp
from jax import lax
from jax.experimental import pallas as pl
from jax.experimental.pallas import tpu as pltpu

F32 = jnp.float32
BF16 = jnp.bfloat16
EPS = 1e-6
INT_MIN = -(2 ** 31)
I16 = jnp.int16
I16_OFFSET = 2 ** 15
LOG2E = math.log2(math.e)

CHUNK = 64
H_A, DK_A, DV_A, CONV_A = 4, 128, 128, 4
C_A = 3 * H_A * DK_A
H_B, HD_B, H_IDX, D_IDX, TOPK_MAX = 8, 64, 4, 64, 256
H_C, HD_C, BAND_CHUNKS, REL_CLIP = 16, 64, 8, 128
WINDOW = BAND_CHUNKS * CHUNK
CONV_F = 3

EVEN_COLS = 3072
COL_Z = 1536
COL_QB = 2048
COL_QI = 2560
COL_KV = 2816
COL_SM = 2944
SM_BETA, SM_A, SM_WI = 64, 68, 72

LANES = 128
VMEM_LIMIT = 48 * 1024 * 1024


def _cparams(sem):
    return pltpu.CompilerParams(dimension_semantics=sem, vmem_limit_bytes=VMEM_LIMIT)


def _dot(a, b):
    return jnp.dot(a.astype(BF16), b.astype(BF16), preferred_element_type=F32)


def _dot_nt(a, b):
    return lax.dot_general(a.astype(BF16), b.astype(BF16), (((1,), (1,)), ((), ())),
                           preferred_element_type=F32)


def _dot_f32(a, b):
    return jnp.dot(a, b, preferred_element_type=F32, precision=lax.Precision.HIGHEST)


def _dot_nt_f32(a, b):
    return lax.dot_general(a, b, (((1,), (1,)), ((), ())), preferred_element_type=F32,
                           precision=lax.Precision.HIGHEST)


def _rms(x, gain):
    return x * lax.rsqrt(jnp.mean(x * x, axis=-1, keepdims=True) + EPS) * gain


def _silu(x):
    return x * jax.nn.sigmoid(x)


def _low_half(shape):
    return lax.broadcasted_iota(jnp.int32, shape, len(shape) - 1) < (LANES // 2)


def _pair_rms(x, gain2):
    lo = _low_half(x.shape)
    sq = x * x
    ms_lo = jnp.sum(jnp.where(lo, sq, 0.0), axis=-1, keepdims=True) * (2.0 / LANES)
    ms_hi = jnp.sum(jnp.where(lo, 0.0, sq), axis=-1, keepdims=True) * (2.0 / LANES)
    return x * jnp.where(lo, lax.rsqrt(ms_lo + EPS), lax.rsqrt(ms_hi + EPS)) * gain2


def _split_halves(x):
    lo = _low_half(x.shape)
    return jnp.concatenate([jnp.where(lo, x, 0.0), jnp.where(lo, 0.0, x)], axis=0)


def _join_halves(o, rows):
    return jnp.where(_low_half((rows, LANES)), o[0:rows], o[rows:2 * rows])


def _exp2_weights(s):
    return jnp.exp2(s - jnp.max(s, axis=-1, keepdims=True)).astype(BF16)


def _dup(x):
    return jnp.concatenate([x, x], axis=1)


def _norm_proj_kernel(x_ref, g_ref, w_ref, o_ref):
    o_ref[...] = jnp.dot(_rms(x_ref[...], g_ref[...]).astype(BF16), w_ref[...], preferred_element_type=F32)


def _norm_proj(x2d, gain, w_bf16, tm):
    m, d = x2d.shape
    n = w_bf16.shape[1]
    return pl.pallas_call(
        _norm_proj_kernel,
        out_shape=jax.ShapeDtypeStruct((m, n), F32),
        grid=(m // tm,),
        in_specs=[pl.BlockSpec((tm, d), lambda i: (i, 0)),
                  pl.BlockSpec((1, d), lambda i: (0, 0)),
                  pl.BlockSpec((d, n), lambda i: (0, 0))],
        out_specs=pl.BlockSpec((tm, n), lambda i: (i, 0)),
        compiler_params=_cparams(("parallel",)),
        name="norm_proj",
    )(x2d, gain.reshape(1, d), w_bf16)


def _tri_inv_lockstep(ms, eye_f):
    ts = [eye_f - m for m in ms]
    pbs = [(-m).astype(BF16) for m in ms]
    for _ in range(5):
        pbs = [jnp.dot(pb, pb, preferred_element_type=F32).astype(BF16) for pb in pbs]
        ts = [t + jnp.dot(t.astype(BF16), pb, preferred_element_type=F32) for t, pb in zip(ts, pbs)]
    return ts


def _deltanet_kernel(qkv_ref, z_ref, sm_ref, conv0_ref, s0_ref, cw_ref, alog_ref, dtb_ref, og_ref,
                     o_ref, convout_ref, sout_ref, xbuf, s_scr, *, nc):
    c = pl.program_id(1)
    t = nc * CHUNK

    @pl.when(c == 0)
    def _():
        xbuf[0:8, :] = conv0_ref[0]
        for h in range(H_A):
            s_scr[:, h * DK_A:(h + 1) * DK_A] = jnp.transpose(s0_ref[0, h])

    x = qkv_ref[0]
    xbuf[8:8 + t, :] = x
    w = cw_ref[...]
    y = (x * w[3:4] + xbuf[7:7 + t, :] * w[2:3] + xbuf[6:6 + t, :] * w[1:2] + xbuf[5:5 + t, :] * w[0:1])
    convout_ref[0] = xbuf[t + 5:t + 8, :]
    xbuf[0:8, :] = xbuf[t:t + 8, :]
    y = _silu(y)

    sm = sm_ref[0]
    beta = jax.nn.sigmoid(sm)
    aa = sm + dtb_ref[...]
    softplus = jnp.maximum(aa, 0.0) + jnp.log1p(jnp.exp(-jnp.abs(aa)))
    g = -jnp.exp(alog_ref[...]) * softplus

    ht = H_A * CHUNK
    hk = H_A * DK_A
    ri = lax.broadcasted_iota(jnp.int32, (ht, ht), 0)
    ci = lax.broadcasted_iota(jnp.int32, (ht, ht), 1)
    same = jnp.right_shift(ri, 6) == jnp.right_shift(ci, 6)
    trilbd = jnp.logical_and(same, ri >= ci)
    strictbd = jnp.logical_and(same, ri > ci)
    eye_f = (ri == ci).astype(F32)
    r2 = lax.broadcasted_iota(jnp.int32, (2 * ht, hk), 0)
    c2 = lax.broadcasted_iota(jnp.int32, (2 * ht, hk), 1)
    bd2 = (jnp.right_shift(r2, 6) & (H_A - 1)) == jnp.right_shift(c2, 7)
    bd1 = bd2[0:ht]
    r64 = lax.broadcasted_iota(jnp.int32, (CHUNK, CHUNK), 0)
    c64 = lax.broadcasted_iota(jnp.int32, (CHUNK, CHUNK), 1)
    tril_f = (r64 >= c64).astype(F32)
    og = og_ref[...]

    def heads_on_rows(a, r0, off, width):
        return jnp.concatenate([a[r0:r0 + CHUNK, off + h * width:off + (h + 1) * width] for h in range(H_A)],
                               axis=0)

    pre = []
    for cidx in range(nc):
        r0 = cidx * CHUNK
        gcum = _dot_f32(tril_f, g[r0:r0 + CHUNK])
        q = heads_on_rows(y, r0, 0, DK_A)
        k = heads_on_rows(y, r0, hk, DK_A)
        v = heads_on_rows(y, r0, 2 * hk, DV_A)
        q = q * lax.rsqrt(jnp.sum(q * q, axis=-1, keepdims=True) + EPS) * (DK_A ** -0.5)
        k = k * lax.rsqrt(jnp.sum(k * k, axis=-1, keepdims=True) + EPS)
        bcol = heads_on_rows(beta, r0, SM_BETA, 1)
        gcol = heads_on_rows(gcum, 0, SM_A, 1)
        glast = [gcum[CHUNK - 1:CHUNK, SM_A + h:SM_A + h + 1] for h in range(H_A)]
        gl_col = jnp.concatenate([jnp.broadcast_to(x_, (CHUNK, 1)) for x_ in glast], axis=0)
        gl_row = jnp.concatenate([jnp.broadcast_to(x_, (1, DK_A)) for x_ in glast], axis=1)
        grow = jnp.transpose(jnp.broadcast_to(gcol, (ht, LANES)))[0:1, :]
        decay = jnp.exp(jnp.where(trilbd, gcol - grow, -jnp.inf))
        kb = k * bcol
        a_full = _dot_nt(jnp.concatenate([kb, q], axis=0), k)
        kd = k * jnp.exp(gl_col - gcol)
        pre.append(dict(
            m=jnp.where(strictbd, a_full[0:ht] * decay, 0.0),
            attn=(a_full[ht:2 * ht] * decay).astype(BF16),
            rhs=jnp.concatenate([v * bcol, kb * jnp.exp(gcol)], axis=1).astype(BF16),
            qg=q * jnp.exp(gcol),
            kd_bd=jnp.where(bd1, jnp.concatenate([kd] * H_A, axis=1), 0.0).astype(BF16),
            decay_row=jnp.exp(gl_row)))

    tmats = _tri_inv_lockstep([c["m"] for c in pre], eye_f)
    uws = [jnp.dot(tm.astype(BF16), c["rhs"], preferred_element_type=F32) for tm, c in zip(tmats, pre)]

    for cidx in range(nc):
        r0 = cidx * CHUNK
        c, uw = pre[cidx], uws[cidx]
        u = uw[:, 0:DV_A]
        wq = jnp.concatenate([uw[:, DV_A:DV_A + DK_A], c["qg"]], axis=0)
        wq_bd = jnp.where(bd2, jnp.concatenate([wq] * H_A, axis=1), 0.0)
        st_old = s_scr[...]
        ws_qs = _dot_nt(wq_bd, st_old)
        v_new = u - ws_qs[0:ht]
        s_scr[...] = st_old * c["decay_row"] + _dot(jnp.transpose(v_new), c["kd_bd"])
        o = ws_qs[ht:2 * ht] + _dot(c["attn"], v_new)
        zz = heads_on_rows(z_ref[0], r0, 0, DV_A)
        res = (_rms(o, og) * _silu(zz)).astype(o_ref.dtype)
        for h in range(H_A):
            o_ref[0, r0:r0 + CHUNK, h * DV_A:(h + 1) * DV_A] = res[h * CHUNK:(h + 1) * CHUNK]

    for h in range(H_A):
        sout_ref[0, h] = jnp.transpose(s_scr[:, h * DK_A:(h + 1) * DK_A])


def _deltanet(proj, conv0, s0, conv_w, a_log, dt_bias, o_gain, nc):
    b, l, _ = proj.shape
    t = nc * CHUNK
    pad = jnp.zeros((1, LANES), F32)
    alog128 = lax.dynamic_update_slice(pad, a_log.reshape(1, H_A), (0, SM_A))
    dtb128 = lax.dynamic_update_slice(pad, dt_bias.reshape(1, H_A), (0, SM_A))
    return pl.pallas_call(
        functools.partial(_deltanet_kernel, nc=nc),
        out_shape=(jax.ShapeDtypeStruct((b, l, H_A * DV_A), BF16),
                   jax.ShapeDtypeStruct((b, CONV_A - 1, C_A), F32),
                   jax.ShapeDtypeStruct((b, H_A, DK_A, DV_A), F32)),
        grid=(b, l // t),
        in_specs=[pl.BlockSpec((1, t, C_A), lambda i, c: (i, c, 0)),
                  pl.BlockSpec((1, t, 512), lambda i, c: (i, c, COL_Z // 512)),
                  pl.BlockSpec((1, t, LANES), lambda i, c: (i, c, COL_SM // LANES)),
                  pl.BlockSpec((1, 8, C_A), lambda i, c: (i, 0, 0)),
                  pl.BlockSpec((1, H_A, DK_A, DV_A), lambda i, c: (i, 0, 0, 0)),
                  pl.BlockSpec((CONV_A, C_A), lambda i, c: (0, 0)),
                  pl.BlockSpec((1, LANES), lambda i, c: (0, 0)),
                  pl.BlockSpec((1, LANES), lambda i, c: (0, 0)),
                  pl.BlockSpec((1, DV_A), lambda i, c: (0, 0))],
        out_specs=(pl.BlockSpec((1, t, H_A * DV_A), lambda i, c: (i, c, 0)),
                   pl.BlockSpec((1, CONV_A - 1, C_A), lambda i, c: (i, 0, 0)),
                   pl.BlockSpec((1, H_A, DK_A, DV_A), lambda i, c: (i, 0, 0, 0))),
        scratch_shapes=[pltpu.VMEM((t + 8, C_A), F32), pltpu.VMEM((DV_A, H_A * DK_A), F32)],
        compiler_params=_cparams(("parallel", "arbitrary")),
        name="deltanet",
    )(proj, proj, proj, conv0, s0, conv_w, alog128, dtb128, o_gain.reshape(1, DV_A))


def _dsa_prep_kernel(*refs, l, p, s_pad):
    if p:
        kv_ref, sm_ref, kg_ref, ck_ref, cv_ref, cki_ref, kout_ref, vout_ref, kiout_ref, k2_ref, v2_ref, ki2_ref = refs
    else:
        kv_ref, sm_ref, kg_ref, kout_ref, vout_ref, kiout_ref, k2_ref, v2_ref, ki2_ref = refs
    kv = kv_ref[0]
    kn = _rms(kv[:, 0:HD_B], kg_ref[...])
    vraw = kv[:, HD_B:2 * HD_B]
    kidx = sm_ref[0][:, 0:D_IDX]
    kout_ref[0] = kn
    vout_ref[0] = vraw
    kiout_ref[0] = kidx
    def with_ones(a):
        return jnp.concatenate([a, jnp.ones_like(a)], axis=1)

    if p:
        k2_ref[0, 0:p, :] = _dup(ck_ref[0]).astype(BF16)
        v2_ref[0, 0:p, :] = with_ones(cv_ref[0]).astype(BF16)
        ki2_ref[0, 0:p, :] = _dup(cki_ref[0]).astype(BF16)
    k2_ref[0, p:p + l, :] = _dup(kn).astype(BF16)
    v2_ref[0, p:p + l, :] = with_ones(vraw).astype(BF16)
    ki2_ref[0, p:p + l, :] = _dup(kidx).astype(BF16)
    if s_pad > p + l:
        zpad = jnp.zeros((s_pad - p - l, LANES), BF16)
        k2_ref[0, p + l:s_pad, :] = zpad
        v2_ref[0, p + l:s_pad, :] = zpad
        ki2_ref[0, p + l:s_pad, :] = zpad


def _colsum(x):
    s, n = x.shape
    return jnp.sum(jnp.sum(x.reshape(s // CHUNK, CHUNK, n), axis=0), axis=0, keepdims=True)


def _colsum16(x):
    s = x.shape[0]
    part = x[0:LANES]
    for i in range(1, s // LANES):
        part = part + x[i * LANES:(i + 1) * LANES]
    return jnp.sum(part.astype(F32), axis=0, keepdims=True)


def _radix_select16(x_scr, k_need, tq):
    def count_ge(cand):
        return _colsum16(jnp.where(x_scr[...] >= cand.astype(I16), I16(1), I16(0)))

    t0 = jnp.where(count_ge(jnp.zeros((1, tq), jnp.int32)) >= k_need, 0, -I16_OFFSET).astype(jnp.int32)

    def body(i, t):
        cand = t | jnp.left_shift(jnp.int32(1), 14 - i)
        return jnp.where(count_ge(cand) >= k_need, cand, t)

    return lax.fori_loop(0, 15, body, t0)


def _dsa_kernel(qb_ref, qi_ref, smq_ref, qg_ref, k2_ref, v2_ref, ki2_ref, o_ref, key_scr, hi_scr, lo_scr, lim_scr,
                *, tq, tile0, p, s_keys, k_sel):
    tg = tile0 + pl.program_id(1)

    smq = smq_ref[0]
    sel_r = lax.broadcasted_iota(jnp.int32, (8, LANES), 0)
    sel_c = lax.broadcasted_iota(jnp.int32, (8, LANES), 1)
    pick = (sel_c == sel_r + SM_WI).astype(F32)
    wi_t = _dot_nt_f32(pick, smq) * (H_IDX ** -0.5)
    qi = qi_ref[0]
    ki_all = ki2_ref[0]
    qheads = jnp.concatenate([_split_halves(qi[:, pr * LANES:(pr + 1) * LANES]) for pr in range(H_IDX // 2)],
                             axis=0)
    score = jnp.zeros((s_keys, tq), F32)
    if tq % LANES == 0:
        d_all = _dot_nt(ki_all, qheads)
        for h in range(H_IDX):
            score = score + jnp.maximum(d_all[:, h * tq:(h + 1) * tq] * (D_IDX ** -0.5), 0.0) * wi_t[h:h + 1, :]
    else:
        for h in range(H_IDX):
            d = _dot_nt(ki_all, qheads[h * tq:(h + 1) * tq])
            score = score + jnp.maximum(d * (D_IDX ** -0.5), 0.0) * wi_t[h:h + 1, :]

    bits = pltpu.bitcast(score + 0.0, jnp.int32)
    key = bits ^ (jnp.right_shift(bits, 31) & 0x7FFFFFFF)
    kpos = lax.broadcasted_iota(jnp.int32, (s_keys, tq), 0)
    qpos = tg * tq + lax.broadcasted_iota(jnp.int32, (s_keys, tq), 1)
    adm = kpos < p + (jnp.right_shift(qpos, 6) + 1) * CHUNK
    key_scr[...] = jnp.where(adm, key, INT_MIN)

    kf = float(k_sel)
    hi_scr[...] = jnp.right_shift(key_scr[...], 16).astype(I16)
    t_hi = _radix_select16(hi_scr, kf, tq)
    hi = hi_scr[...]
    t_hi16 = t_hi.astype(I16)
    need_lo = kf - _colsum16(jnp.where(hi > t_hi16, I16(1), I16(0)))
    lo_half = ((key_scr[...] & 0xFFFF) - I16_OFFSET).astype(I16)
    lo_scr[...] = jnp.where(hi == t_hi16, lo_half, I16(-I16_OFFSET))
    t_lo = _radix_select16(lo_scr, need_lo, tq)
    thr = jnp.left_shift(t_hi, 16) | (t_lo + I16_OFFSET)

    key = key_scr[...]
    gt = key > thr
    eq = jnp.logical_and(key == thr, adm)
    eqf = jnp.where(eq, 1.0, 0.0)
    need = kf - _colsum(jnp.where(gt, 1.0, 0.0))
    surplus = _colsum(eqf) - need
    nbits = int(s_keys).bit_length()
    lim_scr[...] = jnp.full(lim_scr.shape, 2 ** nbits - 1, jnp.int32)

    @pl.when(jnp.max(surplus) > 0.0)
    def _():
        lo_scr[...] = jnp.where(eq, 1, 0).astype(I16)
        hi_scr[...] = kpos.astype(I16)

        def tie_body(i, lim):
            cand = lim | jnp.left_shift(jnp.int32(1), nbits - 1 - i)
            cnt = _colsum16(jnp.where(hi_scr[...] < cand.astype(I16), lo_scr[...], I16(0)))
            return jnp.where(cnt <= need, cand, lim)

        lim = lax.fori_loop(0, nbits, tie_body, jnp.zeros((1, tq), jnp.int32))
        lim_scr[...] = jnp.broadcast_to(lim, lim_scr.shape)

    lim = lim_scr[0:1, :]
    sel_t = jnp.where(jnp.logical_or(gt, jnp.logical_and(eq, kpos < lim)), 1.0, 0.0).astype(BF16)

    er = lax.broadcasted_iota(jnp.int32, (tq, tq), 0)
    ec = lax.broadcasted_iota(jnp.int32, (tq, tq), 1)
    sel = _dot_nt((er == ec).astype(BF16), sel_t)
    mask_bias = jnp.where(sel > 0.5, 0.0, -jnp.inf)

    qb = qb_ref[0]
    k_all = k2_ref[0]
    v_all = v2_ref[0]
    lo = _low_half((tq, LANES))
    outs = []
    for pr in range(H_B // 2):
        qn = _pair_rms(qb[:, pr * LANES:(pr + 1) * LANES], qg_ref[...]) * (HD_B ** -0.5 * LOG2E)
        s = _dot_nt(_split_halves(qn), k_all)
        s = (s.reshape(2, tq, s_keys) + mask_bias[None]).reshape(2 * tq, s_keys)
        r = _dot(_exp2_weights(s), v_all)
        o = r / pltpu.roll(r, LANES // 2, 1)
        outs.append(jnp.where(lo, o[0:tq], pltpu.roll(o[tq:2 * tq], LANES // 2, 1)))
    o_ref[0] = jnp.concatenate(outs, axis=1).astype(o_ref.dtype)


def _dsa(proj, q_gain, k_gain, cache):
    b, l, _ = proj.shape
    p = 0 if cache is None else cache[0].shape[1]
    tq = min(LANES, l)
    s_k = p + l
    s_pad = -(-s_k // LANES) * LANES
    k_sel = min(TOPK_MAX, s_k // 4)

    kv_specs = [pl.BlockSpec((1, l, LANES), lambda i: (i, 0, COL_KV // LANES)),
                pl.BlockSpec((1, l, LANES), lambda i: (i, 0, COL_SM // LANES)),
                pl.BlockSpec((1, HD_B), lambda i: (0, 0))]
    args = [proj, proj, k_gain.reshape(1, HD_B)]
    if p:
        kv_specs += [pl.BlockSpec((1, p, HD_B), lambda i: (i, 0, 0))] * 3
        args += list(cache)
    row_out = jax.ShapeDtypeStruct((b, l, HD_B), F32)
    dup_out = jax.ShapeDtypeStruct((b, s_pad, LANES), BF16)
    row_spec = pl.BlockSpec((1, l, HD_B), lambda i: (i, 0, 0))
    dup_spec = pl.BlockSpec((1, s_pad, LANES), lambda i: (i, 0, 0))
    k_b, v_b, ki_b, k2, v2, ki2 = pl.pallas_call(
        functools.partial(_dsa_prep_kernel, l=l, p=p, s_pad=s_pad),
        out_shape=(row_out, row_out, row_out, dup_out, dup_out, dup_out),
        grid=(b,),
        in_specs=kv_specs,
        out_specs=(row_spec, row_spec, row_spec, dup_spec, dup_spec, dup_spec),
        compiler_params=_cparams(("parallel",)),
        name="dsa_prep",
    )(*args)

    n_tiles = l // tq
    tiles_per_class = max(1, 256 // tq) if p == 0 else n_tiles
    qg2 = _dup(q_gain.reshape(1, HD_B))
    outs = []
    for tile0 in range(0, n_tiles, tiles_per_class):
        nt = min(tiles_per_class, n_tiles - tile0)
        s_keys = min(s_pad, -(-(p + (tile0 + nt) * tq) // LANES) * LANES)
        outs.append(pl.pallas_call(
            functools.partial(_dsa_kernel, tq=tq, tile0=tile0, p=p, s_keys=s_keys, k_sel=k_sel),
            out_shape=jax.ShapeDtypeStruct((b, nt * tq, H_B * HD_B), BF16),
            grid=(b, nt),
            in_specs=[pl.BlockSpec((1, tq, 512), lambda i, t, t0=tile0: (i, t0 + t, COL_QB // 512)),
                      pl.BlockSpec((1, tq, 256), lambda i, t, t0=tile0: (i, t0 + t, COL_QI // 256)),
                      pl.BlockSpec((1, tq, LANES), lambda i, t, t0=tile0: (i, t0 + t, COL_SM // LANES)),
                      pl.BlockSpec((1, LANES), lambda i, t: (0, 0)),
                      pl.BlockSpec((1, s_keys, LANES), lambda i, t: (i, 0, 0)),
                      pl.BlockSpec((1, s_keys, LANES), lambda i, t: (i, 0, 0)),
                      pl.BlockSpec((1, s_keys, LANES), lambda i, t: (i, 0, 0))],
            out_specs=pl.BlockSpec((1, tq, H_B * HD_B), lambda i, t: (i, t, 0)),
            scratch_shapes=[pltpu.VMEM((s_keys, tq), jnp.int32), pltpu.VMEM((s_keys, tq), I16),
                            pltpu.VMEM((s_keys, tq), I16), pltpu.VMEM((8, tq), jnp.int32)],
            compiler_params=_cparams(("parallel", "arbitrary")),
            name="dsa",
        )(proj, proj, proj, qg2, k2, v2, ki2))
    o_b = outs[0] if len(outs) == 1 else jnp.concatenate(outs, axis=1)
    return o_b, k_b, v_b, ki_b


def _band_kernel(*refs, tb, cg, has_cache):
    if has_cache:
        q_ref, k_ref, v_ref, bias_ref, qg_ref, kg_ref, ck_ref, cv_ref, o_ref, kn_ref, kwin, vwin = refs
    else:
        q_ref, k_ref, v_ref, bias_ref, qg_ref, kg_ref, o_ref, kn_ref, kwin, vwin = refs
    j = pl.program_id(1)
    tbq = tb * CHUNK
    gq = cg * CHUNK
    wk = WINDOW + gq

    @pl.when(j == 0)
    def _():
        if has_cache:
            kwin[0:WINDOW, :] = ck_ref[0].astype(BF16)
            vwin[0:WINDOW, :] = cv_ref[0].astype(BF16)
        else:
            kwin[0:WINDOW, :] = jnp.zeros((WINDOW, H_C * HD_C), BF16)
            vwin[0:WINDOW, :] = jnp.zeros((WINDOW, H_C * HD_C), BF16)

    @pl.when(j > 0)
    def _():
        for i in range(WINDOW // tbq):
            kwin[i * tbq:(i + 1) * tbq, :] = kwin[(i + 1) * tbq:(i + 2) * tbq, :]
            vwin[i * tbq:(i + 1) * tbq, :] = vwin[(i + 1) * tbq:(i + 2) * tbq, :]

    vwin[WINDOW:WINDOW + tbq, :] = v_ref[0].astype(BF16)
    kcol = lax.broadcasted_iota(jnp.int32, (2 * gq, wk), 1)
    n_inv = jnp.maximum(WINDOW - j * tbq, 0)
    for i in range(H_C // 2):
        cols = slice(i * LANES, (i + 1) * LANES)
        kn = _pair_rms(k_ref[0, :, cols], kg_ref[...])
        kn_ref[0, :, cols] = kn
        kwin[WINDOW:WINDOW + tbq, cols] = kn.astype(BF16)
        qn = _pair_rms(q_ref[0, :, cols], qg_ref[...]) * (HD_C ** -0.5 * LOG2E)
        for g in range(tb // cg):
            r0 = g * gq
            s = _dot_nt(_split_halves(qn[r0:r0 + gq]), kwin[r0:r0 + wk, cols]) + bias_ref[i]
            if not has_cache:
                s = jnp.where(kcol >= n_inv - r0, s, -jnp.inf)
            m = jnp.max(s, axis=-1, keepdims=True)
            e = jnp.exp2(s - m)
            o = _dot(e, vwin[r0:r0 + wk, cols]) / jnp.sum(e, axis=-1, keepdims=True)
            o_ref[0, r0:r0 + gq, cols] = _join_halves(o, gq).astype(o_ref.dtype)


def _band(proj, rel_bias, q_gain, k_gain, cache, tb, cg):
    b, l, _ = proj.shape
    d = H_C * HD_C
    tbq = tb * CHUNK
    gq = cg * CHUNK
    wk = WINDOW + gq
    has_cache = cache is not None
    in_specs = [pl.BlockSpec((1, tbq, d), lambda i, j: (i, j, 0)),
                pl.BlockSpec((1, tbq, d), lambda i, j: (i, j, 1)),
                pl.BlockSpec((1, tbq, d), lambda i, j: (i, j, 2)),
                pl.BlockSpec((H_C // 2, 2 * gq, wk), lambda i, j: (0, 0, 0)),
                pl.BlockSpec((1, LANES), lambda i, j: (0, 0)),
                pl.BlockSpec((1, LANES), lambda i, j: (0, 0))]
    args = [proj, proj, proj, _band_bias(rel_bias, cg), _dup(q_gain.reshape(1, HD_C)), _dup(k_gain.reshape(1, HD_C))]
    if has_cache:
        in_specs += [pl.BlockSpec((1, WINDOW, d), lambda i, j: (i, 0, 0))] * 2
        args += [cache[0].reshape(b, WINDOW, d), cache[1].reshape(b, WINDOW, d)]
    return pl.pallas_call(
        functools.partial(_band_kernel, tb=tb, cg=cg, has_cache=has_cache),
        out_shape=(jax.ShapeDtypeStruct((b, l, d), BF16), jax.ShapeDtypeStruct((b, l, d), F32)),
        grid=(b, l // tbq),
        in_specs=in_specs,
        out_specs=(pl.BlockSpec((1, tbq, d), lambda i, j: (i, j, 0)),
                   pl.BlockSpec((1, tbq, d), lambda i, j: (i, j, 0))),
        scratch_shapes=[pltpu.VMEM((WINDOW + tbq, d), BF16), pltpu.VMEM((WINDOW + tbq, d), BF16)],
        compiler_params=_cparams(("parallel", "arbitrary")),
        name="band",
    )(*args)


def _band_bias(rel_bias, cg):
    gq = cg * CHUNK
    wk = WINDOW + gq
    n = wk + gq - 1
    dist = WINDOW + gq - 1 - jnp.arange(n)
    seq = rel_bias[:, jnp.clip(dist, -(CHUNK - 1), REL_CLIP) + (CHUNK - 1)].astype(F32) * LOG2E
    period = jnp.roll(jnp.pad(seq, ((0, 0), (0, 1))), -(gq - 1), axis=1)
    table = jnp.tile(period, (1, gq))[:, :gq * n].reshape(H_C, gq, n)[:, :, :wk]
    qc = jnp.arange(gq) // CHUNK
    kc = jnp.arange(wk) // CHUNK
    inband = jnp.logical_and(kc[None, :] >= qc[:, None], kc[None, :] <= qc[:, None] + BAND_CHUNKS)
    return jnp.where(inband[None], table, -jnp.inf).reshape(H_C // 2, 2 * gq, wk)


def _ffn_kernel(*refs, n_mix, nf, nm, tm, tf, final):
    x_ref = refs[0]
    mix_refs = refs[1:1 + n_mix]
    wo_refs = refs[1 + n_mix:1 + 2 * n_mix]
    rest = refs[1 + 2 * n_mix:]
    if final:
        g_ref, wa_ref, wg_ref, cw_ref, cb_ref, wd_ref, st0_ref, gf_ref = rest[:8]
        rest = rest[8:]
    else:
        g_ref, wa_ref, wg_ref, cw_ref, cb_ref, wd_ref, st0_ref = rest[:7]
        gf_ref = None
        rest = rest[7:]
    out_ref, stout_ref, abuf, cbuf = rest
    mi = pl.program_id(1)

    xr = x_ref[0]
    for m_ref, w_ref in zip(mix_refs, wo_refs):
        xr = xr + jnp.dot(m_ref[0], w_ref[...], preferred_element_type=F32)
    hb = _rms(xr, g_ref[...]).astype(BF16)

    @pl.when(mi == 0)
    def _():
        cbuf[...] = st0_ref[0]

    acc = jnp.zeros_like(xr)
    for f in range(nf):
        cols = slice(f * tf, (f + 1) * tf)
        slot = f % 2
        a = jnp.dot(hb, wa_ref[:, cols], preferred_element_type=F32)
        gate = jnp.dot(hb, wg_ref[:, cols], preferred_element_type=F32)
        abuf[slot, 0:8, :] = cbuf[:, cols]
        abuf[slot, 8:8 + tm, :] = a
        ac = (a * cw_ref[2:3, cols] + abuf[slot, 7:7 + tm, :] * cw_ref[1:2, cols]
              + abuf[slot, 6:6 + tm, :] * cw_ref[0:1, cols] + cb_ref[:, cols])
        cbuf[:, cols] = abuf[slot, tm:tm + 8, :]
        u = _silu(ac) * gate
        acc = acc + jnp.dot(u.astype(BF16), wd_ref[cols, :], preferred_element_type=F32)

    yv = xr + acc
    if final:
        yv = _rms(yv, gf_ref[...])
    out_ref[0] = yv

    @pl.when(mi == nm - 1)
    def _():
        stout_ref[0] = cbuf[8 - (CONV_F - 1):8, :]


def _ffn(x, mixes, wos, gain, wa, wg, conv_w, conv_b, wd, st0, gain_final, tm, tf):
    b, l, d = x.shape
    dff = wa.shape[1]
    nf = dff // tf
    nm = l // tm
    n_mix = len(mixes)
    final = gain_final is not None
    in_specs = [pl.BlockSpec((1, tm, d), lambda i, m: (i, m, 0))]
    in_specs += [pl.BlockSpec((1, tm, mx.shape[2]), lambda i, m: (i, m, 0)) for mx in mixes]
    in_specs += [pl.BlockSpec(w.shape, lambda i, m: (0, 0)) for w in wos]
    in_specs += [pl.BlockSpec((1, d), lambda i, m: (0, 0)),
                 pl.BlockSpec((d, dff), lambda i, m: (0, 0)),
                 pl.BlockSpec((d, dff), lambda i, m: (0, 0)),
                 pl.BlockSpec((CONV_F, dff), lambda i, m: (0, 0)),
                 pl.BlockSpec((1, dff), lambda i, m: (0, 0)),
                 pl.BlockSpec((dff, d), lambda i, m: (0, 0)),
                 pl.BlockSpec((1, 8, dff), lambda i, m: (i, 0, 0))]
    args = [x, *mixes, *wos, gain.reshape(1, d), wa, wg, conv_w, conv_b.reshape(1, dff), wd, st0]
    if final:
        in_specs.append(pl.BlockSpec((1, d), lambda i, m: (0, 0)))
        args.append(gain_final.reshape(1, d))
    return pl.pallas_call(
        functools.partial(_ffn_kernel, n_mix=n_mix, nf=nf, nm=nm, tm=tm, tf=tf, final=final),
        out_shape=(jax.ShapeDtypeStruct((b, l, d), F32),
                   jax.ShapeDtypeStruct((b, CONV_F - 1, dff), F32)),
        grid=(b, nm),
        in_specs=in_specs,
        out_specs=(pl.BlockSpec((1, tm, d), lambda i, m: (i, m, 0)),
                   pl.BlockSpec((1, CONV_F - 1, dff), lambda i, m: (i, 0, 0))),
        scratch_shapes=[pltpu.VMEM((2, tm + 8, tf), F32), pltpu.VMEM((8, dff), F32)],
        compiler_params=_cparams(("parallel", "arbitrary")),
        name="ffn",
    )(*args)


def _pad_rows_front(a, rows):
    return jnp.pad(a, ((0, 0), (rows - a.shape[1], 0), (0, 0)))


def _trunk(x, past, w):
    b, l, d = x.shape
    depth = w["norm_mix"].shape[0]
    first = past is None
    n_chunks = l // CHUNK
    tm_proj = min(512, b * l)
    tm_ffn = min(512, l)
    tf = 256
    nc = min(4, n_chunks)
    tb = min(8, n_chunks)
    cg = min(2, tb)
    dff = w["ffn_w_a"].shape[2]
    out = {k: [] for k in ("dn_S", "dn_conv", "dsa_k", "dsa_v", "dsa_kidx", "band_k", "band_v", "ffn_conv")}
    for layer in range(depth):
        x2d = x.reshape(b * l, d)
        if layer % 2 == 0:
            e = layer // 2
            proj = _norm_proj(x2d, w["norm_mix"][layer], w["w_in_even"][e], tm_proj).reshape(b, l, EVEN_COLS)
            if first:
                conv0 = jnp.zeros((b, 8, C_A), F32)
                s0 = jnp.zeros((b, H_A, DK_A, DV_A), F32)
                cache = None
            else:
                conv0 = _pad_rows_front(past["state_dn_conv"][e], 8)
                s0 = past["state_dn_S"][e]
                cache = (past["cache_dsa_k"][e], past["cache_dsa_v"][e], past["cache_dsa_kidx"][e])
            o_a, buf_a, s_a = _deltanet(proj, conv0, s0, w["dn_conv_w"][e], w["dn_a_log"][e],
                                        w["dn_dt_bias"][e], w["dn_o_gain"][e], nc)
            o_b, k_b, v_b, ki_b = _dsa(proj, w["dsa_q_gain"][e], w["dsa_k_gain"][e], cache)
            mixes = [o_a, o_b]
            wos = [w["w_out_even"][e][:H_A * DV_A], w["w_out_even"][e][H_A * DV_A:]]
            out["dn_S"].append(s_a)
            out["dn_conv"].append(buf_a)
            out["dsa_k"].append(k_b)
            out["dsa_v"].append(v_b)
            out["dsa_kidx"].append(ki_b)
        else:
            jj = layer // 2
            proj = _norm_proj(x2d, w["norm_mix"][layer], w["w_in_odd"][jj], tm_proj).reshape(b, l, 3 * H_C * HD_C)
            cache = None if first else (past["cache_band_k"][jj], past["cache_band_v"][jj])
            o_c, kn = _band(proj, w["band_rel_bias"][jj], w["band_q_gain"][jj], w["band_k_gain"][jj], cache, tb, cg)
            keep = min(WINDOW, l) if first else l
            out["band_k"].append(kn[:, l - keep:].reshape(b, keep, H_C, HD_C))
            out["band_v"].append(proj[:, l - keep:, 2 * H_C * HD_C:].reshape(b, keep, H_C, HD_C))
            mixes = [o_c]
            wos = [w["w_out_odd"][jj]]
        st0 = (jnp.zeros((b, 8, dff), F32) if first else _pad_rows_front(past["state_ffn_conv"][layer], 8))
        gfin = w["norm_final"] if layer == depth - 1 else None
        x, fbuf = _ffn(x, mixes, wos, w["norm_ffn"][layer], w["ffn_w_a"][layer], w["ffn_w_g"][layer],
                       w["ffn_conv_w"][layer], w["ffn_conv_b"][layer], w["ffn_w_down"][layer], st0, gfin,
                       tm_ffn, tf)
        out["ffn_conv"].append(fbuf)
    return x, {k: jnp.stack(v) for k, v in out.items()}


def _prep_weights(norm_mix, norm_ffn, norm_final, w_in_even, dn_conv_w, dn_a_log, dn_dt_bias, dn_o_gain,
                  dsa_q_gain, dsa_k_gain, w_out_even, w_in_odd, band_q_gain, band_k_gain, band_rel_bias,
                  w_out_odd, ffn_w_a, ffn_w_g, ffn_conv_w, ffn_conv_b, ffn_w_down):
    o_ba = 4 * H_A * DK_A
    o_qb = o_ba + 2 * H_A
    o_kb = o_qb + H_B * HD_B
    o_qi = o_kb + 2 * HD_B
    o_ki = o_qi + H_IDX * D_IDX
    o_wi = o_ki + D_IDX
    o_end = o_wi + H_IDX
    wie = w_in_even
    pad = jnp.zeros(wie.shape[:2] + (EVEN_COLS - o_end,), wie.dtype)
    w_even = jnp.concatenate([wie[..., :o_ba], wie[..., o_qb:o_kb], wie[..., o_qi:o_ki], wie[..., o_kb:o_qi],
                              wie[..., o_ki:o_wi], wie[..., o_ba:o_qb], wie[..., o_wi:o_end], pad], axis=-1)
    return dict(norm_mix=norm_mix, norm_ffn=norm_ffn, norm_final=norm_final,
                w_in_even=w_even.astype(BF16), dn_conv_w=dn_conv_w, dn_a_log=dn_a_log, dn_dt_bias=dn_dt_bias,
                dn_o_gain=dn_o_gain, dsa_q_gain=dsa_q_gain, dsa_k_gain=dsa_k_gain,
                w_out_even=w_out_even.astype(BF16), w_in_odd=w_in_odd.astype(BF16),
                band_q_gain=band_q_gain, band_k_gain=band_k_gain, band_rel_bias=band_rel_bias,
                w_out_odd=w_out_odd.astype(BF16),
                ffn_w_a=ffn_w_a.astype(BF16), ffn_w_g=ffn_w_g.astype(BF16), ffn_conv_w=ffn_conv_w,
                ffn_conv_b=ffn_conv_b, ffn_w_down=ffn_w_down.astype(BF16))


def kernel(x_prompt, x_sample, state_dn_S, state_dn_conv, cache_dsa_k, cache_dsa_v, cache_dsa_kidx, cache_band_k, cache_band_v, state_ffn_conv, norm_mix, norm_ffn, norm_final, w_in_even, dn_conv_w, dn_a_log, dn_dt_bias, dn_o_gain, dsa_q_gain, dsa_k_gain, w_out_even, w_in_odd, band_q_gain, band_k_gain, band_rel_bias, w_out_odd, ffn_w_a, ffn_w_g, ffn_conv_w, ffn_conv_b, ffn_w_down):
    assert cache_band_k.shape[2] == WINDOW, "band cache must hold exactly one window"
    w = _prep_weights(norm_mix, norm_ffn, norm_final, w_in_even, dn_conv_w, dn_a_log, dn_dt_bias, dn_o_gain,
                      dsa_q_gain, dsa_k_gain, w_out_even, w_in_odd, band_q_gain, band_k_gain, band_rel_bias,
                      w_out_odd, ffn_w_a, ffn_w_g, ffn_conv_w, ffn_conv_b, ffn_w_down)
    past = dict(state_dn_S=state_dn_S, state_dn_conv=state_dn_conv, cache_dsa_k=cache_dsa_k,
                cache_dsa_v=cache_dsa_v, cache_dsa_kidx=cache_dsa_kidx, cache_band_k=cache_band_k,
                cache_band_v=cache_band_v, state_ffn_conv=state_ffn_conv)
    y_p, sp = _trunk(x_prompt, None, w)
    y_s, ss = _trunk(x_sample, past, w)
    return (y_p, y_s,
            sp["dn_S"], ss["dn_S"], sp["dn_conv"], ss["dn_conv"],
            sp["dsa_k"], ss["dsa_k"], sp["dsa_v"], ss["dsa_v"], sp["dsa_kidx"], ss["dsa_kidx"],
            sp["band_k"], ss["band_k"], sp["band_v"], ss["band_v"],
            sp["ffn_conv"], ss["ffn_conv"])
```

```python
import functools
import math

import jax
import jax.numpy as jnp
from jax import lax
from jax.experimental import pallas as pl
from jax.experimental.pallas import tpu as pltpu

F32 = jnp.float32
BF16 = jnp.bfloat16
EPS = 1e-6
INT_MIN = -(2 ** 31)
MASKED = -1e30
LOG2E = math.log2(math.e)

CHUNK = 64
H_A, DK_A, DV_A, CONV_A = 4, 128, 128, 4
C_A = 3 * H_A * DK_A
H_B, HD_B, H_IDX, D_IDX, TOPK_MAX = 8, 64, 4, 64, 256
H_C, HD_C, BAND_CHUNKS, REL_CLIP = 16, 64, 8, 128
WINDOW = BAND_CHUNKS * CHUNK
CONV_F = 3

EVEN_COLS = 3072
COL_Z = 1536
COL_QB = 2048
COL_QI = 2560
COL_KV = 2816
COL_SM = 2944
SM_BETA, SM_A, SM_WI = 64, 68, 72

LANES = 128
VMEM_LIMIT = 48 * 1024 * 1024
V7X_VMEM_USABLE = 56 * 1024 * 1024


def _cparams(sem, vmem=VMEM_LIMIT):
    return pltpu.CompilerParams(dimension_semantics=sem, vmem_limit_bytes=vmem)


def _dot(a, b):
    return jnp.dot(a.astype(BF16), b.astype(BF16), preferred_element_type=F32)


def _dot_nt(a, b):
    return lax.dot_general(a.astype(BF16), b.astype(BF16), (((1,), (1,)), ((), ())),
                           preferred_element_type=F32)


def _dot_f32(a, b):
    return jnp.dot(a, b, preferred_element_type=F32, precision=lax.Precision.HIGHEST)


def _dot_nt_f32(a, b):
    return lax.dot_general(a, b, (((1,), (1,)), ((), ())), preferred_element_type=F32,
                           precision=lax.Precision.HIGHEST)


def _rms(x, gain):
    return x * lax.rsqrt(jnp.mean(x * x, axis=-1, keepdims=True) + EPS) * gain


def _silu(x):
    return x * jax.nn.sigmoid(x)


def _low_half(shape):
    return lax.broadcasted_iota(jnp.int32, shape, len(shape) - 1) < (LANES // 2)


def _pair_rms(x, gain2):
    lo = _low_half(x.shape)
    sq = x * x
    ms_lo = jnp.sum(jnp.where(lo, sq, 0.0), axis=-1, keepdims=True) * (2.0 / LANES)
    ms_hi = jnp.sum(jnp.where(lo, 0.0, sq), axis=-1, keepdims=True) * (2.0 / LANES)
    return x * jnp.where(lo, lax.rsqrt(ms_lo + EPS), lax.rsqrt(ms_hi + EPS)) * gain2


def _split_halves(x):
    lo = _low_half(x.shape)
    return jnp.concatenate([jnp.where(lo, x, 0.0), jnp.where(lo, 0.0, x)], axis=0)


def _join_halves(o, rows):
    return jnp.where(_low_half((rows, LANES)), o[0:rows], o[rows:2 * rows])


def _exp2_weights(s):
    return jnp.exp2(s - jnp.max(s, axis=-1, keepdims=True)).astype(BF16)


def _dup(x):
    return jnp.concatenate([x, x], axis=1)


def _norm_proj_kernel(x_ref, g_ref, w_ref, o_ref):
    o_ref[...] = jnp.dot(_rms(x_ref[...], g_ref[...]).astype(BF16), w_ref[...], preferred_element_type=F32)


def _norm_proj(x2d, gain, w_bf16, tm):
    m, d = x2d.shape
    n = w_bf16.shape[1]
    return pl.pallas_call(
        _norm_proj_kernel,
        out_shape=jax.ShapeDtypeStruct((m, n), F32),
        grid=(m // tm,),
        in_specs=[pl.BlockSpec((tm, d), lambda i: (i, 0)),
                  pl.BlockSpec((1, d), lambda i: (0, 0)),
                  pl.BlockSpec((d, n), lambda i: (0, 0))],
        out_specs=pl.BlockSpec((tm, n), lambda i: (i, 0)),
        compiler_params=_cparams(("parallel",)),
        name="norm_proj",
    )(x2d, gain.reshape(1, d), w_bf16)


def _tri_inv_lockstep(ms, eye_f):
    ts = [eye_f - m for m in ms]
    pbs = [(-m).astype(BF16) for m in ms]
    for _ in range(5):
        pbs = [jnp.dot(pb, pb, preferred_element_type=F32).astype(BF16) for pb in pbs]
        ts = [t + jnp.dot(t.astype(BF16), pb, preferred_element_type=F32) for t, pb in zip(ts, pbs)]
    return ts


def _deltanet_kernel(qkv_ref, z_ref, sm_ref, conv0_ref, s0_ref, cw_ref, alog_ref, dtb_ref, og_ref,
                     o_ref, convout_ref, sout_ref, xbuf, s_scr, *, nc):
    c = pl.program_id(1)
    t = nc * CHUNK

    @pl.when(c == 0)
    def _():
        xbuf[0:8, :] = conv0_ref[0]
        for h in range(H_A):
            s_scr[:, h * DK_A:(h + 1) * DK_A] = jnp.transpose(s0_ref[0, h])

    x = qkv_ref[0]
    xbuf[8:8 + t, :] = x
    w = cw_ref[...]
    y = (x * w[3:4] + xbuf[7:7 + t, :] * w[2:3] + xbuf[6:6 + t, :] * w[1:2] + xbuf[5:5 + t, :] * w[0:1])
    convout_ref[0] = xbuf[t + 5:t + 8, :]
    xbuf[0:8, :] = xbuf[t:t + 8, :]
    y = _silu(y)

    sm = sm_ref[0]
    beta = jax.nn.sigmoid(sm)
    aa = sm + dtb_ref[...]
    softplus = jnp.maximum(aa, 0.0) + jnp.log1p(jnp.exp(-jnp.abs(aa)))
    g = -jnp.exp(alog_ref[...]) * softplus

    ht = H_A * CHUNK
    hk = H_A * DK_A
    ri = lax.broadcasted_iota(jnp.int32, (ht, ht), 0)
    ci = lax.broadcasted_iota(jnp.int32, (ht, ht), 1)
    same = jnp.right_shift(ri, 6) == jnp.right_shift(ci, 6)
    trilbd = jnp.logical_and(same, ri >= ci)
    strictbd = jnp.logical_and(same, ri > ci)
    eye_f = (ri == ci).astype(F32)
    r2 = lax.broadcasted_iota(jnp.int32, (2 * ht, hk), 0)
    c2 = lax.broadcasted_iota(jnp.int32, (2 * ht, hk), 1)
    bd2 = (jnp.right_shift(r2, 6) & (H_A - 1)) == jnp.right_shift(c2, 7)
    bd1 = bd2[0:ht]
    r64 = lax.broadcasted_iota(jnp.int32, (CHUNK, CHUNK), 0)
    c64 = lax.broadcasted_iota(jnp.int32, (CHUNK, CHUNK), 1)
    tril_f = (r64 >= c64).astype(F32)
    og = og_ref[...]

    def heads_on_rows(a, r0, off, width):
        return jnp.concatenate([a[r0:r0 + CHUNK, off + h * width:off + (h + 1) * width] for h in range(H_A)],
                               axis=0)

    pre = []
    for cidx in range(nc):
        r0 = cidx * CHUNK
        gcum = _dot_f32(tril_f, g[r0:r0 + CHUNK])
        q = heads_on_rows(y, r0, 0, DK_A)
        k = heads_on_rows(y, r0, hk, DK_A)
        v = heads_on_rows(y, r0, 2 * hk, DV_A)
        q = q * lax.rsqrt(jnp.sum(q * q, axis=-1, keepdims=True) + EPS) * (DK_A ** -0.5)
        k = k * lax.rsqrt(jnp.sum(k * k, axis=-1, keepdims=True) + EPS)
        bcol = heads_on_rows(beta, r0, SM_BETA, 1)
        gcol = heads_on_rows(gcum, 0, SM_A, 1)
        glast = [gcum[CHUNK - 1:CHUNK, SM_A + h:SM_A + h + 1] for h in range(H_A)]
        gl_col = jnp.concatenate([jnp.broadcast_to(x_, (CHUNK, 1)) for x_ in glast], axis=0)
        gl_row = jnp.concatenate([jnp.broadcast_to(x_, (1, DK_A)) for x_ in glast], axis=1)
        grow = jnp.transpose(jnp.broadcast_to(gcol, (ht, LANES)))[0:1, :]
        decay = jnp.exp(jnp.where(trilbd, gcol - grow, -jnp.inf))
        kb = k * bcol
        a_full = _dot_nt(jnp.concatenate([kb, q], axis=0), k)
        kd = k * jnp.exp(gl_col - gcol)
        pre.append(dict(
            m=jnp.where(strictbd, a_full[0:ht] * decay, 0.0),
            attn=(a_full[ht:2 * ht] * decay).astype(BF16),
            rhs=jnp.concatenate([v * bcol, kb * jnp.exp(gcol)], axis=1).astype(BF16),
            qg=q * jnp.exp(gcol),
            kd_bd=jnp.where(bd1, jnp.concatenate([kd] * H_A, axis=1), 0.0).astype(BF16),
            decay_row=jnp.exp(gl_row)))

    tmats = _tri_inv_lockstep([c["m"] for c in pre], eye_f)
    uws = [jnp.dot(tm.astype(BF16), c["rhs"], preferred_element_type=F32) for tm, c in zip(tmats, pre)]

    for cidx in range(nc):
        r0 = cidx * CHUNK
        c, uw = pre[cidx], uws[cidx]
        u = uw[:, 0:DV_A]
        wq = jnp.concatenate([uw[:, DV_A:DV_A + DK_A], c["qg"]], axis=0)
        wq_bd = jnp.where(bd2, jnp.concatenate([wq] * H_A, axis=1), 0.0)
        st_old = s_scr[...]
        ws_qs = _dot_nt(wq_bd, st_old)
        v_new = u - ws_qs[0:ht]
        s_scr[...] = st_old * c["decay_row"] + _dot(jnp.transpose(v_new), c["kd_bd"])
        o = ws_qs[ht:2 * ht] + _dot(c["attn"], v_new)
        zz = heads_on_rows(z_ref[0], r0, 0, DV_A)
        res = (_rms(o, og) * _silu(zz)).astype(o_ref.dtype)
        for h in range(H_A):
            o_ref[0, r0:r0 + CHUNK, h * DV_A:(h + 1) * DV_A] = res[h * CHUNK:(h + 1) * CHUNK]

    for h in range(H_A):
        sout_ref[0, h] = jnp.transpose(s_scr[:, h * DK_A:(h + 1) * DK_A])


def _deltanet(proj, conv0, s0, conv_w, a_log, dt_bias, o_gain, nc):
    b, l, _ = proj.shape
    t = nc * CHUNK
    pad = jnp.zeros((1, LANES), F32)
    alog128 = lax.dynamic_update_slice(pad, a_log.reshape(1, H_A), (0, SM_A))
    dtb128 = lax.dynamic_update_slice(pad, dt_bias.reshape(1, H_A), (0, SM_A))
    return pl.pallas_call(
        functools.partial(_deltanet_kernel, nc=nc),
        out_shape=(jax.ShapeDtypeStruct((b, l, H_A * DV_A), BF16),
                   jax.ShapeDtypeStruct((b, CONV_A - 1, C_A), F32),
                   jax.ShapeDtypeStruct((b, H_A, DK_A, DV_A), F32)),
        grid=(b, l // t),
        in_specs=[pl.BlockSpec((1, t, C_A), lambda i, c: (i, c, 0)),
                  pl.BlockSpec((1, t, 512), lambda i, c: (i, c, COL_Z // 512)),
                  pl.BlockSpec((1, t, LANES), lambda i, c: (i, c, COL_SM // LANES)),
                  pl.BlockSpec((1, 8, C_A), lambda i, c: (i, 0, 0)),
                  pl.BlockSpec((1, H_A, DK_A, DV_A), lambda i, c: (i, 0, 0, 0)),
                  pl.BlockSpec((CONV_A, C_A), lambda i, c: (0, 0)),
                  pl.BlockSpec((1, LANES), lambda i, c: (0, 0)),
                  pl.BlockSpec((1, LANES), lambda i, c: (0, 0)),
                  pl.BlockSpec((1, DV_A), lambda i, c: (0, 0))],
        out_specs=(pl.BlockSpec((1, t, H_A * DV_A), lambda i, c: (i, c, 0)),
                   pl.BlockSpec((1, CONV_A - 1, C_A), lambda i, c: (i, 0, 0)),
                   pl.BlockSpec((1, H_A, DK_A, DV_A), lambda i, c: (i, 0, 0, 0))),
        scratch_shapes=[pltpu.VMEM((t + 8, C_A), F32), pltpu.VMEM((DV_A, H_A * DK_A), F32)],
        compiler_params=_cparams(("parallel", "arbitrary")),
        name="deltanet",
    )(proj, proj, proj, conv0, s0, conv_w, alog128, dtb128, o_gain.reshape(1, DV_A))


def _dsa_prep_kernel(*refs, l, p, s_pad):
    if p:
        kv_ref, sm_ref, kg_ref, ck_ref, cv_ref, cki_ref, kout_ref, vout_ref, kiout_ref, k2_ref, v2_ref, ki2_ref = refs
    else:
        kv_ref, sm_ref, kg_ref, kout_ref, vout_ref, kiout_ref, k2_ref, v2_ref, ki2_ref = refs
    kv = kv_ref[0]
    kn = _rms(kv[:, 0:HD_B], kg_ref[...])
    vraw = kv[:, HD_B:2 * HD_B]
    kidx = sm_ref[0][:, 0:D_IDX]
    kout_ref[0] = kn
    vout_ref[0] = vraw
    kiout_ref[0] = kidx
    def with_ones(a):
        return jnp.concatenate([a, jnp.ones_like(a)], axis=1)

    if p:
        k2_ref[0, 0:p, :] = _dup(ck_ref[0]).astype(BF16)
        v2_ref[0, 0:p, :] = with_ones(cv_ref[0]).astype(BF16)
        ki2_ref[0, 0:p, :] = _dup(cki_ref[0]).astype(BF16)
    k2_ref[0, p:p + l, :] = _dup(kn).astype(BF16)
    v2_ref[0, p:p + l, :] = with_ones(vraw).astype(BF16)
    ki2_ref[0, p:p + l, :] = _dup(kidx).astype(BF16)
    if s_pad > p + l:
        zpad = jnp.zeros((s_pad - p - l, LANES), BF16)
        k2_ref[0, p + l:s_pad, :] = zpad
        v2_ref[0, p + l:s_pad, :] = zpad
        ki2_ref[0, p + l:s_pad, :] = zpad


def _colsum(x):
    s, n = x.shape
    return jnp.sum(jnp.sum(x.reshape(s // CHUNK, CHUNK, n), axis=0), axis=0, keepdims=True)


def _dsa_kernel(qb_ref, qi_ref, smq_ref, qg_ref, k2_ref, v2_ref, ki2_ref, o_ref, key_scr, eq_scr, selt_scr,
                *, tq, tile0, p, s_keys, k_sel):
    tg = tile0 + pl.program_id(1)

    smq = smq_ref[0]
    sel_r = lax.broadcasted_iota(jnp.int32, (8, LANES), 0)
    sel_c = lax.broadcasted_iota(jnp.int32, (8, LANES), 1)
    pick = (sel_c == sel_r + SM_WI).astype(F32)
    wi_t = _dot_nt_f32(pick, smq) * (H_IDX ** -0.5)
    qi = qi_ref[0]
    ki_all = ki2_ref[0]
    qheads = jnp.concatenate([_split_halves(qi[:, pr * LANES:(pr + 1) * LANES] * (D_IDX ** -0.5))
                              for pr in range(H_IDX // 2)], axis=0)
    score = jnp.zeros((s_keys, tq), F32)
    if tq % LANES == 0:
        d_all = _dot_nt(ki_all, qheads)
        for h in range(H_IDX):
            score = score + jnp.maximum(d_all[:, h * tq:(h + 1) * tq], 0.0) * wi_t[h:h + 1, :]
    else:
        for h in range(H_IDX):
            score = score + jnp.maximum(_dot_nt(ki_all, qheads[h * tq:(h + 1) * tq]), 0.0) * wi_t[h:h + 1, :]

    bits = pltpu.bitcast(score + 0.0, jnp.int32)
    key = bits ^ (jnp.right_shift(bits, 31) & 0x7FFFFFFF)
    kpos = lax.broadcasted_iota(jnp.int32, (s_keys, tq), 0)
    qpos = tg * tq + lax.broadcasted_iota(jnp.int32, (1, tq), 1)
    n_adm = p + (jnp.right_shift(qpos, 6) + 1) * CHUNK
    key_scr[...] = jnp.where(kpos < n_adm, key, INT_MIN)

    kf = float(k_sel)

    def count_ge(cand):
        return _colsum(jnp.where(key_scr[...] >= cand, 1.0, 0.0))

    thr0 = jnp.where(count_ge(jnp.zeros((1, tq), jnp.int32)) >= kf, 0, INT_MIN).astype(jnp.int32)

    def thr_body(i, thr):
        cand = thr | jnp.left_shift(jnp.int32(1), 30 - i)
        return jnp.where(count_ge(cand) >= kf, cand, thr)

    thr = lax.fori_loop(0, 31, thr_body, thr0)

    ge = jnp.logical_and(key_scr[...] >= thr, kpos < n_adm)
    surplus = _colsum(jnp.where(ge, 1.0, 0.0)) - kf
    selt_scr[...] = jnp.where(ge, 0.0, MASKED).astype(BF16)

    @pl.when(jnp.max(surplus) > 0.0)
    def _():
        keys = key_scr[...]
        gt = keys > thr
        eq = jnp.logical_and(keys == thr, kpos < n_adm)
        need = kf - _colsum(jnp.where(gt, 1.0, 0.0))
        eq_scr[...] = jnp.where(eq, 1.0, 0.0)
        nbits = int(s_keys).bit_length()

        def tie_body(i, lim):
            cand = lim | jnp.left_shift(jnp.int32(1), nbits - 1 - i)
            cnt = _colsum(jnp.where(kpos < cand, eq_scr[...], 0.0))
            return jnp.where(cnt <= need, cand, lim)

        lim = lax.fori_loop(0, nbits, tie_body, jnp.zeros((1, tq), jnp.int32))
        keep = jnp.logical_or(gt, jnp.logical_and(eq, kpos < lim))
        selt_scr[...] = jnp.where(keep, 0.0, MASKED).astype(BF16)

    er = lax.broadcasted_iota(jnp.int32, (tq, tq), 0)
    ec = lax.broadcasted_iota(jnp.int32, (tq, tq), 1)
    mask_bias = _dot_nt((er == ec).astype(BF16), selt_scr[...])

    qb = qb_ref[0]
    k_all = k2_ref[0]
    v_all = v2_ref[0]
    lo = _low_half((tq, LANES))
    outs = []
    for pr in range(H_B // 2):
        qn = _pair_rms(qb[:, pr * LANES:(pr + 1) * LANES], qg_ref[...]) * (HD_B ** -0.5 * LOG2E)
        s = _dot_nt(_split_halves(qn), k_all)
        s = (s.reshape(2, tq, s_keys) + mask_bias[None]).reshape(2 * tq, s_keys)
        r = _dot(_exp2_weights(s), v_all)
        o = r / pltpu.roll(r, LANES // 2, 1)
        outs.append(jnp.where(lo, o[0:tq], pltpu.roll(o[tq:2 * tq], LANES // 2, 1)))
    o_ref[0] = jnp.concatenate(outs, axis=1).astype(o_ref.dtype)


def _dsa(proj, q_gain, k_gain, cache):
    b, l, _ = proj.shape
    p = 0 if cache is None else cache[0].shape[1]
    tq = min(LANES, l)
    s_k = p + l
    s_pad = -(-s_k // LANES) * LANES
    k_sel = min(TOPK_MAX, s_k // 4)

    kv_specs = [pl.BlockSpec((1, l, LANES), lambda i: (i, 0, COL_KV // LANES)),
                pl.BlockSpec((1, l, LANES), lambda i: (i, 0, COL_SM // LANES)),
                pl.BlockSpec((1, HD_B), lambda i: (0, 0))]
    args = [proj, proj, k_gain.reshape(1, HD_B)]
    if p:
        kv_specs += [pl.BlockSpec((1, p, HD_B), lambda i: (i, 0, 0))] * 3
        args += list(cache)
    row_out = jax.ShapeDtypeStruct((b, l, HD_B), F32)
    dup_out = jax.ShapeDtypeStruct((b, s_pad, LANES), BF16)
    row_spec = pl.BlockSpec((1, l, HD_B), lambda i: (i, 0, 0))
    dup_spec = pl.BlockSpec((1, s_pad, LANES), lambda i: (i, 0, 0))
    k_b, v_b, ki_b, k2, v2, ki2 = pl.pallas_call(
        functools.partial(_dsa_prep_kernel, l=l, p=p, s_pad=s_pad),
        out_shape=(row_out, row_out, row_out, dup_out, dup_out, dup_out),
        grid=(b,),
        in_specs=kv_specs,
        out_specs=(row_spec, row_spec, row_spec, dup_spec, dup_spec, dup_spec),
        compiler_params=_cparams(("parallel",)),
        name="dsa_prep",
    )(*args)

    n_tiles = l // tq
    tiles_per_class = max(1, 256 // tq) if p == 0 else n_tiles
    qg2 = _dup(q_gain.reshape(1, HD_B))
    outs = []
    for tile0 in range(0, n_tiles, tiles_per_class):
        nt = min(tiles_per_class, n_tiles - tile0)
        s_keys = min(s_pad, -(-(p + (tile0 + nt) * tq) // LANES) * LANES)
        outs.append(pl.pallas_call(
            functools.partial(_dsa_kernel, tq=tq, tile0=tile0, p=p, s_keys=s_keys, k_sel=k_sel),
            out_shape=jax.ShapeDtypeStruct((b, nt * tq, H_B * HD_B), BF16),
            grid=(b, nt),
            in_specs=[pl.BlockSpec((1, tq, 512), lambda i, t, t0=tile0: (i, t0 + t, COL_QB // 512)),
                      pl.BlockSpec((1, tq, 256), lambda i, t, t0=tile0: (i, t0 + t, COL_QI // 256)),
                      pl.BlockSpec((1, tq, LANES), lambda i, t, t0=tile0: (i, t0 + t, COL_SM // LANES)),
                      pl.BlockSpec((1, LANES), lambda i, t: (0, 0)),
                      pl.BlockSpec((1, s_keys, LANES), lambda i, t: (i, 0, 0)),
                      pl.BlockSpec((1, s_keys, LANES), lambda i, t: (i, 0, 0)),
                      pl.BlockSpec((1, s_keys, LANES), lambda i, t: (i, 0, 0))],
            out_specs=pl.BlockSpec((1, tq, H_B * HD_B), lambda i, t: (i, t, 0)),
            scratch_shapes=[pltpu.VMEM((s_keys, tq), jnp.int32), pltpu.VMEM((s_keys, tq), F32),
                            pltpu.VMEM((s_keys, tq), BF16)],
            compiler_params=_cparams(("parallel", "arbitrary")),
            name="dsa",
        )(proj, proj, proj, qg2, k2, v2, ki2))
    o_b = outs[0] if len(outs) == 1 else jnp.concatenate(outs, axis=1)
    return o_b, k_b, v_b, ki_b


def _band_kernel(*refs, tb, cg, has_cache, first_kept):
    if has_cache:
        q_ref, k_ref, v_ref, bias_ref, qg_ref, kg_ref, ck_ref, cv_ref, o_ref, kn_ref, vn_ref, kwin, vwin = refs
    else:
        q_ref, k_ref, v_ref, bias_ref, qg_ref, kg_ref, o_ref, kn_ref, vn_ref, kwin, vwin = refs
    j = pl.program_id(1)
    tbq = tb * CHUNK
    gq = cg * CHUNK
    wk = WINDOW + gq

    @pl.when(j == 0)
    def _():
        if has_cache:
            kwin[0:WINDOW, :] = ck_ref[0].astype(BF16)
            vwin[0:WINDOW, :] = cv_ref[0].astype(BF16)
        else:
            kwin[0:WINDOW, :] = jnp.zeros((WINDOW, H_C * HD_C), BF16)
            vwin[0:WINDOW, :] = jnp.zeros((WINDOW, H_C * HD_C), BF16)

    @pl.when(j > 0)
    def _():
        for i in range(WINDOW // tbq):
            kwin[i * tbq:(i + 1) * tbq, :] = kwin[(i + 1) * tbq:(i + 2) * tbq, :]
            vwin[i * tbq:(i + 1) * tbq, :] = vwin[(i + 1) * tbq:(i + 2) * tbq, :]

    vwin[WINDOW:WINDOW + tbq, :] = v_ref[0].astype(BF16)
    kcol = lax.broadcasted_iota(jnp.int32, (2 * gq, wk), 1)
    n_inv = jnp.maximum(WINDOW - j * tbq, 0)
    for i in range(H_C // 2):
        cols = slice(i * LANES, (i + 1) * LANES)
        kwin[WINDOW:WINDOW + tbq, cols] = _pair_rms(k_ref[0, :, cols], kg_ref[...]).astype(BF16)
        qn = _pair_rms(q_ref[0, :, cols], qg_ref[...]) * (HD_C ** -0.5 * LOG2E)
        for g in range(tb // cg):
            r0 = g * gq
            s = _dot_nt(_split_halves(qn[r0:r0 + gq]), kwin[r0:r0 + wk, cols]) + bias_ref[i]
            if not has_cache:
                s = jnp.where(kcol >= n_inv - r0, s, -jnp.inf)
            m = jnp.max(s, axis=-1, keepdims=True)
            e = jnp.exp2(s - m)
            o = _dot(e, vwin[r0:r0 + wk, cols]) / jnp.sum(e, axis=-1, keepdims=True)
            o_ref[0, r0:r0 + gq, cols] = _join_halves(o, gq).astype(o_ref.dtype)

    @pl.when(j >= first_kept)
    def _():
        rows = pl.ds(pl.multiple_of((j - first_kept) * tbq, tbq), tbq)
        vn_ref[0, rows, :] = v_ref[0]
        for i in range(H_C // 2):
            cols = slice(i * LANES, (i + 1) * LANES)
            kn_ref[0, rows, cols] = _pair_rms(k_ref[0, :, cols], kg_ref[...])


def _band(proj, rel_bias, q_gain, k_gain, cache, tb, cg, keep):
    b, l, _ = proj.shape
    d = H_C * HD_C
    tbq = tb * CHUNK
    gq = cg * CHUNK
    wk = WINDOW + gq
    has_cache = cache is not None
    assert keep % tbq == 0 and l % tbq == 0
    first_kept = (l - keep) // tbq
    in_specs = [pl.BlockSpec((1, tbq, d), lambda i, j: (i, j, 0)),
                pl.BlockSpec((1, tbq, d), lambda i, j: (i, j, 1)),
                pl.BlockSpec((1, tbq, d), lambda i, j: (i, j, 2)),
                pl.BlockSpec((H_C // 2, 2 * gq, wk), lambda i, j: (0, 0, 0)),
                pl.BlockSpec((1, LANES), lambda i, j: (0, 0)),
                pl.BlockSpec((1, LANES), lambda i, j: (0, 0))]
    args = [proj, proj, proj, _band_bias(rel_bias, cg), _dup(q_gain.reshape(1, HD_C)), _dup(k_gain.reshape(1, HD_C))]
    if has_cache:
        in_specs += [pl.BlockSpec((1, WINDOW, d), lambda i, j: (i, 0, 0))] * 2
        args += [cache[0].reshape(b, WINDOW, d), cache[1].reshape(b, WINDOW, d)]
    return pl.pallas_call(
        functools.partial(_band_kernel, tb=tb, cg=cg, has_cache=has_cache, first_kept=first_kept),
        out_shape=(jax.ShapeDtypeStruct((b, l, d), BF16), jax.ShapeDtypeStruct((b, keep, d), F32),
                   jax.ShapeDtypeStruct((b, keep, d), F32)),
        grid=(b, l // tbq),
        in_specs=in_specs,
        out_specs=(pl.BlockSpec((1, tbq, d), lambda i, j: (i, j, 0)),
                   pl.BlockSpec((1, keep, d), lambda i, j: (i, 0, 0)),
                   pl.BlockSpec((1, keep, d), lambda i, j: (i, 0, 0))),
        scratch_shapes=[pltpu.VMEM((WINDOW + tbq, d), BF16), pltpu.VMEM((WINDOW + tbq, d), BF16)],
        compiler_params=_cparams(("parallel", "arbitrary")),
        name="band",
    )(*args)


def _band_bias(rel_bias, cg):
    gq = cg * CHUNK
    wk = WINDOW + gq
    n = wk + gq - 1
    dist = WINDOW + gq - 1 - jnp.arange(n)
    seq = rel_bias[:, jnp.clip(dist, -(CHUNK - 1), REL_CLIP) + (CHUNK - 1)].astype(F32) * LOG2E
    period = jnp.roll(jnp.pad(seq, ((0, 0), (0, 1))), -(gq - 1), axis=1)
    table = jnp.tile(period, (1, gq))[:, :gq * n].reshape(H_C, gq, n)[:, :, :wk]
    qc = jnp.arange(gq) // CHUNK
    kc = jnp.arange(wk) // CHUNK
    inband = jnp.logical_and(kc[None, :] >= qc[:, None], kc[None, :] <= qc[:, None] + BAND_CHUNKS)
    return jnp.where(inband[None], table, -jnp.inf).reshape(H_C // 2, 2 * gq, wk)


def _ffn_kernel(*refs, n_mix, nf, nm, nb, tm, tf, final):
    x_ref = refs[0]
    mix_refs = refs[1:1 + n_mix]
    wo_refs = refs[1 + n_mix:1 + 2 * n_mix]
    rest = refs[1 + 2 * n_mix:]
    if final:
        g_ref, wa_ref, wg_ref, cw_ref, cb_ref, wd_ref, st0_ref, gf_ref = rest[:8]
        rest = rest[8:]
    else:
        g_ref, wa_ref, wg_ref, cw_ref, cb_ref, wd_ref, st0_ref = rest[:7]
        gf_ref = None
        rest = rest[7:]
    out_ref, stout_ref, abuf, cbuf = rest
    mi = pl.program_id(1)
    rows = nb * tm
    bsel = (slice(None),) if nb > 1 else (0,)

    def flat(ref):
        return ref[...].reshape(rows, ref.shape[-1]) if nb > 1 else ref[0]

    def per_stream(z):
        return z.reshape(nb, tm, z.shape[-1]) if nb > 1 else z

    xr = flat(x_ref)
    for m_ref, w_ref in zip(mix_refs, wo_refs):
        xr = xr + jnp.dot(flat(m_ref), w_ref[...], preferred_element_type=F32)
    hb = _rms(xr, g_ref[...]).astype(BF16)

    @pl.when(mi == 0)
    def _():
        cbuf[...] = st0_ref[...]

    acc = jnp.zeros_like(xr)
    for f in range(nf):
        cols = slice(f * tf, (f + 1) * tf)
        slot = f % 2
        a = per_stream(jnp.dot(hb, wa_ref[:, cols], preferred_element_type=F32))
        gate = per_stream(jnp.dot(hb, wg_ref[:, cols], preferred_element_type=F32))
        abuf[(slot, *bsel, slice(0, 8))] = cbuf[(*bsel, slice(None), cols)]
        abuf[(slot, *bsel, slice(8, 8 + tm))] = a
        ac = (a * cw_ref[2:3, cols] + abuf[(slot, *bsel, slice(7, 7 + tm))] * cw_ref[1:2, cols]
              + abuf[(slot, *bsel, slice(6, 6 + tm))] * cw_ref[0:1, cols] + cb_ref[:, cols])
        cbuf[(*bsel, slice(None), cols)] = abuf[(slot, *bsel, slice(tm, tm + 8))]
        u = (_silu(ac) * gate).reshape(rows, tf)
        acc = acc + jnp.dot(u.astype(BF16), wd_ref[cols, :], preferred_element_type=F32)

    yv = xr + acc
    if final:
        yv = _rms(yv, gf_ref[...])
    out_ref[...] = yv.reshape(nb, tm, yv.shape[-1])

    @pl.when(mi == nm - 1)
    def _():
        stout_ref[...] = cbuf[:, 8 - (CONV_F - 1):8, :]


def _ffn_vmem_bytes(rows, d, dff, tf, mix_cols):
    weights = 2 * (3 * d * dff + sum(mix_cols) * d)
    tiles = 2 * (2 * rows * d * 4 + rows * sum(mix_cols) * 2)
    scratch = 2 * (rows + 64) * tf * 4 + 64 * dff * 4
    temps = rows * d * (4 + 4 + 2) + 6 * rows * tf * 4
    return weights + tiles + scratch + temps


def _ffn(x, mixes, wos, gain, wa, wg, conv_w, conv_b, wd, st0, gain_final, nb, tm, tf):
    b, l, d = x.shape
    dff = wa.shape[1]
    nf = dff // tf
    nm = l // tm
    n_mix = len(mixes)
    final = gain_final is not None
    in_specs = [pl.BlockSpec((nb, tm, d), lambda i, m: (i, m, 0))]
    in_specs += [pl.BlockSpec((nb, tm, mx.shape[2]), lambda i, m: (i, m, 0)) for mx in mixes]
    in_specs += [pl.BlockSpec(w.shape, lambda i, m: (0, 0)) for w in wos]
    in_specs += [pl.BlockSpec((1, d), lambda i, m: (0, 0)),
                 pl.BlockSpec((d, dff), lambda i, m: (0, 0)),
                 pl.BlockSpec((d, dff), lambda i, m: (0, 0)),
                 pl.BlockSpec((CONV_F, dff), lambda i, m: (0, 0)),
                 pl.BlockSpec((1, dff), lambda i, m: (0, 0)),
                 pl.BlockSpec((dff, d), lambda i, m: (0, 0)),
                 pl.BlockSpec((nb, 8, dff), lambda i, m: (i, 0, 0))]
    args = [x, *mixes, *wos, gain.reshape(1, d), wa, wg, conv_w, conv_b.reshape(1, dff), wd, st0]
    if final:
        in_specs.append(pl.BlockSpec((1, d), lambda i, m: (0, 0)))
        args.append(gain_final.reshape(1, d))
    return pl.pallas_call(
        functools.partial(_ffn_kernel, n_mix=n_mix, nf=nf, nm=nm, nb=nb, tm=tm, tf=tf, final=final),
        out_shape=(jax.ShapeDtypeStruct((b, l, d), F32),
                   jax.ShapeDtypeStruct((b, CONV_F - 1, dff), F32)),
        grid=(b // nb, nm),
        in_specs=in_specs,
        out_specs=(pl.BlockSpec((nb, tm, d), lambda i, m: (i, m, 0)),
                   pl.BlockSpec((nb, CONV_F - 1, dff), lambda i, m: (i, 0, 0))),
        scratch_shapes=[pltpu.VMEM((2, nb, tm + 8, tf), F32), pltpu.VMEM((nb, 8, dff), F32)],
        compiler_params=_cparams(("parallel", "arbitrary"),
                                 min(V7X_VMEM_USABLE, _ffn_vmem_bytes(nb * tm, d, dff, tf, [mx.shape[2] for mx in mixes]))),
        name="ffn",
    )(*args)


def _pad_rows_front(a, rows):
    return jnp.pad(a, ((0, 0), (rows - a.shape[1], 0), (0, 0)))


def _trunk(x, past, w):
    b, l, d = x.shape
    depth = w["norm_mix"].shape[0]
    first = past is None
    n_chunks = l // CHUNK
    tm_proj = min(512, b * l)
    tm_ffn = min(1024, l)
    nb_ffn = math.gcd(b, max(1, 512 // tm_ffn))
    tf = 256
    nc = min(4, n_chunks)
    tb = min(8, n_chunks)
    cg = min(2, tb)
    dff = w["ffn_w_a"].shape[2]
    out = {k: [] for k in ("dn_S", "dn_conv", "dsa_k", "dsa_v", "dsa_kidx", "band_k", "band_v", "ffn_conv")}
    for layer in range(depth):
        x2d = x.reshape(b * l, d)
        if layer % 2 == 0:
            e = layer // 2
            proj = _norm_proj(x2d, w["norm_mix"][layer], w["w_in_even"][e], tm_proj).reshape(b, l, EVEN_COLS)
            if first:
                conv0 = jnp.zeros((b, 8, C_A), F32)
                s0 = jnp.zeros((b, H_A, DK_A, DV_A), F32)
                cache = None
            else:
                conv0 = _pad_rows_front(past["state_dn_conv"][e], 8)
                s0 = past["state_dn_S"][e]
                cache = (past["cache_dsa_k"][e], past["cache_dsa_v"][e], past["cache_dsa_kidx"][e])
            o_a, buf_a, s_a = _deltanet(proj, conv0, s0, w["dn_conv_w"][e], w["dn_a_log"][e],
                                        w["dn_dt_bias"][e], w["dn_o_gain"][e], nc)
            o_b, k_b, v_b, ki_b = _dsa(proj, w["dsa_q_gain"][e], w["dsa_k_gain"][e], cache)
            mixes = [o_a, o_b]
            wos = [w["w_out_even"][e][:H_A * DV_A], w["w_out_even"][e][H_A * DV_A:]]
            out["dn_S"].append(s_a)
            out["dn_conv"].append(buf_a)
            out["dsa_k"].append(k_b)
            out["dsa_v"].append(v_b)
            out["dsa_kidx"].append(ki_b)
        else:
            jj = layer // 2
            proj = _norm_proj(x2d, w["norm_mix"][layer], w["w_in_odd"][jj], tm_proj).reshape(b, l, 3 * H_C * HD_C)
            cache = None if first else (past["cache_band_k"][jj], past["cache_band_v"][jj])
            keep = min(WINDOW, l) if first else l
            o_c, kn, vn = _band(proj, w["band_rel_bias"][jj], w["band_q_gain"][jj], w["band_k_gain"][jj], cache,
                                tb, cg, keep)
            out["band_k"].append(kn.reshape(b, keep, H_C, HD_C))
            out["band_v"].append(vn.reshape(b, keep, H_C, HD_C))
            mixes = [o_c]
            wos = [w["w_out_odd"][jj]]
        st0 = (jnp.zeros((b, 8, dff), F32) if first else _pad_rows_front(past["state_ffn_conv"][layer], 8))
        gfin = w["norm_final"] if layer == depth - 1 else None
        x, fbuf = _ffn(x, mixes, wos, w["norm_ffn"][layer], w["ffn_w_a"][layer], w["ffn_w_g"][layer],
                       w["ffn_conv_w"][layer], w["ffn_conv_b"][layer], w["ffn_w_down"][layer], st0, gfin,
                       nb_ffn, tm_ffn, tf)
        out["ffn_conv"].append(fbuf)
    return x, {k: jnp.stack(v) for k, v in out.items()}


def _prep_weights(norm_mix, norm_ffn, norm_final, w_in_even, dn_conv_w, dn_a_log, dn_dt_bias, dn_o_gain,
                  dsa_q_gain, dsa_k_gain, w_out_even, w_in_odd, band_q_gain, band_k_gain, band_rel_bias,
                  w_out_odd, ffn_w_a, ffn_w_g, ffn_conv_w, ffn_conv_b, ffn_w_down):
    o_ba = 4 * H_A * DK_A
    o_qb = o_ba + 2 * H_A
    o_kb = o_qb + H_B * HD_B
    o_qi = o_kb + 2 * HD_B
    o_ki = o_qi + H_IDX * D_IDX
    o_wi = o_ki + D_IDX
    o_end = o_wi + H_IDX
    wie = w_in_even
    pad = jnp.zeros(wie.shape[:2] + (EVEN_COLS - o_end,), wie.dtype)
    w_even = jnp.concatenate([wie[..., :o_ba], wie[..., o_qb:o_kb], wie[..., o_qi:o_ki], wie[..., o_kb:o_qi],
                              wie[..., o_ki:o_wi], wie[..., o_ba:o_qb], wie[..., o_wi:o_end], pad], axis=-1)
    return dict(norm_mix=norm_mix, norm_ffn=norm_ffn, norm_final=norm_final,
                w_in_even=w_even.astype(BF16), dn_conv_w=dn_conv_w, dn_a_log=dn_a_log, dn_dt_bias=dn_dt_bias,
                dn_o_gain=dn_o_gain, dsa_q_gain=dsa_q_gain, dsa_k_gain=dsa_k_gain,
                w_out_even=w_out_even.astype(BF16), w_in_odd=w_in_odd.astype(BF16),
                band_q_gain=band_q_gain, band_k_gain=band_k_gain, band_rel_bias=band_rel_bias,
                w_out_odd=w_out_odd.astype(BF16),
                ffn_w_a=ffn_w_a.astype(BF16), ffn_w_g=ffn_w_g.astype(BF16), ffn_conv_w=ffn_conv_w,
                ffn_conv_b=ffn_conv_b, ffn_w_down=ffn_w_down.astype(BF16))


def kernel(x_prompt, x_sample, state_dn_S, state_dn_conv, cache_dsa_k, cache_dsa_v, cache_dsa_kidx, cache_band_k, cache_band_v, state_ffn_conv, norm_mix, norm_ffn, norm_final, w_in_even, dn_conv_w, dn_a_log, dn_dt_bias, dn_o_gain, dsa_q_gain, dsa_k_gain, w_out_even, w_in_odd, band_q_gain, band_k_gain, band_rel_bias, w_out_odd, ffn_w_a, ffn_w_g, ffn_conv_w, ffn_conv_b, ffn_w_down):
    assert cache_band_k.shape[2] == WINDOW, "band cache must hold exactly one window"
    w = _prep_weights(norm_mix, norm_ffn, norm_final, w_in_even, dn_conv_w, dn_a_log, dn_dt_bias, dn_o_gain,
                      dsa_q_gain, dsa_k_gain, w_out_even, w_in_odd, band_q_gain, band_k_gain, band_rel_bias,
                      w_out_odd, ffn_w_a, ffn_w_g, ffn_conv_w, ffn_conv_b, ffn_w_down)
    past = dict(state_dn_S=state_dn_S, state_dn_conv=state_dn_conv, cache_dsa_k=cache_dsa_k,
                cache_dsa_v=cache_dsa_v, cache_dsa_kidx=cache_dsa_kidx, cache_band_k=cache_band_k,
                cache_band_v=cache_band_v, state_ffn_conv=state_ffn_conv)
    y_p, sp = _trunk(x_prompt, None, w)
    y_s, ss = _trunk(x_sample, past, w)
    return (y_p, y_s,
            sp["dn_S"], ss["dn_S"], sp["dn_conv"], ss["dn_conv"],
            sp["dsa_k"], ss["dsa_k"], sp["dsa_v"], ss["dsa_v"], sp["dsa_kidx"], ss["dsa_kidx"],
            sp["band_k"], ss["band_k"], sp["band_v"], ss["band_v"],
            sp["ffn_conv"], ss["ffn_conv"])
```

```python
import functools
import math

import jax
import jax.numpy as jnp
from jax import lax
from jax.experimental import pallas as pl
from jax.experimental.pallas import tpu as pltpu

F32 = jnp.float32
BF16 = jnp.bfloat16
EPS = 1e-6
INT_MIN = -(2 ** 31)
MASKED = -1e30
LOG2E = math.log2(math.e)

CHUNK = 64
H_A, DK_A, DV_A, CONV_A = 4, 128, 128, 4
C_A = 3 * H_A * DK_A
H_B, HD_B, H_IDX, D_IDX, TOPK_MAX = 8, 64, 4, 64, 256
H_C, HD_C, BAND_CHUNKS, REL_CLIP = 16, 64, 8, 128
WINDOW = BAND_CHUNKS * CHUNK
CONV_F = 3

EVEN_COLS = 3072
COL_Z = 1536
COL_QB = 2048
COL_QI = 2560
COL_KV = 2816
COL_SM = 2944
SM_BETA, SM_A, SM_WI = 64, 68, 72

LANES = 128
VMEM_LIMIT = 48 * 1024 * 1024
V7X_VMEM_USABLE = 56 * 1024 * 1024


def _cparams(sem, vmem=VMEM_LIMIT):
    return pltpu.CompilerParams(dimension_semantics=sem, vmem_limit_bytes=vmem)


def _dot(a, b):
    return jnp.dot(a.astype(BF16), b.astype(BF16), preferred_element_type=F32)


def _dot_nt(a, b):
    return lax.dot_general(a.astype(BF16), b.astype(BF16), (((1,), (1,)), ((), ())),
                           preferred_element_type=F32)


def _dot_f32(a, b):
    return jnp.dot(a, b, preferred_element_type=F32, precision=lax.Precision.HIGHEST)


def _dot_nt_f32(a, b):
    return lax.dot_general(a, b, (((1,), (1,)), ((), ())), preferred_element_type=F32,
                           precision=lax.Precision.HIGHEST)


def _rms(x, gain):
    return x * lax.rsqrt(jnp.mean(x * x, axis=-1, keepdims=True) + EPS) * gain


def _silu(x):
    return x * jax.nn.sigmoid(x)


def _low_half(shape):
    return lax.broadcasted_iota(jnp.int32, shape, len(shape) - 1) < (LANES // 2)


def _pair_rms(x, gain2):
    lo = _low_half(x.shape)
    sq = x * x
    ms_lo = jnp.sum(jnp.where(lo, sq, 0.0), axis=-1, keepdims=True) * (2.0 / LANES)
    ms_hi = jnp.sum(jnp.where(lo, 0.0, sq), axis=-1, keepdims=True) * (2.0 / LANES)
    return x * jnp.where(lo, lax.rsqrt(ms_lo + EPS), lax.rsqrt(ms_hi + EPS)) * gain2


def _split_halves(x):
    lo = _low_half(x.shape)
    return jnp.concatenate([jnp.where(lo, x, 0.0), jnp.where(lo, 0.0, x)], axis=0)


def _join_halves(o, rows):
    return jnp.where(_low_half((rows, LANES)), o[0:rows], o[rows:2 * rows])


def _exp2_weights(s):
    return jnp.exp2(s - jnp.max(s, axis=-1, keepdims=True)).astype(BF16)


def _dup(x):
    return jnp.concatenate([x, x], axis=1)


def _norm_proj_kernel(x_ref, g_ref, w_ref, o_ref):
    o_ref[...] = jnp.dot(_rms(x_ref[...], g_ref[...]).astype(BF16), w_ref[...], preferred_element_type=F32)


def _norm_proj(x2d, gain, w_bf16, tm):
    m, d = x2d.shape
    n = w_bf16.shape[1]
    return pl.pallas_call(
        _norm_proj_kernel,
        out_shape=jax.ShapeDtypeStruct((m, n), F32),
        grid=(m // tm,),
        in_specs=[pl.BlockSpec((tm, d), lambda i: (i, 0)),
                  pl.BlockSpec((1, d), lambda i: (0, 0)),
                  pl.BlockSpec((d, n), lambda i: (0, 0))],
        out_specs=pl.BlockSpec((tm, n), lambda i: (i, 0)),
        compiler_params=_cparams(("parallel",)),
        name="norm_proj",
    )(x2d, gain.reshape(1, d), w_bf16)


def _tri_inv_lockstep(ms, eye_f):
    ts = [eye_f - m for m in ms]
    pbs = [(-m).astype(BF16) for m in ms]
    for _ in range(5):
        pbs = [jnp.dot(pb, pb, preferred_element_type=F32).astype(BF16) for pb in pbs]
        ts = [t + jnp.dot(t.astype(BF16), pb, preferred_element_type=F32) for t, pb in zip(ts, pbs)]
    return ts


def _deltanet_kernel(qkv_ref, z_ref, sm_ref, conv0_ref, s0_ref, cw_ref, alog_ref, dtb_ref, og_ref,
                     o_ref, convout_ref, sout_ref, xbuf, s_scr, *, nc):
    c = pl.program_id(1)
    t = nc * CHUNK

    @pl.when(c == 0)
    def _():
        xbuf[0:8, :] = conv0_ref[0]
        for h in range(H_A):
            s_scr[:, h * DK_A:(h + 1) * DK_A] = jnp.transpose(s0_ref[0, h])

    x = qkv_ref[0]
    xbuf[8:8 + t, :] = x
    w = cw_ref[...]
    y = (x * w[3:4] + xbuf[7:7 + t, :] * w[2:3] + xbuf[6:6 + t, :] * w[1:2] + xbuf[5:5 + t, :] * w[0:1])
    convout_ref[0] = xbuf[t + 5:t + 8, :]
    xbuf[0:8, :] = xbuf[t:t + 8, :]
    y = _silu(y)

    sm = sm_ref[0]
    beta = jax.nn.sigmoid(sm)
    aa = sm + dtb_ref[...]
    softplus = jnp.maximum(aa, 0.0) + jnp.log1p(jnp.exp(-jnp.abs(aa)))
    g = -jnp.exp(alog_ref[...]) * softplus

    ht = H_A * CHUNK
    hk = H_A * DK_A
    ri = lax.broadcasted_iota(jnp.int32, (ht, ht), 0)
    ci = lax.broadcasted_iota(jnp.int32, (ht, ht), 1)
    same = jnp.right_shift(ri, 6) == jnp.right_shift(ci, 6)
    trilbd = jnp.logical_and(same, ri >= ci)
    strictbd = jnp.logical_and(same, ri > ci)
    eye_f = (ri == ci).astype(F32)
    r2 = lax.broadcasted_iota(jnp.int32, (2 * ht, hk), 0)
    c2 = lax.broadcasted_iota(jnp.int32, (2 * ht, hk), 1)
    bd2 = (jnp.right_shift(r2, 6) & (H_A - 1)) == jnp.right_shift(c2, 7)
    bd1 = bd2[0:ht]
    r64 = lax.broadcasted_iota(jnp.int32, (CHUNK, CHUNK), 0)
    c64 = lax.broadcasted_iota(jnp.int32, (CHUNK, CHUNK), 1)
    tril_f = (r64 >= c64).astype(F32)
    og = og_ref[...]

    def heads_on_rows(a, r0, off, width):
        return jnp.concatenate([a[r0:r0 + CHUNK, off + h * width:off + (h + 1) * width] for h in range(H_A)],
                               axis=0)

    pre = []
    for cidx in range(nc):
        r0 = cidx * CHUNK
        gcum = _dot_f32(tril_f, g[r0:r0 + CHUNK])
        q = heads_on_rows(y, r0, 0, DK_A)
        k = heads_on_rows(y, r0, hk, DK_A)
        v = heads_on_rows(y, r0, 2 * hk, DV_A)
        q = q * lax.rsqrt(jnp.sum(q * q, axis=-1, keepdims=True) + EPS) * (DK_A ** -0.5)
        k = k * lax.rsqrt(jnp.sum(k * k, axis=-1, keepdims=True) + EPS)
        bcol = heads_on_rows(beta, r0, SM_BETA, 1)
        gcol = heads_on_rows(gcum, 0, SM_A, 1)
        glast = [gcum[CHUNK - 1:CHUNK, SM_A + h:SM_A + h + 1] for h in range(H_A)]
        gl_col = jnp.concatenate([jnp.broadcast_to(x_, (CHUNK, 1)) for x_ in glast], axis=0)
        gl_row = jnp.concatenate([jnp.broadcast_to(x_, (1, DK_A)) for x_ in glast], axis=1)
        grow = jnp.transpose(jnp.broadcast_to(gcol, (ht, LANES)))[0:1, :]
        decay = jnp.exp(jnp.where(trilbd, gcol - grow, -jnp.inf))
        kb = k * bcol
        a_full = _dot_nt(jnp.concatenate([kb, q], axis=0), k)
        kd = k * jnp.exp(gl_col - gcol)
        pre.append(dict(
            m=jnp.where(strictbd, a_full[0:ht] * decay, 0.0),
            attn=(a_full[ht:2 * ht] * decay).astype(BF16),
            rhs=jnp.concatenate([v * bcol, kb * jnp.exp(gcol)], axis=1).astype(BF16),
            qg=q * jnp.exp(gcol),
            kd_bd=jnp.where(bd1, jnp.concatenate([kd] * H_A, axis=1), 0.0).astype(BF16),
            decay_row=jnp.exp(gl_row)))

    tmats = _tri_inv_lockstep([c["m"] for c in pre], eye_f)
    uws = [jnp.dot(tm.astype(BF16), c["rhs"], preferred_element_type=F32) for tm, c in zip(tmats, pre)]

    for cidx in range(nc):
        r0 = cidx * CHUNK
        c, uw = pre[cidx], uws[cidx]
        u = uw[:, 0:DV_A]
        wq = jnp.concatenate([uw[:, DV_A:DV_A + DK_A], c["qg"]], axis=0)
        wq_bd = jnp.where(bd2, jnp.concatenate([wq] * H_A, axis=1), 0.0)
        st_old = s_scr[...]
        ws_qs = _dot_nt(wq_bd, st_old)
        v_new = u - ws_qs[0:ht]
        s_scr[...] = st_old * c["decay_row"] + _dot(jnp.transpose(v_new), c["kd_bd"])
        o = ws_qs[ht:2 * ht] + _dot(c["attn"], v_new)
        zz = heads_on_rows(z_ref[0], r0, 0, DV_A)
        res = (_rms(o, og) * _silu(zz)).astype(o_ref.dtype)
        for h in range(H_A):
            o_ref[0, r0:r0 + CHUNK, h * DV_A:(h + 1) * DV_A] = res[h * CHUNK:(h + 1) * CHUNK]

    for h in range(H_A):
        sout_ref[0, h] = jnp.transpose(s_scr[:, h * DK_A:(h + 1) * DK_A])


def _deltanet(proj, conv0, s0, conv_w, a_log, dt_bias, o_gain, nc):
    b, l, _ = proj.shape
    t = nc * CHUNK
    pad = jnp.zeros((1, LANES), F32)
    alog128 = lax.dynamic_update_slice(pad, a_log.reshape(1, H_A), (0, SM_A))
    dtb128 = lax.dynamic_update_slice(pad, dt_bias.reshape(1, H_A), (0, SM_A))
    return pl.pallas_call(
        functools.partial(_deltanet_kernel, nc=nc),
        out_shape=(jax.ShapeDtypeStruct((b, l, H_A * DV_A), BF16),
                   jax.ShapeDtypeStruct((b, CONV_A - 1, C_A), F32),
                   jax.ShapeDtypeStruct((b, H_A, DK_A, DV_A), F32)),
        grid=(b, l // t),
        in_specs=[pl.BlockSpec((1, t, C_A), lambda i, c: (i, c, 0)),
                  pl.BlockSpec((1, t, 512), lambda i, c: (i, c, COL_Z // 512)),
                  pl.BlockSpec((1, t, LANES), lambda i, c: (i, c, COL_SM // LANES)),
                  pl.BlockSpec((1, 8, C_A), lambda i, c: (i, 0, 0)),
                  pl.BlockSpec((1, H_A, DK_A, DV_A), lambda i, c: (i, 0, 0, 0)),
                  pl.BlockSpec((CONV_A, C_A), lambda i, c: (0, 0)),
                  pl.BlockSpec((1, LANES), lambda i, c: (0, 0)),
                  pl.BlockSpec((1, LANES), lambda i, c: (0, 0)),
                  pl.BlockSpec((1, DV_A), lambda i, c: (0, 0))],
        out_specs=(pl.BlockSpec((1, t, H_A * DV_A), lambda i, c: (i, c, 0)),
                   pl.BlockSpec((1, CONV_A - 1, C_A), lambda i, c: (i, 0, 0)),
                   pl.BlockSpec((1, H_A, DK_A, DV_A), lambda i, c: (i, 0, 0, 0))),
        scratch_shapes=[pltpu.VMEM((t + 8, C_A), F32), pltpu.VMEM((DV_A, H_A * DK_A), F32)],
        compiler_params=_cparams(("parallel", "arbitrary")),
        name="deltanet",
    )(proj, proj, proj, conv0, s0, conv_w, alog128, dtb128, o_gain.reshape(1, DV_A))


def _dsa_prep_kernel(*refs, l, p, s_pad, n_alias):
    n_in = 6 if p else 3
    kv_ref, sm_ref, kg_ref = refs[:3]
    if p:
        ck_ref, cv_ref, cki_ref = refs[3:6]
    kout_ref, vout_ref, kiout_ref, k2_ref, v2_ref, ki2_ref = refs[n_in + n_alias:]
    kv = kv_ref[0]
    kn = _rms(kv[:, 0:HD_B], kg_ref[...])
    vraw = kv[:, HD_B:2 * HD_B]
    kidx = sm_ref[0][:, 0:D_IDX]
    kout_ref[0, 0] = kn
    vout_ref[0, 0] = vraw
    kiout_ref[0, 0] = kidx

    def with_ones(a):
        return jnp.concatenate([a, jnp.ones_like(a)], axis=1)

    if p:
        k2_ref[0, 0:p, :] = _dup(ck_ref[0]).astype(BF16)
        v2_ref[0, 0:p, :] = with_ones(cv_ref[0]).astype(BF16)
        ki2_ref[0, 0:p, :] = _dup(cki_ref[0]).astype(BF16)
    k2_ref[0, p:p + l, :] = _dup(kn).astype(BF16)
    v2_ref[0, p:p + l, :] = with_ones(vraw).astype(BF16)
    ki2_ref[0, p:p + l, :] = _dup(kidx).astype(BF16)
    if s_pad > p + l:
        zpad = jnp.zeros((s_pad - p - l, LANES), BF16)
        k2_ref[0, p + l:s_pad, :] = zpad
        v2_ref[0, p + l:s_pad, :] = zpad
        ki2_ref[0, p + l:s_pad, :] = zpad


def _colsum(x):
    s, n = x.shape
    return jnp.sum(jnp.sum(x.reshape(s // CHUNK, CHUNK, n), axis=0), axis=0, keepdims=True)


def _dsa_kernel(*refs, tq, tile0, p, s_keys, k_sel, aliased):
    qb_ref, qi_ref, smq_ref, qg_ref, k2_ref, v2_ref, ki2_ref = refs[:7]
    o_ref, key_scr, eq_scr, selt_scr = refs[7 + int(aliased):]
    tg = tile0 + pl.program_id(1)

    smq = smq_ref[0]
    sel_r = lax.broadcasted_iota(jnp.int32, (8, LANES), 0)
    sel_c = lax.broadcasted_iota(jnp.int32, (8, LANES), 1)
    pick = (sel_c == sel_r + SM_WI).astype(F32)
    wi_t = _dot_nt_f32(pick, smq) * (H_IDX ** -0.5)
    qi = qi_ref[0]
    ki_all = ki2_ref[0]
    qheads = jnp.concatenate([_split_halves(qi[:, pr * LANES:(pr + 1) * LANES] * (D_IDX ** -0.5))
                              for pr in range(H_IDX // 2)], axis=0)
    score = jnp.zeros((s_keys, tq), F32)
    if tq % LANES == 0:
        d_all = _dot_nt(ki_all, qheads)
        for h in range(H_IDX):
            score = score + jnp.maximum(d_all[:, h * tq:(h + 1) * tq], 0.0) * wi_t[h:h + 1, :]
    else:
        for h in range(H_IDX):
            score = score + jnp.maximum(_dot_nt(ki_all, qheads[h * tq:(h + 1) * tq]), 0.0) * wi_t[h:h + 1, :]

    bits = pltpu.bitcast(score + 0.0, jnp.int32)
    key = bits ^ (jnp.right_shift(bits, 31) & 0x7FFFFFFF)
    kpos = lax.broadcasted_iota(jnp.int32, (s_keys, tq), 0)
    qpos = tg * tq + lax.broadcasted_iota(jnp.int32, (1, tq), 1)
    n_adm = p + (jnp.right_shift(qpos, 6) + 1) * CHUNK
    key_scr[...] = jnp.where(kpos < n_adm, key, INT_MIN)

    kf = float(k_sel)

    def count_ge(cand):
        return _colsum(jnp.where(key_scr[...] >= cand, 1.0, 0.0))

    thr0 = jnp.where(count_ge(jnp.zeros((1, tq), jnp.int32)) >= kf, 0, INT_MIN).astype(jnp.int32)

    def thr_body(i, thr):
        cand = thr | jnp.left_shift(jnp.int32(1), 30 - i)
        return jnp.where(count_ge(cand) >= kf, cand, thr)

    thr = lax.fori_loop(0, 31, thr_body, thr0)

    ge = jnp.logical_and(key_scr[...] >= thr, kpos < n_adm)
    surplus = _colsum(jnp.where(ge, 1.0, 0.0)) - kf
    selt_scr[...] = jnp.where(ge, 0.0, MASKED).astype(BF16)

    @pl.when(jnp.max(surplus) > 0.0)
    def _():
        keys = key_scr[...]
        gt = keys > thr
        eq = jnp.logical_and(keys == thr, kpos < n_adm)
        need = kf - _colsum(jnp.where(gt, 1.0, 0.0))
        eq_scr[...] = jnp.where(eq, 1.0, 0.0)
        nbits = int(s_keys).bit_length()

        def tie_body(i, lim):
            cand = lim | jnp.left_shift(jnp.int32(1), nbits - 1 - i)
            cnt = _colsum(jnp.where(kpos < cand, eq_scr[...], 0.0))
            return jnp.where(cnt <= need, cand, lim)

        lim = lax.fori_loop(0, nbits, tie_body, jnp.zeros((1, tq), jnp.int32))
        keep = jnp.logical_or(gt, jnp.logical_and(eq, kpos < lim))
        selt_scr[...] = jnp.where(keep, 0.0, MASKED).astype(BF16)

    er = lax.broadcasted_iota(jnp.int32, (tq, tq), 0)
    ec = lax.broadcasted_iota(jnp.int32, (tq, tq), 1)
    mask_bias = _dot_nt((er == ec).astype(BF16), selt_scr[...])

    qb = qb_ref[0]
    k_all = k2_ref[0]
    v_all = v2_ref[0]
    lo = _low_half((tq, LANES))
    outs = []
    for pr in range(H_B // 2):
        qn = _pair_rms(qb[:, pr * LANES:(pr + 1) * LANES], qg_ref[...]) * (HD_B ** -0.5 * LOG2E)
        s = _dot_nt(_split_halves(qn), k_all)
        s = (s.reshape(2, tq, s_keys) + mask_bias[None]).reshape(2 * tq, s_keys)
        r = _dot(_exp2_weights(s), v_all)
        o = r / pltpu.roll(r, LANES // 2, 1)
        outs.append(jnp.where(lo, o[0:tq], pltpu.roll(o[tq:2 * tq], LANES // 2, 1)))
    o_ref[0] = jnp.concatenate(outs, axis=1).astype(o_ref.dtype)


def _dsa(proj, q_gain, k_gain, cache, slot, n_slots, prev):
    b, l, _ = proj.shape
    p = 0 if cache is None else cache[0].shape[1]
    tq = min(LANES, l)
    s_k = p + l
    s_pad = -(-s_k // LANES) * LANES
    k_sel = min(TOPK_MAX, s_k // 4)

    kv_specs = [pl.BlockSpec((1, l, LANES), lambda i: (i, 0, COL_KV // LANES)),
                pl.BlockSpec((1, l, LANES), lambda i: (i, 0, COL_SM // LANES)),
                pl.BlockSpec((1, HD_B), lambda i: (0, 0))]
    args = [proj, proj, k_gain.reshape(1, HD_B)]
    if p:
        kv_specs += [pl.BlockSpec((1, p, HD_B), lambda i: (i, 0, 0))] * 3
        args += list(cache)
    aliases = {}
    if prev is not None:
        aliases = {len(args) + n: n for n in range(3)}
        kv_specs += [pl.BlockSpec(memory_space=pl.ANY)] * 3
        args += list(prev)
    row_out = jax.ShapeDtypeStruct((n_slots, b, l, HD_B), F32)
    dup_out = jax.ShapeDtypeStruct((b, s_pad, LANES), BF16)
    row_spec = pl.BlockSpec((1, 1, l, HD_B), lambda i: (slot, i, 0, 0))
    dup_spec = pl.BlockSpec((1, s_pad, LANES), lambda i: (i, 0, 0))
    k_b, v_b, ki_b, k2, v2, ki2 = pl.pallas_call(
        functools.partial(_dsa_prep_kernel, l=l, p=p, s_pad=s_pad, n_alias=len(aliases)),
        out_shape=(row_out, row_out, row_out, dup_out, dup_out, dup_out),
        grid=(b,),
        in_specs=kv_specs,
        out_specs=(row_spec, row_spec, row_spec, dup_spec, dup_spec, dup_spec),
        input_output_aliases=aliases,
        compiler_params=_cparams(("parallel",)),
        name="dsa_prep",
    )(*args)

    n_tiles = l // tq
    tiles_per_class = 1 if p == 0 else n_tiles
    qg2 = _dup(q_gain.reshape(1, HD_B))
    o_b = None
    for tile0 in range(0, n_tiles, tiles_per_class):
        nt = min(tiles_per_class, n_tiles - tile0)
        s_keys = min(s_pad, -(-(p + (tile0 + nt) * tq) // LANES) * LANES)
        aliased = o_b is not None
        o_b = pl.pallas_call(
            functools.partial(_dsa_kernel, tq=tq, tile0=tile0, p=p, s_keys=s_keys, k_sel=k_sel, aliased=aliased),
            out_shape=jax.ShapeDtypeStruct((b, l, H_B * HD_B), BF16),
            grid=(b, nt),
            in_specs=[pl.BlockSpec((1, tq, 512), lambda i, t, t0=tile0: (i, t0 + t, COL_QB // 512)),
                      pl.BlockSpec((1, tq, 256), lambda i, t, t0=tile0: (i, t0 + t, COL_QI // 256)),
                      pl.BlockSpec((1, tq, LANES), lambda i, t, t0=tile0: (i, t0 + t, COL_SM // LANES)),
                      pl.BlockSpec((1, LANES), lambda i, t: (0, 0)),
                      pl.BlockSpec((1, s_keys, LANES), lambda i, t: (i, 0, 0)),
                      pl.BlockSpec((1, s_keys, LANES), lambda i, t: (i, 0, 0)),
                      pl.BlockSpec((1, s_keys, LANES), lambda i, t: (i, 0, 0))]
                     + ([pl.BlockSpec(memory_space=pl.ANY)] if aliased else []),
            out_specs=pl.BlockSpec((1, tq, H_B * HD_B), lambda i, t, t0=tile0: (i, t0 + t, 0)),
            scratch_shapes=[pltpu.VMEM((s_keys, tq), jnp.int32), pltpu.VMEM((s_keys, tq), F32),
                            pltpu.VMEM((s_keys, tq), BF16)],
            input_output_aliases={7: 0} if aliased else {},
            compiler_params=_cparams(("parallel", "arbitrary")),
            name="dsa",
        )(proj, proj, proj, qg2, k2, v2, ki2, *([o_b] if aliased else []))
    return o_b, (k_b, v_b, ki_b)


def _band_kernel(*refs, tb, cg, has_cache, first_kept, n_alias):
    n_in = 8 if has_cache else 6
    q_ref, k_ref, v_ref, bias_ref, qg_ref, kg_ref = refs[:6]
    if has_cache:
        ck_ref, cv_ref = refs[6:8]
    o_ref, kn_ref, vn_ref, kwin, vwin = refs[n_in + n_alias:]
    j = pl.program_id(1)
    tbq = tb * CHUNK
    gq = cg * CHUNK
    wk = WINDOW + gq

    @pl.when(j == 0)
    def _():
        if has_cache:
            kwin[0:WINDOW, :] = ck_ref[0].astype(BF16)
            vwin[0:WINDOW, :] = cv_ref[0].astype(BF16)
        else:
            kwin[0:WINDOW, :] = jnp.zeros((WINDOW, H_C * HD_C), BF16)
            vwin[0:WINDOW, :] = jnp.zeros((WINDOW, H_C * HD_C), BF16)

    @pl.when(j > 0)
    def _():
        for i in range(WINDOW // tbq):
            kwin[i * tbq:(i + 1) * tbq, :] = kwin[(i + 1) * tbq:(i + 2) * tbq, :]
            vwin[i * tbq:(i + 1) * tbq, :] = vwin[(i + 1) * tbq:(i + 2) * tbq, :]

    vwin[WINDOW:WINDOW + tbq, :] = v_ref[0].astype(BF16)
    kcol = lax.broadcasted_iota(jnp.int32, (2 * gq, wk), 1)
    n_inv = jnp.maximum(WINDOW - j * tbq, 0)
    for i in range(H_C // 2):
        cols = slice(i * LANES, (i + 1) * LANES)
        kwin[WINDOW:WINDOW + tbq, cols] = _pair_rms(k_ref[0, :, cols], kg_ref[...]).astype(BF16)
        qn = _pair_rms(q_ref[0, :, cols], qg_ref[...]) * (HD_C ** -0.5 * LOG2E)
        for g in range(tb // cg):
            r0 = g * gq
            s = _dot_nt(_split_halves(qn[r0:r0 + gq]), kwin[r0:r0 + wk, cols]) + bias_ref[i]
            if not has_cache:
                s = jnp.where(kcol >= n_inv - r0, s, -jnp.inf)
            m = jnp.max(s, axis=-1, keepdims=True)
            e = jnp.exp2(s - m)
            o = _dot(e, vwin[r0:r0 + wk, cols]) / jnp.sum(e, axis=-1, keepdims=True)
            o_ref[0, r0:r0 + gq, cols] = _join_halves(o, gq).astype(o_ref.dtype)

    @pl.when(j >= first_kept)
    def _():
        rows = pl.ds(pl.multiple_of((j - first_kept) * tbq, tbq), tbq)
        vn_ref[0, 0, rows, :] = v_ref[0]
        for i in range(H_C // 2):
            cols = slice(i * LANES, (i + 1) * LANES)
            kn_ref[0, 0, rows, cols] = _pair_rms(k_ref[0, :, cols], kg_ref[...])


def _band(proj, rel_bias, q_gain, k_gain, cache, tb, cg, keep, slot, n_slots, prev):
    b, l, _ = proj.shape
    d = H_C * HD_C
    tbq = tb * CHUNK
    gq = cg * CHUNK
    wk = WINDOW + gq
    has_cache = cache is not None
    assert keep % tbq == 0 and l % tbq == 0
    first_kept = (l - keep) // tbq
    in_specs = [pl.BlockSpec((1, tbq, d), lambda i, j: (i, j, 0)),
                pl.BlockSpec((1, tbq, d), lambda i, j: (i, j, 1)),
                pl.BlockSpec((1, tbq, d), lambda i, j: (i, j, 2)),
                pl.BlockSpec((H_C // 2, 2 * gq, wk), lambda i, j: (0, 0, 0)),
                pl.BlockSpec((1, LANES), lambda i, j: (0, 0)),
                pl.BlockSpec((1, LANES), lambda i, j: (0, 0))]
    args = [proj, proj, proj, _band_bias(rel_bias, cg), _dup(q_gain.reshape(1, HD_C)), _dup(k_gain.reshape(1, HD_C))]
    if has_cache:
        in_specs += [pl.BlockSpec((1, WINDOW, d), lambda i, j: (i, 0, 0))] * 2
        args += [cache[0].reshape(b, WINDOW, d), cache[1].reshape(b, WINDOW, d)]
    aliases = {}
    if prev is not None:
        aliases = {len(args): 1, len(args) + 1: 2}
        in_specs += [pl.BlockSpec(memory_space=pl.ANY)] * 2
        args += list(prev)
    cache_out = jax.ShapeDtypeStruct((n_slots, b, keep, d), F32)
    cache_spec = pl.BlockSpec((1, 1, keep, d), lambda i, j: (slot, i, 0, 0))
    o_c, kn, vn = pl.pallas_call(
        functools.partial(_band_kernel, tb=tb, cg=cg, has_cache=has_cache, first_kept=first_kept,
                          n_alias=len(aliases)),
        out_shape=(jax.ShapeDtypeStruct((b, l, d), BF16), cache_out, cache_out),
        grid=(b, l // tbq),
        in_specs=in_specs,
        out_specs=(pl.BlockSpec((1, tbq, d), lambda i, j: (i, j, 0)), cache_spec, cache_spec),
        scratch_shapes=[pltpu.VMEM((WINDOW + tbq, d), BF16), pltpu.VMEM((WINDOW + tbq, d), BF16)],
        input_output_aliases=aliases,
        compiler_params=_cparams(("parallel", "arbitrary")),
        name="band",
    )(*args)
    return o_c, (kn, vn)


def _band_bias(rel_bias, cg):
    gq = cg * CHUNK
    wk = WINDOW + gq
    n = wk + gq - 1
    dist = WINDOW + gq - 1 - jnp.arange(n)
    seq = rel_bias[:, jnp.clip(dist, -(CHUNK - 1), REL_CLIP) + (CHUNK - 1)].astype(F32) * LOG2E
    period = jnp.roll(jnp.pad(seq, ((0, 0), (0, 1))), -(gq - 1), axis=1)
    table = jnp.tile(period, (1, gq))[:, :gq * n].reshape(H_C, gq, n)[:, :, :wk]
    qc = jnp.arange(gq) // CHUNK
    kc = jnp.arange(wk) // CHUNK
    inband = jnp.logical_and(kc[None, :] >= qc[:, None], kc[None, :] <= qc[:, None] + BAND_CHUNKS)
    return jnp.where(inband[None], table, -jnp.inf).reshape(H_C // 2, 2 * gq, wk)


def _ffn_kernel(*refs, n_mix, nf, nm, nb, tm, tf, final):
    x_ref = refs[0]
    mix_refs = refs[1:1 + n_mix]
    wo_refs = refs[1 + n_mix:1 + 2 * n_mix]
    rest = refs[1 + 2 * n_mix:]
    if final:
        g_ref, wa_ref, wg_ref, cw_ref, cb_ref, wd_ref, st0_ref, gf_ref = rest[:8]
        rest = rest[8:]
    else:
        g_ref, wa_ref, wg_ref, cw_ref, cb_ref, wd_ref, st0_ref = rest[:7]
        gf_ref = None
        rest = rest[7:]
    out_ref, stout_ref, abuf, cbuf = rest
    mi = pl.program_id(1)
    rows = nb * tm
    bsel = (slice(None),) if nb > 1 else (0,)

    def flat(ref):
        return ref[...].reshape(rows, ref.shape[-1]) if nb > 1 else ref[0]

    def per_stream(z):
        return z.reshape(nb, tm, z.shape[-1]) if nb > 1 else z

    xr = flat(x_ref)
    for m_ref, w_ref in zip(mix_refs, wo_refs):
        xr = xr + jnp.dot(flat(m_ref), w_ref[...], preferred_element_type=F32)
    hb = _rms(xr, g_ref[...]).astype(BF16)

    @pl.when(mi == 0)
    def _():
        cbuf[...] = st0_ref[...]

    acc = jnp.zeros_like(xr)
    for f in range(nf):
        cols = slice(f * tf, (f + 1) * tf)
        slot = f % 2
        a = per_stream(jnp.dot(hb, wa_ref[:, cols], preferred_element_type=F32))
        gate = per_stream(jnp.dot(hb, wg_ref[:, cols], preferred_element_type=F32))
        abuf[(slot, *bsel, slice(0, 8))] = cbuf[(*bsel, slice(None), cols)]
        abuf[(slot, *bsel, slice(8, 8 + tm))] = a
        ac = (a * cw_ref[2:3, cols] + abuf[(slot, *bsel, slice(7, 7 + tm))] * cw_ref[1:2, cols]
              + abuf[(slot, *bsel, slice(6, 6 + tm))] * cw_ref[0:1, cols] + cb_ref[:, cols])
        cbuf[(*bsel, slice(None), cols)] = abuf[(slot, *bsel, slice(tm, tm + 8))]
        u = (_silu(ac) * gate).reshape(rows, tf)
        acc = acc + jnp.dot(u.astype(BF16), wd_ref[cols, :], preferred_element_type=F32)

    yv = xr + acc
    if final:
        yv = _rms(yv, gf_ref[...])
    out_ref[...] = yv.reshape(nb, tm, yv.shape[-1])

    @pl.when(mi == nm - 1)
    def _():
        stout_ref[...] = cbuf[:, 8 - (CONV_F - 1):8, :]


def _ffn_vmem_bytes(rows, d, dff, tf, mix_cols):
    weights = 2 * (3 * d * dff + sum(mix_cols) * d)
    tiles = 2 * (2 * rows * d * 4 + rows * sum(mix_cols) * 2)
    scratch = 2 * (rows + 64) * tf * 4 + 64 * dff * 4
    temps = rows * d * (4 + 4 + 2) + 6 * rows * tf * 4
    return weights + tiles + scratch + temps


def _ffn(x, mixes, wos, gain, wa, wg, conv_w, conv_b, wd, st0, gain_final, nb, tm, tf):
    b, l, d = x.shape
    dff = wa.shape[1]
    nf = dff // tf
    nm = l // tm
    n_mix = len(mixes)
    final = gain_final is not None
    in_specs = [pl.BlockSpec((nb, tm, d), lambda i, m: (i, m, 0))]
    in_specs += [pl.BlockSpec((nb, tm, mx.shape[2]), lambda i, m: (i, m, 0)) for mx in mixes]
    in_specs += [pl.BlockSpec(w.shape, lambda i, m: (0, 0)) for w in wos]
    in_specs += [pl.BlockSpec((1, d), lambda i, m: (0, 0)),
                 pl.BlockSpec((d, dff), lambda i, m: (0, 0)),
                 pl.BlockSpec((d, dff), lambda i, m: (0, 0)),
                 pl.BlockSpec((CONV_F, dff), lambda i, m: (0, 0)),
                 pl.BlockSpec((1, dff), lambda i, m: (0, 0)),
                 pl.BlockSpec((dff, d), lambda i, m: (0, 0)),
                 pl.BlockSpec((nb, 8, dff), lambda i, m: (i, 0, 0))]
    args = [x, *mixes, *wos, gain.reshape(1, d), wa, wg, conv_w, conv_b.reshape(1, dff), wd, st0]
    if final:
        in_specs.append(pl.BlockSpec((1, d), lambda i, m: (0, 0)))
        args.append(gain_final.reshape(1, d))
    return pl.pallas_call(
        functools.partial(_ffn_kernel, n_mix=n_mix, nf=nf, nm=nm, nb=nb, tm=tm, tf=tf, final=final),
        out_shape=(jax.ShapeDtypeStruct((b, l, d), F32),
                   jax.ShapeDtypeStruct((b, CONV_F - 1, dff), F32)),
        grid=(b // nb, nm),
        in_specs=in_specs,
        out_specs=(pl.BlockSpec((nb, tm, d), lambda i, m: (i, m, 0)),
                   pl.BlockSpec((nb, CONV_F - 1, dff), lambda i, m: (i, 0, 0))),
        scratch_shapes=[pltpu.VMEM((2, nb, tm + 8, tf), F32), pltpu.VMEM((nb, 8, dff), F32)],
        compiler_params=_cparams(("parallel", "arbitrary"),
                                 min(V7X_VMEM_USABLE, _ffn_vmem_bytes(nb * tm, d, dff, tf, [mx.shape[2] for mx in mixes]))),
        name="ffn",
    )(*args)


def _pad_rows_front(a, rows):
    return jnp.pad(a, ((0, 0), (rows - a.shape[1], 0), (0, 0)))


def _trunk(x, past, w):
    b, l, d = x.shape
    depth = w["norm_mix"].shape[0]
    first = past is None
    n_chunks = l // CHUNK
    tm_proj = min(512, b * l)
    tm_ffn = min(512, l)
    nb_ffn = math.gcd(b, max(1, 512 // tm_ffn))
    tf = 256
    nc = min(4, n_chunks)
    tb = min(8, n_chunks)
    cg = min(2, tb)
    dff = w["ffn_w_a"].shape[2]
    n_even, n_odd = (depth + 1) // 2, depth // 2
    keep = min(WINDOW, l) if first else l
    dsa_new = band_new = None
    out = {k: [] for k in ("dn_S", "dn_conv", "ffn_conv")}
    for layer in range(depth):
        x2d = x.reshape(b * l, d)
        if layer % 2 == 0:
            e = layer // 2
            proj = _norm_proj(x2d, w["norm_mix"][layer], w["w_in_even"][e], tm_proj).reshape(b, l, EVEN_COLS)
            if first:
                conv0 = jnp.zeros((b, 8, C_A), F32)
                s0 = jnp.zeros((b, H_A, DK_A, DV_A), F32)
                cache = None
            else:
                conv0 = _pad_rows_front(past["state_dn_conv"][e], 8)
                s0 = past["state_dn_S"][e]
                cache = (past["cache_dsa_k"][e], past["cache_dsa_v"][e], past["cache_dsa_kidx"][e])
            o_a, buf_a, s_a = _deltanet(proj, conv0, s0, w["dn_conv_w"][e], w["dn_a_log"][e],
                                        w["dn_dt_bias"][e], w["dn_o_gain"][e], nc)
            o_b, dsa_new = _dsa(proj, w["dsa_q_gain"][e], w["dsa_k_gain"][e], cache, e, n_even, dsa_new)
            mixes = [o_a, o_b]
            wos = [w["w_out_even"][e][:H_A * DV_A], w["w_out_even"][e][H_A * DV_A:]]
            out["dn_S"].append(s_a)
            out["dn_conv"].append(buf_a)
        else:
            jj = layer // 2
            proj = _norm_proj(x2d, w["norm_mix"][layer], w["w_in_odd"][jj], tm_proj).reshape(b, l, 3 * H_C * HD_C)
            cache = None if first else (past["cache_band_k"][jj], past["cache_band_v"][jj])
            o_c, band_new = _band(proj, w["band_rel_bias"][jj], w["band_q_gain"][jj], w["band_k_gain"][jj], cache,
                                  tb, cg, keep, jj, n_odd, band_new)
            mixes = [o_c]
            wos = [w["w_out_odd"][jj]]
        st0 = (jnp.zeros((b, 8, dff), F32) if first else _pad_rows_front(past["state_ffn_conv"][layer], 8))
        gfin = w["norm_final"] if layer == depth - 1 else None
        x, fbuf = _ffn(x, mixes, wos, w["norm_ffn"][layer], w["ffn_w_a"][layer], w["ffn_w_g"][layer],
                       w["ffn_conv_w"][layer], w["ffn_conv_b"][layer], w["ffn_w_down"][layer], st0, gfin,
                       nb_ffn, tm_ffn, tf)
        out["ffn_conv"].append(fbuf)
    states = {k: jnp.stack(v) for k, v in out.items()}
    states["dsa_k"], states["dsa_v"], states["dsa_kidx"] = dsa_new
    states["band_k"], states["band_v"] = (a.reshape(n_odd, b, keep, H_C, HD_C) for a in band_new)
    return x, states


def _prep_weights(norm_mix, norm_ffn, norm_final, w_in_even, dn_conv_w, dn_a_log, dn_dt_bias, dn_o_gain,
                  dsa_q_gain, dsa_k_gain, w_out_even, w_in_odd, band_q_gain, band_k_gain, band_rel_bias,
                  w_out_odd, ffn_w_a, ffn_w_g, ffn_conv_w, ffn_conv_b, ffn_w_down):
    o_ba = 4 * H_A * DK_A
    o_qb = o_ba + 2 * H_A
    o_kb = o_qb + H_B * HD_B
    o_qi = o_kb + 2 * HD_B
    o_ki = o_qi + H_IDX * D_IDX
    o_wi = o_ki + D_IDX
    o_end = o_wi + H_IDX
    wie = w_in_even
    pad = jnp.zeros(wie.shape[:2] + (EVEN_COLS - o_end,), wie.dtype)
    w_even = jnp.concatenate([wie[..., :o_ba], wie[..., o_qb:o_kb], wie[..., o_qi:o_ki], wie[..., o_kb:o_qi],
                              wie[..., o_ki:o_wi], wie[..., o_ba:o_qb], wie[..., o_wi:o_end], pad], axis=-1)
    return dict(norm_mix=norm_mix, norm_ffn=norm_ffn, norm_final=norm_final,
                w_in_even=w_even.astype(BF16), dn_conv_w=dn_conv_w, dn_a_log=dn_a_log, dn_dt_bias=dn_dt_bias,
                dn_o_gain=dn_o_gain, dsa_q_gain=dsa_q_gain, dsa_k_gain=dsa_k_gain,
                w_out_even=w_out_even.astype(BF16), w_in_odd=w_in_odd.astype(BF16),
                band_q_gain=band_q_gain, band_k_gain=band_k_gain, band_rel_bias=band_rel_bias,
                w_out_odd=w_out_odd.astype(BF16),
                ffn_w_a=ffn_w_a.astype(BF16), ffn_w_g=ffn_w_g.astype(BF16), ffn_conv_w=ffn_conv_w,
                ffn_conv_b=ffn_conv_b, ffn_w_down=ffn_w_down.astype(BF16))


def kernel(x_prompt, x_sample, state_dn_S, state_dn_conv, cache_dsa_k, cache_dsa_v, cache_dsa_kidx, cache_band_k, cache_band_v, state_ffn_conv, norm_mix, norm_ffn, norm_final, w_in_even, dn_conv_w, dn_a_log, dn_dt_bias, dn_o_gain, dsa_q_gain, dsa_k_gain, w_out_even, w_in_odd, band_q_gain, band_k_gain, band_rel_bias, w_out_odd, ffn_w_a, ffn_w_g, ffn_conv_w, ffn_conv_b, ffn_w_down):
    assert cache_band_k.shape[2] == WINDOW, "band cache must hold exactly one window"
    w = _prep_weights(norm_mix, norm_ffn, norm_final, w_in_even, dn_conv_w, dn_a_log, dn_dt_bias, dn_o_gain,
                      dsa_q_gain, dsa_k_gain, w_out_even, w_in_odd, band_q_gain, band_k_gain, band_rel_bias,
                      w_out_odd, ffn_w_a, ffn_w_g, ffn_conv_w, ffn_conv_b, ffn_w_down)
    past = dict(state_dn_S=state_dn_S, state_dn_conv=state_dn_conv, cache_dsa_k=cache_dsa_k,
                cache_dsa_v=cache_dsa_v, cache_dsa_kidx=cache_dsa_kidx, cache_band_k=cache_band_k,
                cache_band_v=cache_band_v, state_ffn_conv=state_ffn_conv)
    y_p, sp = _trunk(x_prompt, None, w)
    y_s, ss = _trunk(x_sample, past, w)
    return (y_p, y_s,
            sp["dn_S"], ss["dn_S"], sp["dn_conv"], ss["dn_conv"],
            sp["dsa_k"], ss["dsa_k"], sp["dsa_v"], ss["dsa_v"], sp["dsa_kidx"], ss["dsa_kidx"],
            sp["band_k"], ss["band_k"], sp["band_v"], ss["band_v"],
            sp["ffn_conv"], ss["ffn_conv"])
```

```python
import functools
import math

import jax
import jax.numpy as jnp
from jax import lax
from jax.experimental import pallas as pl
from jax.experimental.pallas import tpu as pltpu

F32 = jnp.float32
BF16 = jnp.bfloat16
EPS = 1e-6
INT_MIN = -(2 ** 31)
MASKED = -1e30
LOG2E = math.log2(math.e)

CHUNK = 64
H_A, DK_A, DV_A, CONV_A = 4, 128, 128, 4
C_A = 3 * H_A * DK_A
H_B, HD_B, H_IDX, D_IDX, TOPK_MAX = 8, 64, 4, 64, 256
H_C, HD_C, BAND_CHUNKS, REL_CLIP = 16, 64, 8, 128
WINDOW = BAND_CHUNKS * CHUNK
CONV_F = 3

EVEN_COLS = 3072
COL_Z = 1536
COL_QB = 2048
COL_QI = 2560
COL_KV = 2816
COL_SM = 2944
SM_BETA, SM_A, SM_WI = 64, 68, 72

LANES = 128
VMEM_LIMIT = 48 * 1024 * 1024
V7X_VMEM_USABLE = 56 * 1024 * 1024


def _cparams(sem, vmem=VMEM_LIMIT):
    return pltpu.CompilerParams(dimension_semantics=sem, vmem_limit_bytes=vmem)


def _dot(a, b):
    return jnp.dot(a.astype(BF16), b.astype(BF16), preferred_element_type=F32)


def _dot_nt(a, b):
    return lax.dot_general(a.astype(BF16), b.astype(BF16), (((1,), (1,)), ((), ())),
                           preferred_element_type=F32)


def _dot_f32(a, b):
    return jnp.dot(a, b, preferred_element_type=F32, precision=lax.Precision.HIGHEST)


def _dot_nt_f32(a, b):
    return lax.dot_general(a, b, (((1,), (1,)), ((), ())), preferred_element_type=F32,
                           precision=lax.Precision.HIGHEST)


def _rms(x, gain):
    return x * lax.rsqrt(jnp.mean(x * x, axis=-1, keepdims=True) + EPS) * gain


def _silu(x):
    return x * jax.nn.sigmoid(x)


def _low_half(shape):
    return lax.broadcasted_iota(jnp.int32, shape, len(shape) - 1) < (LANES // 2)


def _pair_rms(x, gain2):
    lo = _low_half(x.shape)
    sq = x * x
    ms_lo = jnp.sum(jnp.where(lo, sq, 0.0), axis=-1, keepdims=True) * (2.0 / LANES)
    ms_hi = jnp.sum(jnp.where(lo, 0.0, sq), axis=-1, keepdims=True) * (2.0 / LANES)
    return x * jnp.where(lo, lax.rsqrt(ms_lo + EPS), lax.rsqrt(ms_hi + EPS)) * gain2


def _split_halves(x):
    lo = _low_half(x.shape)
    return jnp.concatenate([jnp.where(lo, x, 0.0), jnp.where(lo, 0.0, x)], axis=0)


def _join_halves(o, rows):
    return jnp.where(_low_half((rows, LANES)), o[0:rows], o[rows:2 * rows])


def _exp2_weights(s):
    return jnp.exp2(s - jnp.max(s, axis=-1, keepdims=True)).astype(BF16)


def _dup(x):
    return jnp.concatenate([x, x], axis=1)


def _norm_proj_kernel(x_ref, g_ref, w_ref, o_ref):
    o_ref[...] = jnp.dot(_rms(x_ref[...], g_ref[...]).astype(BF16), w_ref[...], preferred_element_type=F32)


def _norm_proj(x2d, gain, w_bf16, tm):
    m, d = x2d.shape
    n = w_bf16.shape[1]
    return pl.pallas_call(
        _norm_proj_kernel,
        out_shape=jax.ShapeDtypeStruct((m, n), F32),
        grid=(m // tm,),
        in_specs=[pl.BlockSpec((tm, d), lambda i: (i, 0)),
                  pl.BlockSpec((1, d), lambda i: (0, 0)),
                  pl.BlockSpec((d, n), lambda i: (0, 0))],
        out_specs=pl.BlockSpec((tm, n), lambda i: (i, 0)),
        compiler_params=_cparams(("parallel",)),
        name="norm_proj",
    )(x2d, gain.reshape(1, d), w_bf16)


def _tri_inv_lockstep(ms, eye_f):
    ts = [eye_f - m for m in ms]
    pbs = [(-m).astype(BF16) for m in ms]
    for _ in range(5):
        pbs = [jnp.dot(pb, pb, preferred_element_type=F32).astype(BF16) for pb in pbs]
        ts = [t + jnp.dot(t.astype(BF16), pb, preferred_element_type=F32) for t, pb in zip(ts, pbs)]
    return ts


def _deltanet_kernel(qkv_ref, z_ref, sm_ref, conv0_ref, s0_ref, cw_ref, alog_ref, dtb_ref, og_ref,
                     o_ref, convout_ref, sout_ref, xbuf, s_scr, *, nc):
    c = pl.program_id(1)
    t = nc * CHUNK

    @pl.when(c == 0)
    def _():
        xbuf[0:8, :] = conv0_ref[0]
        for h in range(H_A):
            s_scr[:, h * DK_A:(h + 1) * DK_A] = jnp.transpose(s0_ref[0, h])

    x = qkv_ref[0]
    xbuf[8:8 + t, :] = x
    w = cw_ref[...]
    y = (x * w[3:4] + xbuf[7:7 + t, :] * w[2:3] + xbuf[6:6 + t, :] * w[1:2] + xbuf[5:5 + t, :] * w[0:1])
    convout_ref[0] = xbuf[t + 5:t + 8, :]
    xbuf[0:8, :] = xbuf[t:t + 8, :]
    y = _silu(y)

    sm = sm_ref[0]
    beta = jax.nn.sigmoid(sm)
    aa = sm + dtb_ref[...]
    softplus = jnp.maximum(aa, 0.0) + jnp.log1p(jnp.exp(-jnp.abs(aa)))
    g = -jnp.exp(alog_ref[...]) * softplus

    ht = H_A * CHUNK
    hk = H_A * DK_A
    ri = lax.broadcasted_iota(jnp.int32, (ht, ht), 0)
    ci = lax.broadcasted_iota(jnp.int32, (ht, ht), 1)
    same = jnp.right_shift(ri, 6) == jnp.right_shift(ci, 6)
    trilbd = jnp.logical_and(same, ri >= ci)
    strictbd = jnp.logical_and(same, ri > ci)
    eye_f = (ri == ci).astype(F32)
    r2 = lax.broadcasted_iota(jnp.int32, (2 * ht, hk), 0)
    c2 = lax.broadcasted_iota(jnp.int32, (2 * ht, hk), 1)
    bd2 = (jnp.right_shift(r2, 6) & (H_A - 1)) == jnp.right_shift(c2, 7)
    bd1 = bd2[0:ht]
    r64 = lax.broadcasted_iota(jnp.int32, (CHUNK, CHUNK), 0)
    c64 = lax.broadcasted_iota(jnp.int32, (CHUNK, CHUNK), 1)
    tril_f = (r64 >= c64).astype(F32)
    og = og_ref[...]

    def heads_on_rows(a, r0, off, width):
        return jnp.concatenate([a[r0:r0 + CHUNK, off + h * width:off + (h + 1) * width] for h in range(H_A)],
                               axis=0)

    pre = []
    for cidx in range(nc):
        r0 = cidx * CHUNK
        gcum = _dot_f32(tril_f, g[r0:r0 + CHUNK])
        q = heads_on_rows(y, r0, 0, DK_A)
        k = heads_on_rows(y, r0, hk, DK_A)
        v = heads_on_rows(y, r0, 2 * hk, DV_A)
        q = q * lax.rsqrt(jnp.sum(q * q, axis=-1, keepdims=True) + EPS) * (DK_A ** -0.5)
        k = k * lax.rsqrt(jnp.sum(k * k, axis=-1, keepdims=True) + EPS)
        bcol = heads_on_rows(beta, r0, SM_BETA, 1)
        gcol = heads_on_rows(gcum, 0, SM_A, 1)
        glast = [gcum[CHUNK - 1:CHUNK, SM_A + h:SM_A + h + 1] for h in range(H_A)]
        gl_col = jnp.concatenate([jnp.broadcast_to(x_, (CHUNK, 1)) for x_ in glast], axis=0)
        gl_row = jnp.concatenate([jnp.broadcast_to(x_, (1, DK_A)) for x_ in glast], axis=1)
        grow = jnp.transpose(jnp.broadcast_to(gcol, (ht, LANES)))[0:1, :]
        decay = jnp.exp(jnp.where(trilbd, gcol - grow, -jnp.inf))
        kb = k * bcol
        a_full = _dot_nt(jnp.concatenate([kb, q], axis=0), k)
        kd = k * jnp.exp(gl_col - gcol)
        pre.append(dict(
            m=jnp.where(strictbd, a_full[0:ht] * decay, 0.0),
            attn=(a_full[ht:2 * ht] * decay).astype(BF16),
            rhs=jnp.concatenate([v * bcol, kb * jnp.exp(gcol)], axis=1).astype(BF16),
            qg=q * jnp.exp(gcol),
            kd_bd=jnp.where(bd1, jnp.concatenate([kd] * H_A, axis=1), 0.0).astype(BF16),
            decay_row=jnp.exp(gl_row)))

    tmats = _tri_inv_lockstep([c["m"] for c in pre], eye_f)
    uws = [jnp.dot(tm.astype(BF16), c["rhs"], preferred_element_type=F32) for tm, c in zip(tmats, pre)]

    for cidx in range(nc):
        r0 = cidx * CHUNK
        c, uw = pre[cidx], uws[cidx]
        u = uw[:, 0:DV_A]
        wq = jnp.concatenate([uw[:, DV_A:DV_A + DK_A], c["qg"]], axis=0)
        wq_bd = jnp.where(bd2, jnp.concatenate([wq] * H_A, axis=1), 0.0)
        st_old = s_scr[...]
        ws_qs = _dot_nt(wq_bd, st_old)
        v_new = u - ws_qs[0:ht]
        s_scr[...] = st_old * c["decay_row"] + _dot(jnp.transpose(v_new), c["kd_bd"])
        o = ws_qs[ht:2 * ht] + _dot(c["attn"], v_new)
        zz = heads_on_rows(z_ref[0], r0, 0, DV_A)
        res = (_rms(o, og) * _silu(zz)).astype(o_ref.dtype)
        for h in range(H_A):
            o_ref[0, r0:r0 + CHUNK, h * DV_A:(h + 1) * DV_A] = res[h * CHUNK:(h + 1) * CHUNK]

    for h in range(H_A):
        sout_ref[0, h] = jnp.transpose(s_scr[:, h * DK_A:(h + 1) * DK_A])


def _deltanet(proj, conv0, s0, conv_w, a_log, dt_bias, o_gain, nc):
    b, l, _ = proj.shape
    t = nc * CHUNK
    pad = jnp.zeros((1, LANES), F32)
    alog128 = lax.dynamic_update_slice(pad, a_log.reshape(1, H_A), (0, SM_A))
    dtb128 = lax.dynamic_update_slice(pad, dt_bias.reshape(1, H_A), (0, SM_A))
    return pl.pallas_call(
        functools.partial(_deltanet_kernel, nc=nc),
        out_shape=(jax.ShapeDtypeStruct((b, l, H_A * DV_A), BF16),
                   jax.ShapeDtypeStruct((b, CONV_A - 1, C_A), F32),
                   jax.ShapeDtypeStruct((b, H_A, DK_A, DV_A), F32)),
        grid=(b, l // t),
        in_specs=[pl.BlockSpec((1, t, C_A), lambda i, c: (i, c, 0)),
                  pl.BlockSpec((1, t, 512), lambda i, c: (i, c, COL_Z // 512)),
                  pl.BlockSpec((1, t, LANES), lambda i, c: (i, c, COL_SM // LANES)),
                  pl.BlockSpec((1, 8, C_A), lambda i, c: (i, 0, 0)),
                  pl.BlockSpec((1, H_A, DK_A, DV_A), lambda i, c: (i, 0, 0, 0)),
                  pl.BlockSpec((CONV_A, C_A), lambda i, c: (0, 0)),
                  pl.BlockSpec((1, LANES), lambda i, c: (0, 0)),
                  pl.BlockSpec((1, LANES), lambda i, c: (0, 0)),
                  pl.BlockSpec((1, DV_A), lambda i, c: (0, 0))],
        out_specs=(pl.BlockSpec((1, t, H_A * DV_A), lambda i, c: (i, c, 0)),
                   pl.BlockSpec((1, CONV_A - 1, C_A), lambda i, c: (i, 0, 0)),
                   pl.BlockSpec((1, H_A, DK_A, DV_A), lambda i, c: (i, 0, 0, 0))),
        scratch_shapes=[pltpu.VMEM((t + 8, C_A), F32), pltpu.VMEM((DV_A, H_A * DK_A), F32)],
        compiler_params=_cparams(("parallel", "arbitrary")),
        name="deltanet",
    )(proj, proj, proj, conv0, s0, conv_w, alog128, dtb128, o_gain.reshape(1, DV_A))


def _dsa_prep_kernel(*refs, l, p, s_pad, n_alias):
    n_in = 6 if p else 3
    kv_ref, sm_ref, kg_ref = refs[:3]
    if p:
        ck_ref, cv_ref, cki_ref = refs[3:6]
    kout_ref, vout_ref, kiout_ref, k2_ref, v2_ref, ki2_ref = refs[n_in + n_alias:]
    kv = kv_ref[0]
    kn = _rms(kv[:, 0:HD_B], kg_ref[...])
    vraw = kv[:, HD_B:2 * HD_B]
    kidx = sm_ref[0][:, 0:D_IDX]
    kout_ref[0, 0] = kn
    vout_ref[0, 0] = vraw
    kiout_ref[0, 0] = kidx

    def with_ones(a):
        return jnp.concatenate([a, jnp.ones_like(a)], axis=1)

    if p:
        k2_ref[0, 0:p, :] = _dup(ck_ref[0]).astype(BF16)
        v2_ref[0, 0:p, :] = with_ones(cv_ref[0]).astype(BF16)
        ki2_ref[0, 0:p, :] = _dup(cki_ref[0]).astype(BF16)
    k2_ref[0, p:p + l, :] = _dup(kn).astype(BF16)
    v2_ref[0, p:p + l, :] = with_ones(vraw).astype(BF16)
    ki2_ref[0, p:p + l, :] = _dup(kidx).astype(BF16)
    if s_pad > p + l:
        zpad = jnp.zeros((s_pad - p - l, LANES), BF16)
        k2_ref[0, p + l:s_pad, :] = zpad
        v2_ref[0, p + l:s_pad, :] = zpad
        ki2_ref[0, p + l:s_pad, :] = zpad


def _colsum(x):
    s, n = x.shape
    return jnp.sum(jnp.sum(x.reshape(s // CHUNK, CHUNK, n), axis=0), axis=0, keepdims=True)


def _dsa_kernel(*refs, tq, tile0, p, s_keys, k_sel, aliased):
    qb_ref, qi_ref, smq_ref, qg_ref, k2_ref, v2_ref, ki2_ref = refs[:7]
    o_ref, key_scr, eq_scr, selt_scr = refs[7 + int(aliased):]
    tg = tile0 + pl.program_id(1)

    smq = smq_ref[0]
    sel_r = lax.broadcasted_iota(jnp.int32, (8, LANES), 0)
    sel_c = lax.broadcasted_iota(jnp.int32, (8, LANES), 1)
    pick = (sel_c == sel_r + SM_WI).astype(F32)
    wi_t = _dot_nt_f32(pick, smq) * (H_IDX ** -0.5)
    qi = qi_ref[0]
    ki_all = ki2_ref[0]
    qheads = jnp.concatenate([_split_halves(qi[:, pr * LANES:(pr + 1) * LANES] * (D_IDX ** -0.5))
                              for pr in range(H_IDX // 2)], axis=0)
    score = jnp.zeros((s_keys, tq), F32)
    if tq % LANES == 0:
        d_all = _dot_nt(ki_all, qheads)
        for h in range(H_IDX):
            score = score + jnp.maximum(d_all[:, h * tq:(h + 1) * tq], 0.0) * wi_t[h:h + 1, :]
    else:
        for h in range(H_IDX):
            score = score + jnp.maximum(_dot_nt(ki_all, qheads[h * tq:(h + 1) * tq]), 0.0) * wi_t[h:h + 1, :]

    bits = pltpu.bitcast(score + 0.0, jnp.int32)
    key = bits ^ (jnp.right_shift(bits, 31) & 0x7FFFFFFF)
    kpos = lax.broadcasted_iota(jnp.int32, (s_keys, tq), 0)
    qpos = tg * tq + lax.broadcasted_iota(jnp.int32, (1, tq), 1)
    n_adm = p + (jnp.right_shift(qpos, 6) + 1) * CHUNK
    key_scr[...] = jnp.where(kpos < n_adm, key, INT_MIN)

    kf = float(k_sel)

    def count_ge(cand):
        return _colsum(jnp.where(key_scr[...] >= cand, 1.0, 0.0))

    thr0 = jnp.where(count_ge(jnp.zeros((1, tq), jnp.int32)) >= kf, 0, INT_MIN).astype(jnp.int32)

    def thr_body(i, thr):
        cand = thr | jnp.left_shift(jnp.int32(1), 30 - i)
        return jnp.where(count_ge(cand) >= kf, cand, thr)

    thr = lax.fori_loop(0, 31, thr_body, thr0)

    ge = jnp.logical_and(key_scr[...] >= thr, kpos < n_adm)
    surplus = _colsum(jnp.where(ge, 1.0, 0.0)) - kf
    selt_scr[...] = jnp.where(ge, 0.0, MASKED).astype(BF16)

    @pl.when(jnp.max(surplus) > 0.0)
    def _():
        keys = key_scr[...]
        gt = keys > thr
        eq = jnp.logical_and(keys == thr, kpos < n_adm)
        need = kf - _colsum(jnp.where(gt, 1.0, 0.0))
        eq_scr[...] = jnp.where(eq, 1.0, 0.0)
        nbits = int(s_keys).bit_length()

        def tie_body(i, lim):
            cand = lim | jnp.left_shift(jnp.int32(1), nbits - 1 - i)
            cnt = _colsum(jnp.where(kpos < cand, eq_scr[...], 0.0))
            return jnp.where(cnt <= need, cand, lim)

        lim = lax.fori_loop(0, nbits, tie_body, jnp.zeros((1, tq), jnp.int32))
        keep = jnp.logical_or(gt, jnp.logical_and(eq, kpos < lim))
        selt_scr[...] = jnp.where(keep, 0.0, MASKED).astype(BF16)

    er = lax.broadcasted_iota(jnp.int32, (tq, tq), 0)
    ec = lax.broadcasted_iota(jnp.int32, (tq, tq), 1)
    mask_bias = _dot_nt((er == ec).astype(BF16), selt_scr[...])

    qb = qb_ref[0]
    k_all = k2_ref[0]
    v_all = v2_ref[0]
    lo = _low_half((tq, LANES))
    outs = []
    for pr in range(H_B // 2):
        qn = _pair_rms(qb[:, pr * LANES:(pr + 1) * LANES], qg_ref[...]) * (HD_B ** -0.5 * LOG2E)
        s = _dot_nt(_split_halves(qn), k_all)
        s = (s.reshape(2, tq, s_keys) + mask_bias[None]).reshape(2 * tq, s_keys)
        r = _dot(_exp2_weights(s), v_all)
        o = r / pltpu.roll(r, LANES // 2, 1)
        outs.append(jnp.where(lo, o[0:tq], pltpu.roll(o[tq:2 * tq], LANES // 2, 1)))
    o_ref[0] = jnp.concatenate(outs, axis=1).astype(o_ref.dtype)


def _dsa(proj, q_gain, k_gain, cache, slot, n_slots, prev):
    b, l, _ = proj.shape
    p = 0 if cache is None else cache[0].shape[1]
    tq = min(LANES, l)
    s_k = p + l
    s_pad = -(-s_k // LANES) * LANES
    k_sel = min(TOPK_MAX, s_k // 4)

    kv_specs = [pl.BlockSpec((1, l, LANES), lambda i: (i, 0, COL_KV // LANES)),
                pl.BlockSpec((1, l, LANES), lambda i: (i, 0, COL_SM // LANES)),
                pl.BlockSpec((1, HD_B), lambda i: (0, 0))]
    args = [proj, proj, k_gain.reshape(1, HD_B)]
    if p:
        kv_specs += [pl.BlockSpec((1, p, HD_B), lambda i: (i, 0, 0))] * 3
        args += list(cache)
    aliases = {}
    if prev is not None:
        aliases = {len(args) + n: n for n in range(3)}
        kv_specs += [pl.BlockSpec(memory_space=pl.ANY)] * 3
        args += list(prev)
    row_out = jax.ShapeDtypeStruct((n_slots, b, l, HD_B), F32)
    dup_out = jax.ShapeDtypeStruct((b, s_pad, LANES), BF16)
    row_spec = pl.BlockSpec((1, 1, l, HD_B), lambda i: (slot, i, 0, 0))
    dup_spec = pl.BlockSpec((1, s_pad, LANES), lambda i: (i, 0, 0))
    k_b, v_b, ki_b, k2, v2, ki2 = pl.pallas_call(
        functools.partial(_dsa_prep_kernel, l=l, p=p, s_pad=s_pad, n_alias=len(aliases)),
        out_shape=(row_out, row_out, row_out, dup_out, dup_out, dup_out),
        grid=(b,),
        in_specs=kv_specs,
        out_specs=(row_spec, row_spec, row_spec, dup_spec, dup_spec, dup_spec),
        input_output_aliases=aliases,
        compiler_params=_cparams(("parallel",)),
        name="dsa_prep",
    )(*args)

    n_tiles = l // tq
    tiles_per_class = 1 if p == 0 else n_tiles
    qg2 = _dup(q_gain.reshape(1, HD_B))
    o_b = None
    for tile0 in range(0, n_tiles, tiles_per_class):
        nt = min(tiles_per_class, n_tiles - tile0)
        s_keys = min(s_pad, -(-(p + (tile0 + nt) * tq) // LANES) * LANES)
        aliased = o_b is not None
        o_b = pl.pallas_call(
            functools.partial(_dsa_kernel, tq=tq, tile0=tile0, p=p, s_keys=s_keys, k_sel=k_sel, aliased=aliased),
            out_shape=jax.ShapeDtypeStruct((b, l, H_B * HD_B), BF16),
            grid=(b, nt),
            in_specs=[pl.BlockSpec((1, tq, 512), lambda i, t, t0=tile0: (i, t0 + t, COL_QB // 512)),
                      pl.BlockSpec((1, tq, 256), lambda i, t, t0=tile0: (i, t0 + t, COL_QI // 256)),
                      pl.BlockSpec((1, tq, LANES), lambda i, t, t0=tile0: (i, t0 + t, COL_SM // LANES)),
                      pl.BlockSpec((1, LANES), lambda i, t: (0, 0)),
                      pl.BlockSpec((1, s_keys, LANES), lambda i, t: (i, 0, 0)),
                      pl.BlockSpec((1, s_keys, LANES), lambda i, t: (i, 0, 0)),
                      pl.BlockSpec((1, s_keys, LANES), lambda i, t: (i, 0, 0))]
                     + ([pl.BlockSpec(memory_space=pl.ANY)] if aliased else []),
            out_specs=pl.BlockSpec((1, tq, H_B * HD_B), lambda i, t, t0=tile0: (i, t0 + t, 0)),
            scratch_shapes=[pltpu.VMEM((s_keys, tq), jnp.int32), pltpu.VMEM((s_keys, tq), F32),
                            pltpu.VMEM((s_keys, tq), BF16)],
            input_output_aliases={7: 0} if aliased else {},
            compiler_params=_cparams(("parallel", "arbitrary")),
            name="dsa",
        )(proj, proj, proj, qg2, k2, v2, ki2, *([o_b] if aliased else []))
    return o_b, (k_b, v_b, ki_b)


def _band_kernel(*refs, tb, cg, has_cache, first_kept, n_alias):
    n_in = 8 if has_cache else 6
    q_ref, k_ref, v_ref, bias_ref, qg_ref, kg_ref = refs[:6]
    if has_cache:
        ck_ref, cv_ref = refs[6:8]
    o_ref, kn_ref, vn_ref, kwin, vwin = refs[n_in + n_alias:]
    j = pl.program_id(1)
    tbq = tb * CHUNK
    gq = cg * CHUNK
    wk = WINDOW + gq

    @pl.when(j == 0)
    def _():
        if has_cache:
            kwin[0:WINDOW, :] = ck_ref[0].astype(BF16)
            vwin[0:WINDOW, :] = cv_ref[0].astype(BF16)
        else:
            kwin[0:WINDOW, :] = jnp.zeros((WINDOW, H_C * HD_C), BF16)
            vwin[0:WINDOW, :] = jnp.zeros((WINDOW, H_C * HD_C), BF16)

    @pl.when(j > 0)
    def _():
        for i in range(WINDOW // tbq):
            kwin[i * tbq:(i + 1) * tbq, :] = kwin[(i + 1) * tbq:(i + 2) * tbq, :]
            vwin[i * tbq:(i + 1) * tbq, :] = vwin[(i + 1) * tbq:(i + 2) * tbq, :]

    vwin[WINDOW:WINDOW + tbq, :] = v_ref[0].astype(BF16)
    kcol = lax.broadcasted_iota(jnp.int32, (2 * gq, wk), 1)
    n_before = jnp.maximum(WINDOW - j * tbq, 0)
    for i in range(H_C // 2):
        cols = slice(i * LANES, (i + 1) * LANES)
        kwin[WINDOW:WINDOW + tbq, cols] = _pair_rms(k_ref[0, :, cols], kg_ref[...]).astype(BF16)
        qn = _pair_rms(q_ref[0, :, cols], qg_ref[...]) * (HD_C ** -0.5 * LOG2E)
        for g in range(tb // cg):
            r0 = g * gq
            s = _dot_nt(_split_halves(qn[r0:r0 + gq]), kwin[r0:r0 + wk, cols]) + bias_ref[i]
            if not has_cache:
                s = jnp.where(kcol >= n_before - r0, s, -jnp.inf)
            m = jnp.max(s, axis=-1, keepdims=True)
            e = jnp.exp2(s - m)
            o = _dot(e, vwin[r0:r0 + wk, cols]) / jnp.sum(e, axis=-1, keepdims=True)
            o_ref[0, r0:r0 + gq, cols] = _join_halves(o, gq).astype(o_ref.dtype)

    @pl.when(j >= first_kept)
    def _():
        rows = pl.ds(pl.multiple_of((j - first_kept) * tbq, tbq), tbq)
        vn_ref[0, 0, rows, :] = v_ref[0]
        for i in range(H_C // 2):
            cols = slice(i * LANES, (i + 1) * LANES)
            kn_ref[0, 0, rows, cols] = _pair_rms(k_ref[0, :, cols], kg_ref[...])


def _band(proj, rel_bias, q_gain, k_gain, cache, tb, cg, keep, slot, n_slots, prev):
    b, l, _ = proj.shape
    d = H_C * HD_C
    tbq = tb * CHUNK
    gq = cg * CHUNK
    wk = WINDOW + gq
    has_cache = cache is not None
    assert keep % tbq == 0 and l % tbq == 0
    first_kept = (l - keep) // tbq
    in_specs = [pl.BlockSpec((1, tbq, d), lambda i, j: (i, j, 0)),
                pl.BlockSpec((1, tbq, d), lambda i, j: (i, j, 1)),
                pl.BlockSpec((1, tbq, d), lambda i, j: (i, j, 2)),
                pl.BlockSpec((H_C // 2, 2 * gq, wk), lambda i, j: (0, 0, 0)),
                pl.BlockSpec((1, LANES), lambda i, j: (0, 0)),
                pl.BlockSpec((1, LANES), lambda i, j: (0, 0))]
    args = [proj, proj, proj, _band_bias(rel_bias, cg), _dup(q_gain.reshape(1, HD_C)), _dup(k_gain.reshape(1, HD_C))]
    if has_cache:
        in_specs += [pl.BlockSpec((1, WINDOW, d), lambda i, j: (i, 0, 0))] * 2
        args += [cache[0].reshape(b, WINDOW, d), cache[1].reshape(b, WINDOW, d)]
    aliases = {}
    if prev is not None:
        aliases = {len(args): 1, len(args) + 1: 2}
        in_specs += [pl.BlockSpec(memory_space=pl.ANY)] * 2
        args += list(prev)
    cache_out = jax.ShapeDtypeStruct((n_slots, b, keep, d), F32)
    cache_spec = pl.BlockSpec((1, 1, keep, d), lambda i, j: (slot, i, 0, 0))
    o_c, kn, vn = pl.pallas_call(
        functools.partial(_band_kernel, tb=tb, cg=cg, has_cache=has_cache, first_kept=first_kept,
                          n_alias=len(aliases)),
        out_shape=(jax.ShapeDtypeStruct((b, l, d), BF16), cache_out, cache_out),
        grid=(b, l // tbq),
        in_specs=in_specs,
        out_specs=(pl.BlockSpec((1, tbq, d), lambda i, j: (i, j, 0)), cache_spec, cache_spec),
        scratch_shapes=[pltpu.VMEM((WINDOW + tbq, d), BF16), pltpu.VMEM((WINDOW + tbq, d), BF16)],
        input_output_aliases=aliases,
        compiler_params=_cparams(("parallel", "arbitrary")),
        name="band",
    )(*args)
    return o_c, (kn, vn)


def _band_bias(rel_bias, cg):
    gq = cg * CHUNK
    wk = WINDOW + gq
    n = wk + gq - 1
    dist = WINDOW + gq - 1 - jnp.arange(n)
    seq = rel_bias[:, jnp.clip(dist, -(CHUNK - 1), REL_CLIP) + (CHUNK - 1)].astype(F32) * LOG2E
    period = jnp.roll(jnp.pad(seq, ((0, 0), (0, 1))), -(gq - 1), axis=1)
    table = jnp.tile(period, (1, gq))[:, :gq * n].reshape(H_C, gq, n)[:, :, :wk]
    qc = jnp.arange(gq) // CHUNK
    kc = jnp.arange(wk) // CHUNK
    inband = jnp.logical_and(kc[None, :] >= qc[:, None], kc[None, :] <= qc[:, None] + BAND_CHUNKS)
    return jnp.where(inband[None], table, -jnp.inf).reshape(H_C // 2, 2 * gq, wk)


def _ffn_kernel(*refs, n_mix, nf, nm, nb, tm, tf, final):
    x_ref = refs[0]
    mix_refs = refs[1:1 + n_mix]
    wo_refs = refs[1 + n_mix:1 + 2 * n_mix]
    rest = refs[1 + 2 * n_mix:]
    if final:
        g_ref, wa_ref, wg_ref, cw_ref, cb_ref, wd_ref, st0_ref, gf_ref = rest[:8]
        rest = rest[8:]
    else:
        g_ref, wa_ref, wg_ref, cw_ref, cb_ref, wd_ref, st0_ref = rest[:7]
        gf_ref = None
        rest = rest[7:]
    out_ref, stout_ref, abuf, cbuf, ubuf = rest
    mi = pl.program_id(1)
    rows = nb * tm
    bsel = (slice(None),) if nb > 1 else (0,)

    def flat(ref):
        return ref[...].reshape(rows, ref.shape[-1]) if nb > 1 else ref[0]

    def per_stream(z):
        return z.reshape(nb, tm, z.shape[-1]) if nb > 1 else z

    xr = flat(x_ref)
    for m_ref, w_ref in zip(mix_refs, wo_refs):
        xr = xr + jnp.dot(flat(m_ref), w_ref[...], preferred_element_type=F32)
    hb = _rms(xr, g_ref[...]).astype(BF16)

    @pl.when(mi == 0)
    def _():
        cbuf[...] = st0_ref[...]

    for f in range(nf):
        cols = slice(f * tf, (f + 1) * tf)
        slot = f % 2
        a = per_stream(jnp.dot(hb, wa_ref[:, cols], preferred_element_type=F32))
        gate = per_stream(jnp.dot(hb, wg_ref[:, cols], preferred_element_type=F32))
        abuf[(slot, *bsel, slice(0, 8))] = cbuf[(*bsel, slice(None), cols)]
        abuf[(slot, *bsel, slice(8, 8 + tm))] = a
        ac = (a * cw_ref[2:3, cols] + abuf[(slot, *bsel, slice(7, 7 + tm))] * cw_ref[1:2, cols]
              + abuf[(slot, *bsel, slice(6, 6 + tm))] * cw_ref[0:1, cols] + cb_ref[:, cols])
        cbuf[(*bsel, slice(None), cols)] = abuf[(slot, *bsel, slice(tm, tm + 8))]
        ubuf[:, cols] = (_silu(ac) * gate).reshape(rows, tf).astype(BF16)

    yv = xr + jnp.dot(ubuf[...], wd_ref[...], preferred_element_type=F32)
    if final:
        yv = _rms(yv, gf_ref[...])
    out_ref[...] = yv.reshape(nb, tm, yv.shape[-1])

    @pl.when(mi == nm - 1)
    def _():
        stout_ref[...] = cbuf[:, 8 - (CONV_F - 1):8, :]


def _ffn_vmem_bytes(rows, d, dff, tf, mix_cols):
    weights = 2 * (3 * d * dff + sum(mix_cols) * d)
    tiles = 2 * (2 * rows * d * 4 + rows * sum(mix_cols) * 2)
    scratch = 2 * (rows + 64) * tf * 4 + 64 * dff * 4
    temps = rows * d * (4 + 4 + 2) + 6 * rows * tf * 4
    return weights + tiles + scratch + temps


def _ffn(x, mixes, wos, gain, wa, wg, conv_w, conv_b, wd, st0, gain_final, nb, tm, tf):
    b, l, d = x.shape
    dff = wa.shape[1]
    nf = dff // tf
    nm = l // tm
    n_mix = len(mixes)
    final = gain_final is not None
    in_specs = [pl.BlockSpec((nb, tm, d), lambda i, m: (i, m, 0))]
    in_specs += [pl.BlockSpec((nb, tm, mx.shape[2]), lambda i, m: (i, m, 0)) for mx in mixes]
    in_specs += [pl.BlockSpec(w.shape, lambda i, m: (0, 0)) for w in wos]
    in_specs += [pl.BlockSpec((1, d), lambda i, m: (0, 0)),
                 pl.BlockSpec((d, dff), lambda i, m: (0, 0)),
                 pl.BlockSpec((d, dff), lambda i, m: (0, 0)),
                 pl.BlockSpec((CONV_F, dff), lambda i, m: (0, 0)),
                 pl.BlockSpec((1, dff), lambda i, m: (0, 0)),
                 pl.BlockSpec((dff, d), lambda i, m: (0, 0)),
                 pl.BlockSpec((nb, 8, dff), lambda i, m: (i, 0, 0))]
    args = [x, *mixes, *wos, gain.reshape(1, d), wa, wg, conv_w, conv_b.reshape(1, dff), wd, st0]
    if final:
        in_specs.append(pl.BlockSpec((1, d), lambda i, m: (0, 0)))
        args.append(gain_final.reshape(1, d))
    return pl.pallas_call(
        functools.partial(_ffn_kernel, n_mix=n_mix, nf=nf, nm=nm, nb=nb, tm=tm, tf=tf, final=final),
        out_shape=(jax.ShapeDtypeStruct((b, l, d), F32),
                   jax.ShapeDtypeStruct((b, CONV_F - 1, dff), F32)),
        grid=(b // nb, nm),
        in_specs=in_specs,
        out_specs=(pl.BlockSpec((nb, tm, d), lambda i, m: (i, m, 0)),
                   pl.BlockSpec((nb, CONV_F - 1, dff), lambda i, m: (i, 0, 0))),
        scratch_shapes=[pltpu.VMEM((2, nb, tm + 8, tf), F32), pltpu.VMEM((nb, 8, dff), F32),
                        pltpu.VMEM((nb * tm, dff), BF16)],
        compiler_params=_cparams(("parallel", "arbitrary"),
                                 min(V7X_VMEM_USABLE, _ffn_vmem_bytes(nb * tm, d, dff, tf, [mx.shape[2] for mx in mixes]))),
        name="ffn",
    )(*args)


def _pad_rows_front(a, rows):
    return jnp.pad(a, ((0, 0), (rows - a.shape[1], 0), (0, 0)))


def _trunk(x, past, w):
    b, l, d = x.shape
    depth = w["norm_mix"].shape[0]
    first = past is None
    n_chunks = l // CHUNK
    tm_proj = min(512, b * l)
    tm_ffn = min(512, l)
    nb_ffn = math.gcd(b, max(1, 512 // tm_ffn))
    tf = 256
    nc = min(8, n_chunks)
    tb = min(8, n_chunks)
    cg = min(2, tb)
    dff = w["ffn_w_a"].shape[2]
    n_even, n_odd = (depth + 1) // 2, depth // 2
    keep = min(WINDOW, l) if first else l
    dsa_new = band_new = None
    out = {k: [] for k in ("dn_S", "dn_conv", "ffn_conv")}
    for layer in range(depth):
        x2d = x.reshape(b * l, d)
        if layer % 2 == 0:
            e = layer // 2
            proj = _norm_proj(x2d, w["norm_mix"][layer], w["w_in_even"][e], tm_proj).reshape(b, l, EVEN_COLS)
            if first:
                conv0 = jnp.zeros((b, 8, C_A), F32)
                s0 = jnp.zeros((b, H_A, DK_A, DV_A), F32)
                cache = None
            else:
                conv0 = _pad_rows_front(past["state_dn_conv"][e], 8)
                s0 = past["state_dn_S"][e]
                cache = (past["cache_dsa_k"][e], past["cache_dsa_v"][e], past["cache_dsa_kidx"][e])
            o_a, buf_a, s_a = _deltanet(proj, conv0, s0, w["dn_conv_w"][e], w["dn_a_log"][e],
                                        w["dn_dt_bias"][e], w["dn_o_gain"][e], nc)
            o_b, dsa_new = _dsa(proj, w["dsa_q_gain"][e], w["dsa_k_gain"][e], cache, e, n_even, dsa_new)
            mixes = [o_a, o_b]
            wos = [w["w_out_even"][e][:H_A * DV_A], w["w_out_even"][e][H_A * DV_A:]]
            out["dn_S"].append(s_a)
            out["dn_conv"].append(buf_a)
        else:
            jj = layer // 2
            proj = _norm_proj(x2d, w["norm_mix"][layer], w["w_in_odd"][jj], tm_proj).reshape(b, l, 3 * H_C * HD_C)
            cache = None if first else (past["cache_band_k"][jj], past["cache_band_v"][jj])
            o_c, band_new = _band(proj, w["band_rel_bias"][jj], w["band_q_gain"][jj], w["band_k_gain"][jj], cache,
                                  tb, cg, keep, jj, n_odd, band_new)
            mixes = [o_c]
            wos = [w["w_out_odd"][jj]]
        st0 = (jnp.zeros((b, 8, dff), F32) if first else _pad_rows_front(past["state_ffn_conv"][layer], 8))
        gfin = w["norm_final"] if layer == depth - 1 else None
        x, fbuf = _ffn(x, mixes, wos, w["norm_ffn"][layer], w["ffn_w_a"][layer], w["ffn_w_g"][layer],
                       w["ffn_conv_w"][layer], w["ffn_conv_b"][layer], w["ffn_w_down"][layer], st0, gfin,
                       nb_ffn, tm_ffn, tf)
        out["ffn_conv"].append(fbuf)
    states = {k: jnp.stack(v) for k, v in out.items()}
    states["dsa_k"], states["dsa_v"], states["dsa_kidx"] = dsa_new
    states["band_k"], states["band_v"] = (a.reshape(n_odd, b, keep, H_C, HD_C) for a in band_new)
    return x, states


def _prep_weights(norm_mix, norm_ffn, norm_final, w_in_even, dn_conv_w, dn_a_log, dn_dt_bias, dn_o_gain,
                  dsa_q_gain, dsa_k_gain, w_out_even, w_in_odd, band_q_gain, band_k_gain, band_rel_bias,
                  w_out_odd, ffn_w_a, ffn_w_g, ffn_conv_w, ffn_conv_b, ffn_w_down):
    o_ba = 4 * H_A * DK_A
    o_qb = o_ba + 2 * H_A
    o_kb = o_qb + H_B * HD_B
    o_qi = o_kb + 2 * HD_B
    o_ki = o_qi + H_IDX * D_IDX
    o_wi = o_ki + D_IDX
    o_end = o_wi + H_IDX
    wie = w_in_even
    pad = jnp.zeros(wie.shape[:2] + (EVEN_COLS - o_end,), wie.dtype)
    w_even = jnp.concatenate([wie[..., :o_ba], wie[..., o_qb:o_kb], wie[..., o_qi:o_ki], wie[..., o_kb:o_qi],
                              wie[..., o_ki:o_wi], wie[..., o_ba:o_qb], wie[..., o_wi:o_end], pad], axis=-1)
    return dict(norm_mix=norm_mix, norm_ffn=norm_ffn, norm_final=norm_final,
                w_in_even=w_even.astype(BF16), dn_conv_w=dn_conv_w, dn_a_log=dn_a_log, dn_dt_bias=dn_dt_bias,
                dn_o_gain=dn_o_gain, dsa_q_gain=dsa_q_gain, dsa_k_gain=dsa_k_gain,
                w_out_even=w_out_even.astype(BF16), w_in_odd=w_in_odd.astype(BF16),
                band_q_gain=band_q_gain, band_k_gain=band_k_gain, band_rel_bias=band_rel_bias,
                w_out_odd=w_out_odd.astype(BF16),
                ffn_w_a=ffn_w_a.astype(BF16), ffn_w_g=ffn_w_g.astype(BF16), ffn_conv_w=ffn_conv_w,
                ffn_conv_b=ffn_conv_b, ffn_w_down=ffn_w_down.astype(BF16))


def kernel(x_prompt, x_sample, state_dn_S, state_dn_conv, cache_dsa_k, cache_dsa_v, cache_dsa_kidx, cache_band_k, cache_band_v, state_ffn_conv, norm_mix, norm_ffn, norm_final, w_in_even, dn_conv_w, dn_a_log, dn_dt_bias, dn_o_gain, dsa_q_gain, dsa_k_gain, w_out_even, w_in_odd, band_q_gain, band_k_gain, band_rel_bias, w_out_odd, ffn_w_a, ffn_w_g, ffn_conv_w, ffn_conv_b, ffn_w_down):
    assert cache_band_k.shape[2] == WINDOW, "band cache must hold exactly one window"
    w = _prep_weights(norm_mix, norm_ffn, norm_final, w_in_even, dn_conv_w, dn_a_log, dn_dt_bias, dn_o_gain,
                      dsa_q_gain, dsa_k_gain, w_out_even, w_in_odd, band_q_gain, band_k_gain, band_rel_bias,
                      w_out_odd, ffn_w_a, ffn_w_g, ffn_conv_w, ffn_conv_b, ffn_w_down)
    past = dict(state_dn_S=state_dn_S, state_dn_conv=state_dn_conv, cache_dsa_k=cache_dsa_k,
                cache_dsa_v=cache_dsa_v, cache_dsa_kidx=cache_dsa_kidx, cache_band_k=cache_band_k,
                cache_band_v=cache_band_v, state_ffn_conv=state_ffn_conv)
    y_p, sp = _trunk(x_prompt, None, w)
    y_s, ss = _trunk(x_sample, past, w)
    return (y_p, y_s,
            sp["dn_S"], ss["dn_S"], sp["dn_conv"], ss["dn_conv"],
            sp["dsa_k"], ss["dsa_k"], sp["dsa_v"], ss["dsa_v"], sp["dsa_kidx"], ss["dsa_kidx"],
            sp["band_k"], ss["band_k"], sp["band_v"], ss["band_v"],
            sp["ffn_conv"], ss["ffn_conv"])
```

```python
import functools
import math

import jax
import jax.numpy as jnp
from jax import lax
from jax.experimental import pallas as pl
from jax.experimental.pallas import tpu as pltpu

F32 = jnp.float32
BF16 = jnp.bfloat16
EPS = 1e-6
INT_MIN = -(2 ** 31)
MASKED = -1e30
LOG2E = math.log2(math.e)

CHUNK = 64
H_A, DK_A, DV_A, CONV_A = 4, 128, 128, 4
C_A = 3 * H_A * DK_A
H_B, HD_B, H_IDX, D_IDX, TOPK_MAX = 8, 64, 4, 64, 256
H_C, HD_C, BAND_CHUNKS, REL_CLIP = 16, 64, 8, 128
WINDOW = BAND_CHUNKS * CHUNK
CONV_F = 3

EVEN_COLS = 3072
COL_Z = 1536
COL_QB = 2048
COL_QI = 2560
COL_KV = 2816
COL_SM = 2944
SM_BETA, SM_A, SM_WI = 64, 68, 72

LANES = 128
VMEM_LIMIT = 48 * 1024 * 1024
V7X_VMEM_USABLE = 56 * 1024 * 1024


def _cparams(sem, vmem=VMEM_LIMIT):
    return pltpu.CompilerParams(dimension_semantics=sem, vmem_limit_bytes=vmem)


def _dot(a, b):
    return jnp.dot(a.astype(BF16), b.astype(BF16), preferred_element_type=F32)


def _dot_nt(a, b):
    return lax.dot_general(a.astype(BF16), b.astype(BF16), (((1,), (1,)), ((), ())),
                           preferred_element_type=F32)


def _dot_f32(a, b):
    return jnp.dot(a, b, preferred_element_type=F32, precision=lax.Precision.HIGHEST)


def _dot_nt_f32(a, b):
    return lax.dot_general(a, b, (((1,), (1,)), ((), ())), preferred_element_type=F32,
                           precision=lax.Precision.HIGHEST)


def _rms(x, gain):
    return x * lax.rsqrt(jnp.mean(x * x, axis=-1, keepdims=True) + EPS) * gain


def _silu(x):
    return x * jax.nn.sigmoid(x)


def _low_half(shape):
    return lax.broadcasted_iota(jnp.int32, shape, len(shape) - 1) < (LANES // 2)


def _pair_rms(x, gain2):
    lo = _low_half(x.shape)
    sq = x * x
    ms_lo = jnp.sum(jnp.where(lo, sq, 0.0), axis=-1, keepdims=True) * (2.0 / LANES)
    ms_hi = jnp.sum(jnp.where(lo, 0.0, sq), axis=-1, keepdims=True) * (2.0 / LANES)
    return x * jnp.where(lo, lax.rsqrt(ms_lo + EPS), lax.rsqrt(ms_hi + EPS)) * gain2


def _split_halves(x):
    lo = _low_half(x.shape)
    return jnp.concatenate([jnp.where(lo, x, 0.0), jnp.where(lo, 0.0, x)], axis=0)


def _join_halves(o, rows):
    return jnp.where(_low_half((rows, LANES)), o[0:rows], o[rows:2 * rows])


def _exp2_weights(s):
    return jnp.exp2(s - jnp.max(s, axis=-1, keepdims=True)).astype(BF16)


def _dup(x):
    return jnp.concatenate([x, x], axis=1)


def _norm_proj_kernel(x_ref, g_ref, w_ref, o_ref):
    o_ref[...] = jnp.dot(_rms(x_ref[...], g_ref[...]).astype(BF16), w_ref[...], preferred_element_type=F32)


def _norm_proj(x2d, gain, w_bf16, tm):
    m, d = x2d.shape
    n = w_bf16.shape[1]
    return pl.pallas_call(
        _norm_proj_kernel,
        out_shape=jax.ShapeDtypeStruct((m, n), F32),
        grid=(m // tm,),
        in_specs=[pl.BlockSpec((tm, d), lambda i: (i, 0)),
                  pl.BlockSpec((1, d), lambda i: (0, 0)),
                  pl.BlockSpec((d, n), lambda i: (0, 0))],
        out_specs=pl.BlockSpec((tm, n), lambda i: (i, 0)),
        compiler_params=_cparams(("parallel",)),
        name="norm_proj",
    )(x2d, gain.reshape(1, d), w_bf16)


def _tri_inv_lockstep(ms, eye_f):
    ts = [eye_f - m for m in ms]
    pbs = [(-m).astype(BF16) for m in ms]
    for _ in range(5):
        pbs = [jnp.dot(pb, pb, preferred_element_type=F32).astype(BF16) for pb in pbs]
        ts = [t + jnp.dot(t.astype(BF16), pb, preferred_element_type=F32) for t, pb in zip(ts, pbs)]
    return ts


def _deltanet_kernel(qkv_ref, z_ref, sm_ref, conv0_ref, s0_ref, cw_ref, alog_ref, dtb_ref, og_ref,
                     o_ref, convout_ref, sout_ref, xbuf, s_scr, *, nc):
    c = pl.program_id(1)
    t = nc * CHUNK

    @pl.when(c == 0)
    def _():
        xbuf[0:8, :] = conv0_ref[0]
        for h in range(H_A):
            s_scr[:, h * DK_A:(h + 1) * DK_A] = jnp.transpose(s0_ref[0, h])

    x = qkv_ref[0]
    xbuf[8:8 + t, :] = x
    w = cw_ref[...]
    y = (x * w[3:4] + xbuf[7:7 + t, :] * w[2:3] + xbuf[6:6 + t, :] * w[1:2] + xbuf[5:5 + t, :] * w[0:1])
    convout_ref[0] = xbuf[t + 5:t + 8, :]
    xbuf[0:8, :] = xbuf[t:t + 8, :]
    y = _silu(y)

    sm = sm_ref[0]
    beta = jax.nn.sigmoid(sm)
    aa = sm + dtb_ref[...]
    softplus = jnp.maximum(aa, 0.0) + jnp.log1p(jnp.exp(-jnp.abs(aa)))
    g = -jnp.exp(alog_ref[...]) * softplus

    ht = H_A * CHUNK
    hk = H_A * DK_A
    ri = lax.broadcasted_iota(jnp.int32, (ht, ht), 0)
    ci = lax.broadcasted_iota(jnp.int32, (ht, ht), 1)
    same = jnp.right_shift(ri, 6) == jnp.right_shift(ci, 6)
    trilbd = jnp.logical_and(same, ri >= ci)
    strictbd = jnp.logical_and(same, ri > ci)
    eye_f = (ri == ci).astype(F32)
    r2 = lax.broadcasted_iota(jnp.int32, (2 * ht, hk), 0)
    c2 = lax.broadcasted_iota(jnp.int32, (2 * ht, hk), 1)
    bd2 = (jnp.right_shift(r2, 6) & (H_A - 1)) == jnp.right_shift(c2, 7)
    bd1 = bd2[0:ht]
    r64 = lax.broadcasted_iota(jnp.int32, (CHUNK, CHUNK), 0)
    c64 = lax.broadcasted_iota(jnp.int32, (CHUNK, CHUNK), 1)
    tril_f = (r64 >= c64).astype(F32)
    og = og_ref[...]

    def heads_on_rows(a, r0, off, width):
        return jnp.concatenate([a[r0:r0 + CHUNK, off + h * width:off + (h + 1) * width] for h in range(H_A)],
                               axis=0)

    pre = []
    for cidx in range(nc):
        r0 = cidx * CHUNK
        gcum = _dot_f32(tril_f, g[r0:r0 + CHUNK])
        q = heads_on_rows(y, r0, 0, DK_A)
        k = heads_on_rows(y, r0, hk, DK_A)
        v = heads_on_rows(y, r0, 2 * hk, DV_A)
        q = q * lax.rsqrt(jnp.sum(q * q, axis=-1, keepdims=True) + EPS) * (DK_A ** -0.5)
        k = k * lax.rsqrt(jnp.sum(k * k, axis=-1, keepdims=True) + EPS)
        bcol = heads_on_rows(beta, r0, SM_BETA, 1)
        gcol = heads_on_rows(gcum, 0, SM_A, 1)
        glast = [gcum[CHUNK - 1:CHUNK, SM_A + h:SM_A + h + 1] for h in range(H_A)]
        gl_col = jnp.concatenate([jnp.broadcast_to(x_, (CHUNK, 1)) for x_ in glast], axis=0)
        gl_row = jnp.concatenate([jnp.broadcast_to(x_, (1, DK_A)) for x_ in glast], axis=1)
        grow = jnp.transpose(jnp.broadcast_to(gcol, (ht, LANES)))[0:1, :]
        decay = jnp.exp(jnp.where(trilbd, gcol - grow, -jnp.inf))
        kb = k * bcol
        a_full = _dot_nt(jnp.concatenate([kb, q], axis=0), k)
        kd = k * jnp.exp(gl_col - gcol)
        pre.append(dict(
            m=jnp.where(strictbd, a_full[0:ht] * decay, 0.0),
            attn=(a_full[ht:2 * ht] * decay).astype(BF16),
            rhs=jnp.concatenate([v * bcol, kb * jnp.exp(gcol)], axis=1).astype(BF16),
            qg=q * jnp.exp(gcol),
            kd_bd=jnp.where(bd1, jnp.concatenate([kd] * H_A, axis=1), 0.0).astype(BF16),
            decay_row=jnp.exp(gl_row)))

    tmats = _tri_inv_lockstep([c["m"] for c in pre], eye_f)
    uws = [jnp.dot(tm.astype(BF16), c["rhs"], preferred_element_type=F32) for tm, c in zip(tmats, pre)]

    for cidx in range(nc):
        r0 = cidx * CHUNK
        c, uw = pre[cidx], uws[cidx]
        u = uw[:, 0:DV_A]
        wq = jnp.concatenate([uw[:, DV_A:DV_A + DK_A], c["qg"]], axis=0)
        wq_bd = jnp.where(bd2, jnp.concatenate([wq] * H_A, axis=1), 0.0)
        st_old = s_scr[...]
        ws_qs = _dot_nt(wq_bd, st_old)
        v_new = u - ws_qs[0:ht]
        s_scr[...] = st_old * c["decay_row"] + _dot(jnp.transpose(v_new), c["kd_bd"])
        o = ws_qs[ht:2 * ht] + _dot(c["attn"], v_new)
        zz = heads_on_rows(z_ref[0], r0, 0, DV_A)
        res = (_rms(o, og) * _silu(zz)).astype(o_ref.dtype)
        for h in range(H_A):
            o_ref[0, r0:r0 + CHUNK, h * DV_A:(h + 1) * DV_A] = res[h * CHUNK:(h + 1) * CHUNK]

    for h in range(H_A):
        sout_ref[0, h] = jnp.transpose(s_scr[:, h * DK_A:(h + 1) * DK_A])


def _deltanet(proj, conv0, s0, conv_w, a_log, dt_bias, o_gain, nc):
    b, l, _ = proj.shape
    t = nc * CHUNK
    pad = jnp.zeros((1, LANES), F32)
    alog128 = lax.dynamic_update_slice(pad, a_log.reshape(1, H_A), (0, SM_A))
    dtb128 = lax.dynamic_update_slice(pad, dt_bias.reshape(1, H_A), (0, SM_A))
    return pl.pallas_call(
        functools.partial(_deltanet_kernel, nc=nc),
        out_shape=(jax.ShapeDtypeStruct((b, l, H_A * DV_A), BF16),
                   jax.ShapeDtypeStruct((b, CONV_A - 1, C_A), F32),
                   jax.ShapeDtypeStruct((b, H_A, DK_A, DV_A), F32)),
        grid=(b, l // t),
        in_specs=[pl.BlockSpec((1, t, C_A), lambda i, c: (i, c, 0)),
                  pl.BlockSpec((1, t, 512), lambda i, c: (i, c, COL_Z // 512)),
                  pl.BlockSpec((1, t, LANES), lambda i, c: (i, c, COL_SM // LANES)),
                  pl.BlockSpec((1, 8, C_A), lambda i, c: (i, 0, 0)),
                  pl.BlockSpec((1, H_A, DK_A, DV_A), lambda i, c: (i, 0, 0, 0)),
                  pl.BlockSpec((CONV_A, C_A), lambda i, c: (0, 0)),
                  pl.BlockSpec((1, LANES), lambda i, c: (0, 0)),
                  pl.BlockSpec((1, LANES), lambda i, c: (0, 0)),
                  pl.BlockSpec((1, DV_A), lambda i, c: (0, 0))],
        out_specs=(pl.BlockSpec((1, t, H_A * DV_A), lambda i, c: (i, c, 0)),
                   pl.BlockSpec((1, CONV_A - 1, C_A), lambda i, c: (i, 0, 0)),
                   pl.BlockSpec((1, H_A, DK_A, DV_A), lambda i, c: (i, 0, 0, 0))),
        scratch_shapes=[pltpu.VMEM((t + 8, C_A), F32), pltpu.VMEM((DV_A, H_A * DK_A), F32)],
        compiler_params=_cparams(("parallel", "arbitrary")),
        name="deltanet",
    )(proj, proj, proj, conv0, s0, conv_w, alog128, dtb128, o_gain.reshape(1, DV_A))


def _dsa_prep_kernel(*refs, l, p, s_pad):
    n_in = 6 if p else 3
    kv_ref, sm_ref, kg_ref = refs[:3]
    if p:
        ck_ref, cv_ref, cki_ref = refs[3:6]
    kout_ref, vout_ref, kiout_ref, k2_ref, v2_ref, ki2_ref = refs[n_in + 3:]
    kv = kv_ref[0]
    kn = _rms(kv[:, 0:HD_B], kg_ref[...])
    vraw = kv[:, HD_B:2 * HD_B]
    kidx = sm_ref[0][:, 0:D_IDX]
    kout_ref[0, 0] = kn
    vout_ref[0, 0] = vraw
    kiout_ref[0, 0] = kidx

    def with_ones(a):
        return jnp.concatenate([a, jnp.ones_like(a)], axis=1)

    if p:
        k2_ref[0, 0:p, :] = _dup(ck_ref[0]).astype(BF16)
        v2_ref[0, 0:p, :] = with_ones(cv_ref[0]).astype(BF16)
        ki2_ref[0, 0:p, :] = _dup(cki_ref[0]).astype(BF16)
    k2_ref[0, p:p + l, :] = _dup(kn).astype(BF16)
    v2_ref[0, p:p + l, :] = with_ones(vraw).astype(BF16)
    ki2_ref[0, p:p + l, :] = _dup(kidx).astype(BF16)
    if s_pad > p + l:
        zpad = jnp.zeros((s_pad - p - l, LANES), BF16)
        k2_ref[0, p + l:s_pad, :] = zpad
        v2_ref[0, p + l:s_pad, :] = zpad
        ki2_ref[0, p + l:s_pad, :] = zpad


def _colsum(x):
    s, n = x.shape
    return jnp.sum(jnp.sum(x.reshape(s // CHUNK, CHUNK, n), axis=0), axis=0, keepdims=True)


def _dsa_kernel(qb_ref, qi_ref, smq_ref, qg_ref, k2_ref, v2_ref, ki2_ref, _o_in_ref, o_ref, key_scr, eq_scr,
                selt_scr, *, tq, tile0, p, s_keys, k_sel):
    tg = tile0 + pl.program_id(1)

    smq = smq_ref[0]
    sel_r = lax.broadcasted_iota(jnp.int32, (8, LANES), 0)
    sel_c = lax.broadcasted_iota(jnp.int32, (8, LANES), 1)
    pick = (sel_c == sel_r + SM_WI).astype(F32)
    wi_t = _dot_nt_f32(pick, smq) * (H_IDX ** -0.5)
    qi = qi_ref[0]
    ki_all = ki2_ref[0]
    qheads = jnp.concatenate([_split_halves(qi[:, pr * LANES:(pr + 1) * LANES] * (D_IDX ** -0.5))
                              for pr in range(H_IDX // 2)], axis=0)
    score = jnp.zeros((s_keys, tq), F32)
    if tq % LANES == 0:
        d_all = _dot_nt(ki_all, qheads)
        for h in range(H_IDX):
            score = score + jnp.maximum(d_all[:, h * tq:(h + 1) * tq], 0.0) * wi_t[h:h + 1, :]
    else:
        for h in range(H_IDX):
            score = score + jnp.maximum(_dot_nt(ki_all, qheads[h * tq:(h + 1) * tq]), 0.0) * wi_t[h:h + 1, :]

    bits = pltpu.bitcast(score + 0.0, jnp.int32)
    key = bits ^ (jnp.right_shift(bits, 31) & 0x7FFFFFFF)
    kpos = lax.broadcasted_iota(jnp.int32, (s_keys, tq), 0)
    qpos = tg * tq + lax.broadcasted_iota(jnp.int32, (1, tq), 1)
    n_adm = p + (jnp.right_shift(qpos, 6) + 1) * CHUNK
    key_scr[...] = jnp.where(kpos < n_adm, key, INT_MIN)

    kf = float(k_sel)

    def count_ge(cand):
        return _colsum(jnp.where(key_scr[...] >= cand, 1.0, 0.0))

    thr0 = jnp.where(count_ge(jnp.zeros((1, tq), jnp.int32)) >= kf, 0, INT_MIN).astype(jnp.int32)

    def thr_body(i, thr):
        cand = thr | jnp.left_shift(jnp.int32(1), 30 - i)
        return jnp.where(count_ge(cand) >= kf, cand, thr)

    thr = lax.fori_loop(0, 31, thr_body, thr0)

    ge = jnp.logical_and(key_scr[...] >= thr, kpos < n_adm)
    surplus = _colsum(jnp.where(ge, 1.0, 0.0)) - kf
    selt_scr[...] = jnp.where(ge, 0.0, MASKED).astype(BF16)

    @pl.when(jnp.max(surplus) > 0.0)
    def _():
        keys = key_scr[...]
        gt = keys > thr
        eq = jnp.logical_and(keys == thr, kpos < n_adm)
        need = kf - _colsum(jnp.where(gt, 1.0, 0.0))
        eq_scr[...] = jnp.where(eq, 1.0, 0.0)
        nbits = int(s_keys).bit_length()

        def tie_body(i, lim):
            cand = lim | jnp.left_shift(jnp.int32(1), nbits - 1 - i)
            cnt = _colsum(jnp.where(kpos < cand, eq_scr[...], 0.0))
            return jnp.where(cnt <= need, cand, lim)

        lim = lax.fori_loop(0, nbits, tie_body, jnp.zeros((1, tq), jnp.int32))
        keep = jnp.logical_or(gt, jnp.logical_and(eq, kpos < lim))
        selt_scr[...] = jnp.where(keep, 0.0, MASKED).astype(BF16)

    er = lax.broadcasted_iota(jnp.int32, (tq, tq), 0)
    ec = lax.broadcasted_iota(jnp.int32, (tq, tq), 1)
    mask_bias = _dot_nt((er == ec).astype(BF16), selt_scr[...])

    qb = qb_ref[0]
    k_all = k2_ref[0]
    v_all = v2_ref[0]
    lo = _low_half((tq, LANES))
    outs = []
    for pr in range(H_B // 2):
        qn = _pair_rms(qb[:, pr * LANES:(pr + 1) * LANES], qg_ref[...]) * (HD_B ** -0.5 * LOG2E)
        s = _dot_nt(_split_halves(qn), k_all)
        s = (s.reshape(2, tq, s_keys) + mask_bias[None]).reshape(2 * tq, s_keys)
        r = _dot(_exp2_weights(s), v_all)
        o = r / pltpu.roll(r, LANES // 2, 1)
        outs.append(jnp.where(lo, o[0:tq], pltpu.roll(o[tq:2 * tq], LANES // 2, 1)))
    o_ref[0] = jnp.concatenate(outs, axis=1).astype(o_ref.dtype)


def _dsa(proj, q_gain, k_gain, cache, slot, n_slots, prev):
    b, l, _ = proj.shape
    p = 0 if cache is None else cache[0].shape[1]
    tq = min(LANES, l)
    s_k = p + l
    s_pad = -(-s_k // LANES) * LANES
    k_sel = min(TOPK_MAX, s_k // 4)

    kv_specs = [pl.BlockSpec((1, l, LANES), lambda i: (i, 0, COL_KV // LANES)),
                pl.BlockSpec((1, l, LANES), lambda i: (i, 0, COL_SM // LANES)),
                pl.BlockSpec((1, HD_B), lambda i: (0, 0))]
    args = [proj, proj, k_gain.reshape(1, HD_B)]
    if p:
        kv_specs += [pl.BlockSpec((1, p, HD_B), lambda i: (i, 0, 0))] * 3
        args += list(cache)
    row_out = jax.ShapeDtypeStruct((n_slots, b, l, HD_B), F32)
    if prev is None:
        prev = tuple(jnp.zeros(row_out.shape, F32) for _ in range(3))
    aliases = {len(args) + n: n for n in range(3)}
    kv_specs += [pl.BlockSpec(memory_space=pl.ANY)] * 3
    args += list(prev)
    dup_out = jax.ShapeDtypeStruct((b, s_pad, LANES), BF16)
    row_spec = pl.BlockSpec((1, 1, l, HD_B), lambda i: (slot, i, 0, 0))
    dup_spec = pl.BlockSpec((1, s_pad, LANES), lambda i: (i, 0, 0))
    k_b, v_b, ki_b, k2, v2, ki2 = pl.pallas_call(
        functools.partial(_dsa_prep_kernel, l=l, p=p, s_pad=s_pad),
        out_shape=(row_out, row_out, row_out, dup_out, dup_out, dup_out),
        grid=(b,),
        in_specs=kv_specs,
        out_specs=(row_spec, row_spec, row_spec, dup_spec, dup_spec, dup_spec),
        input_output_aliases=aliases,
        compiler_params=_cparams(("parallel",)),
        name="dsa_prep",
    )(*args)

    n_tiles = l // tq
    tiles_per_class = 1 if p == 0 else n_tiles
    qg2 = _dup(q_gain.reshape(1, HD_B))
    o_b = jnp.zeros((b, l, H_B * HD_B), BF16)
    for tile0 in range(0, n_tiles, tiles_per_class):
        nt = min(tiles_per_class, n_tiles - tile0)
        s_keys = min(s_pad, -(-(p + (tile0 + nt) * tq) // LANES) * LANES)
        o_b = pl.pallas_call(
            functools.partial(_dsa_kernel, tq=tq, tile0=tile0, p=p, s_keys=s_keys, k_sel=k_sel),
            out_shape=jax.ShapeDtypeStruct(o_b.shape, o_b.dtype),
            grid=(b, nt),
            in_specs=[pl.BlockSpec((1, tq, 512), lambda i, t, t0=tile0: (i, t0 + t, COL_QB // 512)),
                      pl.BlockSpec((1, tq, 256), lambda i, t, t0=tile0: (i, t0 + t, COL_QI // 256)),
                      pl.BlockSpec((1, tq, LANES), lambda i, t, t0=tile0: (i, t0 + t, COL_SM // LANES)),
                      pl.BlockSpec((1, LANES), lambda i, t: (0, 0)),
                      pl.BlockSpec((1, s_keys, LANES), lambda i, t: (i, 0, 0)),
                      pl.BlockSpec((1, s_keys, LANES), lambda i, t: (i, 0, 0)),
                      pl.BlockSpec((1, s_keys, LANES), lambda i, t: (i, 0, 0)),
                      pl.BlockSpec(memory_space=pl.ANY)],
            out_specs=pl.BlockSpec((1, tq, H_B * HD_B), lambda i, t, t0=tile0: (i, t0 + t, 0)),
            scratch_shapes=[pltpu.VMEM((s_keys, tq), jnp.int32), pltpu.VMEM((s_keys, tq), F32),
                            pltpu.VMEM((s_keys, tq), BF16)],
            input_output_aliases={7: 0},
            compiler_params=_cparams(("parallel", "arbitrary")),
            name="dsa",
        )(proj, proj, proj, qg2, k2, v2, ki2, o_b)
    return o_b, (k_b, v_b, ki_b)


def _band_kernel(*refs, tb, cg, has_cache, first_kept):
    n_in = 8 if has_cache else 6
    q_ref, k_ref, v_ref, bias_ref, qg_ref, kg_ref = refs[:6]
    if has_cache:
        ck_ref, cv_ref = refs[6:8]
    o_ref, kn_ref, vn_ref, kwin, vwin = refs[n_in + 2:]
    j = pl.program_id(1)
    tbq = tb * CHUNK
    gq = cg * CHUNK
    wk = WINDOW + gq

    @pl.when(j == 0)
    def _():
        if has_cache:
            kwin[0:WINDOW, :] = ck_ref[0].astype(BF16)
            vwin[0:WINDOW, :] = cv_ref[0].astype(BF16)
        else:
            kwin[0:WINDOW, :] = jnp.zeros((WINDOW, H_C * HD_C), BF16)
            vwin[0:WINDOW, :] = jnp.zeros((WINDOW, H_C * HD_C), BF16)

    @pl.when(j > 0)
    def _():
        for i in range(WINDOW // tbq):
            kwin[i * tbq:(i + 1) * tbq, :] = kwin[(i + 1) * tbq:(i + 2) * tbq, :]
            vwin[i * tbq:(i + 1) * tbq, :] = vwin[(i + 1) * tbq:(i + 2) * tbq, :]

    vwin[WINDOW:WINDOW + tbq, :] = v_ref[0].astype(BF16)
    kcol = lax.broadcasted_iota(jnp.int32, (2 * gq, wk), 1)
    n_before = jnp.maximum(WINDOW - j * tbq, 0)
    for i in range(H_C // 2):
        cols = slice(i * LANES, (i + 1) * LANES)
        kwin[WINDOW:WINDOW + tbq, cols] = _pair_rms(k_ref[0, :, cols], kg_ref[...]).astype(BF16)
        qn = _pair_rms(q_ref[0, :, cols], qg_ref[...]) * (HD_C ** -0.5 * LOG2E)
        for g in range(tb // cg):
            r0 = g * gq
            s = _dot_nt(_split_halves(qn[r0:r0 + gq]), kwin[r0:r0 + wk, cols]) + bias_ref[i]
            if not has_cache:
                s = jnp.where(kcol >= n_before - r0, s, -jnp.inf)
            m = jnp.max(s, axis=-1, keepdims=True)
            e = jnp.exp2(s - m)
            o = _dot(e, vwin[r0:r0 + wk, cols]) / jnp.sum(e, axis=-1, keepdims=True)
            o_ref[0, r0:r0 + gq, cols] = _join_halves(o, gq).astype(o_ref.dtype)

    @pl.when(j >= first_kept)
    def _():
        rows = pl.ds(pl.multiple_of((j - first_kept) * tbq, tbq), tbq)
        vn_ref[0, 0, rows, :] = v_ref[0]
        for i in range(H_C // 2):
            cols = slice(i * LANES, (i + 1) * LANES)
            kn_ref[0, 0, rows, cols] = _pair_rms(k_ref[0, :, cols], kg_ref[...])


def _band(proj, rel_bias, q_gain, k_gain, cache, tb, cg, keep, slot, n_slots, prev):
    b, l, _ = proj.shape
    d = H_C * HD_C
    tbq = tb * CHUNK
    gq = cg * CHUNK
    wk = WINDOW + gq
    has_cache = cache is not None
    assert keep % tbq == 0 and l % tbq == 0
    first_kept = (l - keep) // tbq
    in_specs = [pl.BlockSpec((1, tbq, d), lambda i, j: (i, j, 0)),
                pl.BlockSpec((1, tbq, d), lambda i, j: (i, j, 1)),
                pl.BlockSpec((1, tbq, d), lambda i, j: (i, j, 2)),
                pl.BlockSpec((H_C // 2, 2 * gq, wk), lambda i, j: (0, 0, 0)),
                pl.BlockSpec((1, LANES), lambda i, j: (0, 0)),
                pl.BlockSpec((1, LANES), lambda i, j: (0, 0))]
    args = [proj, proj, proj, _band_bias(rel_bias, cg), _dup(q_gain.reshape(1, HD_C)), _dup(k_gain.reshape(1, HD_C))]
    if has_cache:
        in_specs += [pl.BlockSpec((1, WINDOW, d), lambda i, j: (i, 0, 0))] * 2
        args += [cache[0].reshape(b, WINDOW, d), cache[1].reshape(b, WINDOW, d)]
    cache_out = jax.ShapeDtypeStruct((n_slots, b, keep, d), F32)
    if prev is None:
        prev = tuple(jnp.zeros(cache_out.shape, F32) for _ in range(2))
    aliases = {len(args): 1, len(args) + 1: 2}
    in_specs += [pl.BlockSpec(memory_space=pl.ANY)] * 2
    args += list(prev)
    cache_spec = pl.BlockSpec((1, 1, keep, d), lambda i, j: (slot, i, 0, 0))
    o_c, kn, vn = pl.pallas_call(
        functools.partial(_band_kernel, tb=tb, cg=cg, has_cache=has_cache, first_kept=first_kept),
        out_shape=(jax.ShapeDtypeStruct((b, l, d), BF16), cache_out, cache_out),
        grid=(b, l // tbq),
        in_specs=in_specs,
        out_specs=(pl.BlockSpec((1, tbq, d), lambda i, j: (i, j, 0)), cache_spec, cache_spec),
        scratch_shapes=[pltpu.VMEM((WINDOW + tbq, d), BF16), pltpu.VMEM((WINDOW + tbq, d), BF16)],
        input_output_aliases=aliases,
        compiler_params=_cparams(("parallel", "arbitrary")),
        name="band",
    )(*args)
    return o_c, (kn, vn)


def _band_bias(rel_bias, cg):
    gq = cg * CHUNK
    wk = WINDOW + gq
    n = wk + gq - 1
    dist = WINDOW + gq - 1 - jnp.arange(n)
    seq = rel_bias[:, jnp.clip(dist, -(CHUNK - 1), REL_CLIP) + (CHUNK - 1)].astype(F32) * LOG2E
    period = jnp.roll(jnp.pad(seq, ((0, 0), (0, 1))), -(gq - 1), axis=1)
    table = jnp.tile(period, (1, gq))[:, :gq * n].reshape(H_C, gq, n)[:, :, :wk]
    qc = jnp.arange(gq) // CHUNK
    kc = jnp.arange(wk) // CHUNK
    inband = jnp.logical_and(kc[None, :] >= qc[:, None], kc[None, :] <= qc[:, None] + BAND_CHUNKS)
    return jnp.where(inband[None], table, -jnp.inf).reshape(H_C // 2, 2 * gq, wk)


def _ffn_kernel(*refs, n_mix, nf, nm, nb, tm, tf, final):
    x_ref = refs[0]
    mix_refs = refs[1:1 + n_mix]
    wo_refs = refs[1 + n_mix:1 + 2 * n_mix]
    rest = refs[1 + 2 * n_mix:]
    if final:
        g_ref, wa_ref, wg_ref, cw_ref, cb_ref, wd_ref, st0_ref, gf_ref = rest[:8]
        rest = rest[8:]
    else:
        g_ref, wa_ref, wg_ref, cw_ref, cb_ref, wd_ref, st0_ref = rest[:7]
        gf_ref = None
        rest = rest[7:]
    out_ref, stout_ref, abuf, cbuf, ubuf = rest
    mi = pl.program_id(1)
    rows = nb * tm
    bsel = (slice(None),) if nb > 1 else (0,)

    def flat(ref):
        return ref[...].reshape(rows, ref.shape[-1]) if nb > 1 else ref[0]

    def per_stream(z):
        return z.reshape(nb, tm, z.shape[-1]) if nb > 1 else z

    xr = flat(x_ref)
    for m_ref, w_ref in zip(mix_refs, wo_refs):
        xr = xr + jnp.dot(flat(m_ref), w_ref[...], preferred_element_type=F32)
    hb = _rms(xr, g_ref[...]).astype(BF16)

    @pl.when(mi == 0)
    def _():
        cbuf[...] = st0_ref[...]

    for f in range(nf):
        cols = slice(f * tf, (f + 1) * tf)
        slot = f % 2
        a = per_stream(jnp.dot(hb, wa_ref[:, cols], preferred_element_type=F32))
        gate = per_stream(jnp.dot(hb, wg_ref[:, cols], preferred_element_type=F32))
        abuf[(slot, *bsel, slice(0, 8))] = cbuf[(*bsel, slice(None), cols)]
        abuf[(slot, *bsel, slice(8, 8 + tm))] = a
        ac = (a * cw_ref[2:3, cols] + abuf[(slot, *bsel, slice(7, 7 + tm))] * cw_ref[1:2, cols]
              + abuf[(slot, *bsel, slice(6, 6 + tm))] * cw_ref[0:1, cols] + cb_ref[:, cols])
        cbuf[(*bsel, slice(None), cols)] = abuf[(slot, *bsel, slice(tm, tm + 8))]
        ubuf[:, cols] = (_silu(ac) * gate).reshape(rows, tf).astype(BF16)

    yv = xr + jnp.dot(ubuf[...], wd_ref[...], preferred_element_type=F32)
    if final:
        yv = _rms(yv, gf_ref[...])
    out_ref[...] = yv.reshape(nb, tm, yv.shape[-1])

    @pl.when(mi == nm - 1)
    def _():
        stout_ref[...] = cbuf[:, 8 - (CONV_F - 1):8, :]


def _ffn_vmem_bytes(rows, d, dff, tf, mix_cols):
    weights = 2 * (3 * d * dff + sum(mix_cols) * d)
    tiles = 2 * (2 * rows * d * 4 + rows * sum(mix_cols) * 2)
    scratch = 2 * (rows + 64) * tf * 4 + 64 * dff * 4
    temps = rows * d * (4 + 4 + 2) + 6 * rows * tf * 4
    return weights + tiles + scratch + temps


def _ffn(x, mixes, wos, gain, wa, wg, conv_w, conv_b, wd, st0, gain_final, nb, tm, tf):
    b, l, d = x.shape
    dff = wa.shape[1]
    nf = dff // tf
    nm = l // tm
    n_mix = len(mixes)
    final = gain_final is not None
    in_specs = [pl.BlockSpec((nb, tm, d), lambda i, m: (i, m, 0))]
    in_specs += [pl.BlockSpec((nb, tm, mx.shape[2]), lambda i, m: (i, m, 0)) for mx in mixes]
    in_specs += [pl.BlockSpec(w.shape, lambda i, m: (0, 0)) for w in wos]
    in_specs += [pl.BlockSpec((1, d), lambda i, m: (0, 0)),
                 pl.BlockSpec((d, dff), lambda i, m: (0, 0)),
                 pl.BlockSpec((d, dff), lambda i, m: (0, 0)),
                 pl.BlockSpec((CONV_F, dff), lambda i, m: (0, 0)),
                 pl.BlockSpec((1, dff), lambda i, m: (0, 0)),
                 pl.BlockSpec((dff, d), lambda i, m: (0, 0)),
                 pl.BlockSpec((nb, 8, dff), lambda i, m: (i, 0, 0))]
    args = [x, *mixes, *wos, gain.reshape(1, d), wa, wg, conv_w, conv_b.reshape(1, dff), wd, st0]
    if final:
        in_specs.append(pl.BlockSpec((1, d), lambda i, m: (0, 0)))
        args.append(gain_final.reshape(1, d))
    return pl.pallas_call(
        functools.partial(_ffn_kernel, n_mix=n_mix, nf=nf, nm=nm, nb=nb, tm=tm, tf=tf, final=final),
        out_shape=(jax.ShapeDtypeStruct((b, l, d), F32),
                   jax.ShapeDtypeStruct((b, CONV_F - 1, dff), F32)),
        grid=(b // nb, nm),
        in_specs=in_specs,
        out_specs=(pl.BlockSpec((nb, tm, d), lambda i, m: (i, m, 0)),
                   pl.BlockSpec((nb, CONV_F - 1, dff), lambda i, m: (i, 0, 0))),
        scratch_shapes=[pltpu.VMEM((2, nb, tm + 8, tf), F32), pltpu.VMEM((nb, 8, dff), F32),
                        pltpu.VMEM((nb * tm, dff), BF16)],
        compiler_params=_cparams(("parallel", "arbitrary"),
                                 min(V7X_VMEM_USABLE, _ffn_vmem_bytes(nb * tm, d, dff, tf, [mx.shape[2] for mx in mixes]))),
        name="ffn",
    )(*args)


def _pad_rows_front(a, rows):
    return jnp.pad(a, ((0, 0), (rows - a.shape[1], 0), (0, 0)))


def _trunk(x, past, w):
    b, l, d = x.shape
    depth = w["norm_mix"].shape[0]
    first = past is None
    n_chunks = l // CHUNK
    tm_proj = min(512, b * l)
    tm_ffn = min(512, l)
    nb_ffn = math.gcd(b, max(1, 512 // tm_ffn))
    tf = 256
    nc = min(8, n_chunks)
    tb = min(8, n_chunks)
    cg = min(2, tb)
    dff = w["ffn_w_a"].shape[2]
    n_even, n_odd = (depth + 1) // 2, depth // 2
    keep = min(WINDOW, l) if first else l
    dsa_new = band_new = None
    out = {k: [] for k in ("dn_S", "dn_conv", "ffn_conv")}
    for layer in range(depth):
        x2d = x.reshape(b * l, d)
        if layer % 2 == 0:
            e = layer // 2
            proj = _norm_proj(x2d, w["norm_mix"][layer], w["w_in_even"][e], tm_proj).reshape(b, l, EVEN_COLS)
            if first:
                conv0 = jnp.zeros((b, 8, C_A), F32)
                s0 = jnp.zeros((b, H_A, DK_A, DV_A), F32)
                cache = None
            else:
                conv0 = _pad_rows_front(past["state_dn_conv"][e], 8)
                s0 = past["state_dn_S"][e]
                cache = (past["cache_dsa_k"][e], past["cache_dsa_v"][e], past["cache_dsa_kidx"][e])
            o_a, buf_a, s_a = _deltanet(proj, conv0, s0, w["dn_conv_w"][e], w["dn_a_log"][e],
                                        w["dn_dt_bias"][e], w["dn_o_gain"][e], nc)
            o_b, dsa_new = _dsa(proj, w["dsa_q_gain"][e], w["dsa_k_gain"][e], cache, e, n_even, dsa_new)
            mixes = [o_a, o_b]
            wos = [w["w_out_even"][e][:H_A * DV_A], w["w_out_even"][e][H_A * DV_A:]]
            out["dn_S"].append(s_a)
            out["dn_conv"].append(buf_a)
        else:
            jj = layer // 2
            proj = _norm_proj(x2d, w["norm_mix"][layer], w["w_in_odd"][jj], tm_proj).reshape(b, l, 3 * H_C * HD_C)
            cache = None if first else (past["cache_band_k"][jj], past["cache_band_v"][jj])
            o_c, band_new = _band(proj, w["band_rel_bias"][jj], w["band_q_gain"][jj], w["band_k_gain"][jj], cache,
                                  tb, cg, keep, jj, n_odd, band_new)
            mixes = [o_c]
            wos = [w["w_out_odd"][jj]]
        st0 = (jnp.zeros((b, 8, dff), F32) if first else _pad_rows_front(past["state_ffn_conv"][layer], 8))
        gfin = w["norm_final"] if layer == depth - 1 else None
        x, fbuf = _ffn(x, mixes, wos, w["norm_ffn"][layer], w["ffn_w_a"][layer], w["ffn_w_g"][layer],
                       w["ffn_conv_w"][layer], w["ffn_conv_b"][layer], w["ffn_w_down"][layer], st0, gfin,
                       nb_ffn, tm_ffn, tf)
        out["ffn_conv"].append(fbuf)
    states = {k: jnp.stack(v) for k, v in out.items()}
    states["dsa_k"], states["dsa_v"], states["dsa_kidx"] = dsa_new
    states["band_k"], states["band_v"] = (a.reshape(n_odd, b, keep, H_C, HD_C) for a in band_new)
    return x, states


def _prep_weights(norm_mix, norm_ffn, norm_final, w_in_even, dn_conv_w, dn_a_log, dn_dt_bias, dn_o_gain,
                  dsa_q_gain, dsa_k_gain, w_out_even, w_in_odd, band_q_gain, band_k_gain, band_rel_bias,
                  w_out_odd, ffn_w_a, ffn_w_g, ffn_conv_w, ffn_conv_b, ffn_w_down):
    o_ba = 4 * H_A * DK_A
    o_qb = o_ba + 2 * H_A
    o_kb = o_qb + H_B * HD_B
    o_qi = o_kb + 2 * HD_B
    o_ki = o_qi + H_IDX * D_IDX
    o_wi = o_ki + D_IDX
    o_end = o_wi + H_IDX
    wie = w_in_even
    pad = jnp.zeros(wie.shape[:2] + (EVEN_COLS - o_end,), wie.dtype)
    w_even = jnp.concatenate([wie[..., :o_ba], wie[..., o_qb:o_kb], wie[..., o_qi:o_ki], wie[..., o_kb:o_qi],
                              wie[..., o_ki:o_wi], wie[..., o_ba:o_qb], wie[..., o_wi:o_end], pad], axis=-1)
    return dict(norm_mix=norm_mix, norm_ffn=norm_ffn, norm_final=norm_final,
                w_in_even=w_even.astype(BF16), dn_conv_w=dn_conv_w, dn_a_log=dn_a_log, dn_dt_bias=dn_dt_bias,
                dn_o_gain=dn_o_gain, dsa_q_gain=dsa_q_gain, dsa_k_gain=dsa_k_gain,
                w_out_even=w_out_even.astype(BF16), w_in_odd=w_in_odd.astype(BF16),
                band_q_gain=band_q_gain, band_k_gain=band_k_gain, band_rel_bias=band_rel_bias,
                w_out_odd=w_out_odd.astype(BF16),
                ffn_w_a=ffn_w_a.astype(BF16), ffn_w_g=ffn_w_g.astype(BF16), ffn_conv_w=ffn_conv_w,
                ffn_conv_b=ffn_conv_b, ffn_w_down=ffn_w_down.astype(BF16))


def kernel(x_prompt, x_sample, state_dn_S, state_dn_conv, cache_dsa_k, cache_dsa_v, cache_dsa_kidx, cache_band_k, cache_band_v, state_ffn_conv, norm_mix, norm_ffn, norm_final, w_in_even, dn_conv_w, dn_a_log, dn_dt_bias, dn_o_gain, dsa_q_gain, dsa_k_gain, w_out_even, w_in_odd, band_q_gain, band_k_gain, band_rel_bias, w_out_odd, ffn_w_a, ffn_w_g, ffn_conv_w, ffn_conv_b, ffn_w_down):
    assert cache_band_k.shape[2] == WINDOW, "band cache must hold exactly one window"
    w = _prep_weights(norm_mix, norm_ffn, norm_final, w_in_even, dn_conv_w, dn_a_log, dn_dt_bias, dn_o_gain,
                      dsa_q_gain, dsa_k_gain, w_out_even, w_in_odd, band_q_gain, band_k_gain, band_rel_bias,
                      w_out_odd, ffn_w_a, ffn_w_g, ffn_conv_w, ffn_conv_b, ffn_w_down)
    past = dict(state_dn_S=state_dn_S, state_dn_conv=state_dn_conv, cache_dsa_k=cache_dsa_k,
                cache_dsa_v=cache_dsa_v, cache_dsa_kidx=cache_dsa_kidx, cache_band_k=cache_band_k,
                cache_band_v=cache_band_v, state_ffn_conv=state_ffn_conv)
    y_p, sp = _trunk(x_prompt, None, w)
    y_s, ss = _trunk(x_sample, past, w)
    return (y_p, y_s,
            sp["dn_S"], ss["dn_S"], sp["dn_conv"], ss["dn_conv"],
            sp["dsa_k"], ss["dsa_k"], sp["dsa_v"], ss["dsa_v"], sp["dsa_kidx"], ss["dsa_kidx"],
            sp["band_k"], ss["band_k"], sp["band_v"], ss["band_v"],
            sp["ffn_conv"], ss["ffn_conv"])
```

```python
import functools
import math

import jax
import jax.numpy as jnp
from jax import lax
from jax.experimental import pallas as pl
from jax.experimental.pallas import tpu as pltpu

F32 = jnp.float32
BF16 = jnp.bfloat16
EPS = 1e-6
INT_MIN = -(2 ** 31)
MASKED = -1e30
LOG2E = math.log2(math.e)

CHUNK = 64
H_A, DK_A, DV_A, CONV_A = 4, 128, 128, 4
C_A = 3 * H_A * DK_A
H_B, HD_B, H_IDX, D_IDX, TOPK_MAX = 8, 64, 4, 64, 256
H_C, HD_C, BAND_CHUNKS, REL_CLIP = 16, 64, 8, 128
WINDOW = BAND_CHUNKS * CHUNK
CONV_F = 3

EVEN_COLS = 3072
COL_Z = 1536
COL_QB = 2048
COL_QI = 2560
COL_KV = 2816
COL_SM = 2944
SM_BETA, SM_A, SM_WI = 64, 68, 72

LANES = 128
VMEM_LIMIT = 48 * 1024 * 1024
V7X_VMEM_USABLE = 56 * 1024 * 1024


def _cparams(sem, vmem=VMEM_LIMIT):
    return pltpu.CompilerParams(dimension_semantics=sem, vmem_limit_bytes=vmem)


def _dot(a, b):
    return jnp.dot(a.astype(BF16), b.astype(BF16), preferred_element_type=F32)


def _dot_nt(a, b):
    return lax.dot_general(a.astype(BF16), b.astype(BF16), (((1,), (1,)), ((), ())),
                           preferred_element_type=F32)


def _dot_f32(a, b):
    return jnp.dot(a, b, preferred_element_type=F32, precision=lax.Precision.HIGHEST)


def _dot_nt_f32(a, b):
    return lax.dot_general(a, b, (((1,), (1,)), ((), ())), preferred_element_type=F32,
                           precision=lax.Precision.HIGHEST)


def _rms(x, gain):
    return x * lax.rsqrt(jnp.mean(x * x, axis=-1, keepdims=True) + EPS) * gain


def _silu(x):
    return x * jax.nn.sigmoid(x)


def _low_half(shape):
    return lax.broadcasted_iota(jnp.int32, shape, len(shape) - 1) < (LANES // 2)


def _pair_rms(x, gain2):
    lo = _low_half(x.shape)
    sq = x * x
    ms_lo = jnp.sum(jnp.where(lo, sq, 0.0), axis=-1, keepdims=True) * (2.0 / LANES)
    ms_hi = jnp.sum(jnp.where(lo, 0.0, sq), axis=-1, keepdims=True) * (2.0 / LANES)
    return x * jnp.where(lo, lax.rsqrt(ms_lo + EPS), lax.rsqrt(ms_hi + EPS)) * gain2


def _split_halves(x):
    lo = _low_half(x.shape)
    return jnp.concatenate([jnp.where(lo, x, 0.0), jnp.where(lo, 0.0, x)], axis=0)


def _join_halves(o, rows):
    return jnp.where(_low_half((rows, LANES)), o[0:rows], o[rows:2 * rows])


def _exp2_weights(s):
    return jnp.exp2(s - jnp.max(s, axis=-1, keepdims=True)).astype(BF16)


def _dup(x):
    return jnp.concatenate([x, x], axis=1)


def _norm_proj_kernel(x_ref, g_ref, w_ref, o_ref):
    o_ref[...] = jnp.dot(_rms(x_ref[...], g_ref[...]).astype(BF16), w_ref[...], preferred_element_type=F32)


def _norm_proj(x2d, gain, w_bf16, tm):
    m, d = x2d.shape
    n = w_bf16.shape[1]
    return pl.pallas_call(
        _norm_proj_kernel,
        out_shape=jax.ShapeDtypeStruct((m, n), F32),
        grid=(m // tm,),
        in_specs=[pl.BlockSpec((tm, d), lambda i: (i, 0)),
                  pl.BlockSpec((1, d), lambda i: (0, 0)),
                  pl.BlockSpec((d, n), lambda i: (0, 0))],
        out_specs=pl.BlockSpec((tm, n), lambda i: (i, 0)),
        compiler_params=_cparams(("parallel",)),
        name="norm_proj",
    )(x2d, gain.reshape(1, d), w_bf16)


def _tri_inv_lockstep(ms, eye_f):
    ts = [eye_f - m for m in ms]
    pbs = [(-m).astype(BF16) for m in ms]
    for _ in range(5):
        pbs = [jnp.dot(pb, pb, preferred_element_type=F32).astype(BF16) for pb in pbs]
        ts = [t + jnp.dot(t.astype(BF16), pb, preferred_element_type=F32) for t, pb in zip(ts, pbs)]
    return ts


def _deltanet_kernel(qkv_ref, z_ref, sm_ref, conv0_ref, s0_ref, cw_ref, alog_ref, dtb_ref, og_ref, _mix_in_ref,
                     o_ref, convout_ref, sout_ref, xbuf, s_scr, *, nc):
    c = pl.program_id(1)
    t = nc * CHUNK

    @pl.when(c == 0)
    def _():
        xbuf[0:8, :] = conv0_ref[0]
        for h in range(H_A):
            s_scr[:, h * DK_A:(h + 1) * DK_A] = jnp.transpose(s0_ref[0, h])

    x = qkv_ref[0]
    xbuf[8:8 + t, :] = x
    w = cw_ref[...]
    y = (x * w[3:4] + xbuf[7:7 + t, :] * w[2:3] + xbuf[6:6 + t, :] * w[1:2] + xbuf[5:5 + t, :] * w[0:1])
    convout_ref[0] = xbuf[t + 5:t + 8, :]
    xbuf[0:8, :] = xbuf[t:t + 8, :]
    y = _silu(y)

    sm = sm_ref[0]
    beta = jax.nn.sigmoid(sm)
    aa = sm + dtb_ref[...]
    softplus = jnp.maximum(aa, 0.0) + jnp.log1p(jnp.exp(-jnp.abs(aa)))
    g = -jnp.exp(alog_ref[...]) * softplus

    ht = H_A * CHUNK
    hk = H_A * DK_A
    ri = lax.broadcasted_iota(jnp.int32, (ht, ht), 0)
    ci = lax.broadcasted_iota(jnp.int32, (ht, ht), 1)
    same = jnp.right_shift(ri, 6) == jnp.right_shift(ci, 6)
    trilbd = jnp.logical_and(same, ri >= ci)
    strictbd = jnp.logical_and(same, ri > ci)
    eye_f = (ri == ci).astype(F32)
    r2 = lax.broadcasted_iota(jnp.int32, (2 * ht, hk), 0)
    c2 = lax.broadcasted_iota(jnp.int32, (2 * ht, hk), 1)
    bd2 = (jnp.right_shift(r2, 6) & (H_A - 1)) == jnp.right_shift(c2, 7)
    bd1 = bd2[0:ht]
    r64 = lax.broadcasted_iota(jnp.int32, (CHUNK, CHUNK), 0)
    c64 = lax.broadcasted_iota(jnp.int32, (CHUNK, CHUNK), 1)
    tril_f = (r64 >= c64).astype(F32)
    og = og_ref[...]

    def heads_on_rows(a, r0, off, width):
        return jnp.concatenate([a[r0:r0 + CHUNK, off + h * width:off + (h + 1) * width] for h in range(H_A)],
                               axis=0)

    pre = []
    for cidx in range(nc):
        r0 = cidx * CHUNK
        gcum = _dot_f32(tril_f, g[r0:r0 + CHUNK])
        q = heads_on_rows(y, r0, 0, DK_A)
        k = heads_on_rows(y, r0, hk, DK_A)
        v = heads_on_rows(y, r0, 2 * hk, DV_A)
        q = q * lax.rsqrt(jnp.sum(q * q, axis=-1, keepdims=True) + EPS) * (DK_A ** -0.5)
        k = k * lax.rsqrt(jnp.sum(k * k, axis=-1, keepdims=True) + EPS)
        bcol = heads_on_rows(beta, r0, SM_BETA, 1)
        gcol = heads_on_rows(gcum, 0, SM_A, 1)
        glast = [gcum[CHUNK - 1:CHUNK, SM_A + h:SM_A + h + 1] for h in range(H_A)]
        gl_col = jnp.concatenate([jnp.broadcast_to(x_, (CHUNK, 1)) for x_ in glast], axis=0)
        gl_row = jnp.concatenate([jnp.broadcast_to(x_, (1, DK_A)) for x_ in glast], axis=1)
        grow = jnp.transpose(jnp.broadcast_to(gcol, (ht, LANES)))[0:1, :]
        decay = jnp.exp(jnp.where(trilbd, gcol - grow, -jnp.inf))
        kb = k * bcol
        a_full = _dot_nt(jnp.concatenate([kb, q], axis=0), k)
        kd = k * jnp.exp(gl_col - gcol)
        pre.append(dict(
            m=jnp.where(strictbd, a_full[0:ht] * decay, 0.0),
            attn=(a_full[ht:2 * ht] * decay).astype(BF16),
            rhs=jnp.concatenate([v * bcol, kb * jnp.exp(gcol)], axis=1).astype(BF16),
            qg=q * jnp.exp(gcol),
            kd_bd=jnp.where(bd1, jnp.concatenate([kd] * H_A, axis=1), 0.0).astype(BF16),
            decay_row=jnp.exp(gl_row)))

    tmats = _tri_inv_lockstep([c["m"] for c in pre], eye_f)
    uws = [jnp.dot(tm.astype(BF16), c["rhs"], preferred_element_type=F32) for tm, c in zip(tmats, pre)]

    for cidx in range(nc):
        r0 = cidx * CHUNK
        c, uw = pre[cidx], uws[cidx]
        u = uw[:, 0:DV_A]
        wq = jnp.concatenate([uw[:, DV_A:DV_A + DK_A], c["qg"]], axis=0)
        wq_bd = jnp.where(bd2, jnp.concatenate([wq] * H_A, axis=1), 0.0)
        st_old = s_scr[...]
        ws_qs = _dot_nt(wq_bd, st_old)
        v_new = u - ws_qs[0:ht]
        s_scr[...] = st_old * c["decay_row"] + _dot(jnp.transpose(v_new), c["kd_bd"])
        o = ws_qs[ht:2 * ht] + _dot(c["attn"], v_new)
        zz = heads_on_rows(z_ref[0], r0, 0, DV_A)
        res = (_rms(o, og) * _silu(zz)).astype(o_ref.dtype)
        for h in range(H_A):
            o_ref[0, r0:r0 + CHUNK, h * DV_A:(h + 1) * DV_A] = res[h * CHUNK:(h + 1) * CHUNK]

    for h in range(H_A):
        sout_ref[0, h] = jnp.transpose(s_scr[:, h * DK_A:(h + 1) * DK_A])


def _deltanet(proj, conv0, s0, conv_w, a_log, dt_bias, o_gain, mix, nc):
    b, l, _ = proj.shape
    t = nc * CHUNK
    pad = jnp.zeros((1, LANES), F32)
    alog128 = lax.dynamic_update_slice(pad, a_log.reshape(1, H_A), (0, SM_A))
    dtb128 = lax.dynamic_update_slice(pad, dt_bias.reshape(1, H_A), (0, SM_A))
    return pl.pallas_call(
        functools.partial(_deltanet_kernel, nc=nc),
        out_shape=(jax.ShapeDtypeStruct(mix.shape, mix.dtype),
                   jax.ShapeDtypeStruct((b, CONV_A - 1, C_A), F32),
                   jax.ShapeDtypeStruct((b, H_A, DK_A, DV_A), F32)),
        grid=(b, l // t),
        in_specs=[pl.BlockSpec((1, t, C_A), lambda i, c: (i, c, 0)),
                  pl.BlockSpec((1, t, 512), lambda i, c: (i, c, COL_Z // 512)),
                  pl.BlockSpec((1, t, LANES), lambda i, c: (i, c, COL_SM // LANES)),
                  pl.BlockSpec((1, 8, C_A), lambda i, c: (i, 0, 0)),
                  pl.BlockSpec((1, H_A, DK_A, DV_A), lambda i, c: (i, 0, 0, 0)),
                  pl.BlockSpec((CONV_A, C_A), lambda i, c: (0, 0)),
                  pl.BlockSpec((1, LANES), lambda i, c: (0, 0)),
                  pl.BlockSpec((1, LANES), lambda i, c: (0, 0)),
                  pl.BlockSpec((1, DV_A), lambda i, c: (0, 0)),
                  pl.BlockSpec(memory_space=pl.ANY)],
        out_specs=(pl.BlockSpec((1, t, H_A * DV_A), lambda i, c: (i, c, 0)),
                   pl.BlockSpec((1, CONV_A - 1, C_A), lambda i, c: (i, 0, 0)),
                   pl.BlockSpec((1, H_A, DK_A, DV_A), lambda i, c: (i, 0, 0, 0))),
        scratch_shapes=[pltpu.VMEM((t + 8, C_A), F32), pltpu.VMEM((DV_A, H_A * DK_A), F32)],
        input_output_aliases={9: 0},
        compiler_params=_cparams(("parallel", "arbitrary")),
        name="deltanet",
    )(proj, proj, proj, conv0, s0, conv_w, alog128, dtb128, o_gain.reshape(1, DV_A), mix)


def _dsa_prep_kernel(*refs, l, p, s_pad):
    n_in = 6 if p else 3
    kv_ref, sm_ref, kg_ref = refs[:3]
    if p:
        ck_ref, cv_ref, cki_ref = refs[3:6]
    kout_ref, vout_ref, kiout_ref, k2_ref, v2_ref, ki2_ref = refs[n_in + 3:]
    kv = kv_ref[0]
    kn = _rms(kv[:, 0:HD_B], kg_ref[...])
    vraw = kv[:, HD_B:2 * HD_B]
    kidx = sm_ref[0][:, 0:D_IDX]
    kout_ref[0, 0] = kn
    vout_ref[0, 0] = vraw
    kiout_ref[0, 0] = kidx

    def with_ones(a):
        return jnp.concatenate([a, jnp.ones_like(a)], axis=1)

    if p:
        k2_ref[0, 0:p, :] = _dup(ck_ref[0]).astype(BF16)
        v2_ref[0, 0:p, :] = with_ones(cv_ref[0]).astype(BF16)
        ki2_ref[0, 0:p, :] = _dup(cki_ref[0]).astype(BF16)
    k2_ref[0, p:p + l, :] = _dup(kn).astype(BF16)
    v2_ref[0, p:p + l, :] = with_ones(vraw).astype(BF16)
    ki2_ref[0, p:p + l, :] = _dup(kidx).astype(BF16)
    if s_pad > p + l:
        zpad = jnp.zeros((s_pad - p - l, LANES), BF16)
        k2_ref[0, p + l:s_pad, :] = zpad
        v2_ref[0, p + l:s_pad, :] = zpad
        ki2_ref[0, p + l:s_pad, :] = zpad


def _colsum(x):
    s, n = x.shape
    return jnp.sum(jnp.sum(x.reshape(s // CHUNK, CHUNK, n), axis=0), axis=0, keepdims=True)


def _dsa_kernel(qb_ref, qi_ref, smq_ref, qg_ref, k2_ref, v2_ref, ki2_ref, _o_in_ref, o_ref, key_scr, eq_scr,
                selt_scr, *, tq, tile0, p, s_keys, k_sel):
    tg = tile0 + pl.program_id(1)

    smq = smq_ref[0]
    sel_r = lax.broadcasted_iota(jnp.int32, (8, LANES), 0)
    sel_c = lax.broadcasted_iota(jnp.int32, (8, LANES), 1)
    pick = (sel_c == sel_r + SM_WI).astype(F32)
    wi_t = _dot_nt_f32(pick, smq) * (H_IDX ** -0.5)
    qi = qi_ref[0]
    ki_all = ki2_ref[0]
    qheads = jnp.concatenate([_split_halves(qi[:, pr * LANES:(pr + 1) * LANES] * (D_IDX ** -0.5))
                              for pr in range(H_IDX // 2)], axis=0)
    score = jnp.zeros((s_keys, tq), F32)
    if tq % LANES == 0:
        d_all = _dot_nt(ki_all, qheads)
        for h in range(H_IDX):
            score = score + jnp.maximum(d_all[:, h * tq:(h + 1) * tq], 0.0) * wi_t[h:h + 1, :]
    else:
        for h in range(H_IDX):
            score = score + jnp.maximum(_dot_nt(ki_all, qheads[h * tq:(h + 1) * tq]), 0.0) * wi_t[h:h + 1, :]

    bits = pltpu.bitcast(score + 0.0, jnp.int32)
    key = bits ^ (jnp.right_shift(bits, 31) & 0x7FFFFFFF)
    kpos = lax.broadcasted_iota(jnp.int32, (s_keys, tq), 0)
    qpos = tg * tq + lax.broadcasted_iota(jnp.int32, (1, tq), 1)
    n_adm = p + (jnp.right_shift(qpos, 6) + 1) * CHUNK
    key_scr[...] = jnp.where(kpos < n_adm, key, INT_MIN)

    kf = float(k_sel)

    def count_ge(cand):
        return _colsum(jnp.where(key_scr[...] >= cand, 1.0, 0.0))

    def thr_body(i, thr):
        cand = thr | jnp.left_shift(jnp.int32(1), 30 - i)
        return jnp.where(count_ge(cand) >= kf, cand, thr)

    if s_keys < k_sel:
        thr = jnp.full((1, tq), INT_MIN, jnp.int32)
    else:
        thr0 = jnp.where(count_ge(jnp.zeros((1, tq), jnp.int32)) >= kf, 0, INT_MIN).astype(jnp.int32)
        thr = lax.fori_loop(0, 31, thr_body, thr0)

    ge = jnp.logical_and(key_scr[...] >= thr, kpos < n_adm)
    surplus = _colsum(jnp.where(ge, 1.0, 0.0)) - kf
    selt_scr[...] = jnp.where(ge, 0.0, MASKED).astype(BF16)

    @pl.when(jnp.max(surplus) > 0.0)
    def _():
        keys = key_scr[...]
        gt = keys > thr
        eq = jnp.logical_and(keys == thr, kpos < n_adm)
        need = kf - _colsum(jnp.where(gt, 1.0, 0.0))
        eq_scr[...] = jnp.where(eq, 1.0, 0.0)
        nbits = int(s_keys).bit_length()

        def tie_body(i, lim):
            cand = lim | jnp.left_shift(jnp.int32(1), nbits - 1 - i)
            cnt = _colsum(jnp.where(kpos < cand, eq_scr[...], 0.0))
            return jnp.where(cnt <= need, cand, lim)

        lim = lax.fori_loop(0, nbits, tie_body, jnp.zeros((1, tq), jnp.int32))
        keep = jnp.logical_or(gt, jnp.logical_and(eq, kpos < lim))
        selt_scr[...] = jnp.where(keep, 0.0, MASKED).astype(BF16)

    er = lax.broadcasted_iota(jnp.int32, (tq, tq), 0)
    ec = lax.broadcasted_iota(jnp.int32, (tq, tq), 1)
    mask_bias = _dot_nt((er == ec).astype(BF16), selt_scr[...])

    qb = qb_ref[0]
    k_all = k2_ref[0]
    v_all = v2_ref[0]
    lo = _low_half((tq, LANES))
    outs = []
    for pr in range(H_B // 2):
        qn = _pair_rms(qb[:, pr * LANES:(pr + 1) * LANES], qg_ref[...]) * (HD_B ** -0.5 * LOG2E)
        s = _dot_nt(_split_halves(qn), k_all)
        s = (s.reshape(2, tq, s_keys) + mask_bias[None]).reshape(2 * tq, s_keys)
        r = _dot(_exp2_weights(s), v_all)
        o = r / pltpu.roll(r, LANES // 2, 1)
        outs.append(jnp.where(lo, o[0:tq], pltpu.roll(o[tq:2 * tq], LANES // 2, 1)))
    o_ref[0] = jnp.concatenate(outs, axis=1).astype(o_ref.dtype)


def _dsa(proj, q_gain, k_gain, cache, mix, slot, n_slots, prev):
    b, l, _ = proj.shape
    p = 0 if cache is None else cache[0].shape[1]
    tq = min(LANES, l)
    s_k = p + l
    s_pad = -(-s_k // LANES) * LANES
    k_sel = min(TOPK_MAX, s_k // 4)

    kv_specs = [pl.BlockSpec((1, l, LANES), lambda i: (i, 0, COL_KV // LANES)),
                pl.BlockSpec((1, l, LANES), lambda i: (i, 0, COL_SM // LANES)),
                pl.BlockSpec((1, HD_B), lambda i: (0, 0))]
    args = [proj, proj, k_gain.reshape(1, HD_B)]
    if p:
        kv_specs += [pl.BlockSpec((1, p, HD_B), lambda i: (i, 0, 0))] * 3
        args += list(cache)
    row_out = jax.ShapeDtypeStruct((n_slots, b, l, HD_B), F32)
    if prev is None:
        prev = tuple(jnp.zeros(row_out.shape, F32) for _ in range(3))
    aliases = {len(args) + n: n for n in range(3)}
    kv_specs += [pl.BlockSpec(memory_space=pl.ANY)] * 3
    args += list(prev)
    dup_out = jax.ShapeDtypeStruct((b, s_pad, LANES), BF16)
    row_spec = pl.BlockSpec((1, 1, l, HD_B), lambda i: (slot, i, 0, 0))
    dup_spec = pl.BlockSpec((1, s_pad, LANES), lambda i: (i, 0, 0))
    k_b, v_b, ki_b, k2, v2, ki2 = pl.pallas_call(
        functools.partial(_dsa_prep_kernel, l=l, p=p, s_pad=s_pad),
        out_shape=(row_out, row_out, row_out, dup_out, dup_out, dup_out),
        grid=(b,),
        in_specs=kv_specs,
        out_specs=(row_spec, row_spec, row_spec, dup_spec, dup_spec, dup_spec),
        input_output_aliases=aliases,
        compiler_params=_cparams(("parallel",)),
        name="dsa_prep",
    )(*args)

    n_tiles = l // tq
    tiles_per_class = 1 if p == 0 else n_tiles
    qg2 = _dup(q_gain.reshape(1, HD_B))
    o_b = mix
    for tile0 in range(0, n_tiles, tiles_per_class):
        nt = min(tiles_per_class, n_tiles - tile0)
        s_keys = min(s_pad, -(-(p + (tile0 + nt) * tq) // LANES) * LANES)
        o_b = pl.pallas_call(
            functools.partial(_dsa_kernel, tq=tq, tile0=tile0, p=p, s_keys=s_keys, k_sel=k_sel),
            out_shape=jax.ShapeDtypeStruct(o_b.shape, o_b.dtype),
            grid=(b, nt),
            in_specs=[pl.BlockSpec((1, tq, 512), lambda i, t, t0=tile0: (i, t0 + t, COL_QB // 512)),
                      pl.BlockSpec((1, tq, 256), lambda i, t, t0=tile0: (i, t0 + t, COL_QI // 256)),
                      pl.BlockSpec((1, tq, LANES), lambda i, t, t0=tile0: (i, t0 + t, COL_SM // LANES)),
                      pl.BlockSpec((1, LANES), lambda i, t: (0, 0)),
                      pl.BlockSpec((1, s_keys, LANES), lambda i, t: (i, 0, 0)),
                      pl.BlockSpec((1, s_keys, LANES), lambda i, t: (i, 0, 0)),
                      pl.BlockSpec((1, s_keys, LANES), lambda i, t: (i, 0, 0)),
                      pl.BlockSpec(memory_space=pl.ANY)],
            out_specs=pl.BlockSpec((1, tq, H_B * HD_B), lambda i, t, t0=tile0: (i, t0 + t, 1)),
            scratch_shapes=[pltpu.VMEM((s_keys, tq), jnp.int32), pltpu.VMEM((s_keys, tq), F32),
                            pltpu.VMEM((s_keys, tq), BF16)],
            input_output_aliases={7: 0},
            compiler_params=_cparams(("parallel", "arbitrary")),
            name="dsa",
        )(proj, proj, proj, qg2, k2, v2, ki2, o_b)
    return o_b, (k_b, v_b, ki_b)


def _band_kernel(*refs, tb, cg, has_cache, first_kept):
    n_in = 8 if has_cache else 6
    q_ref, k_ref, v_ref, bias_ref, qg_ref, kg_ref = refs[:6]
    if has_cache:
        ck_ref, cv_ref = refs[6:8]
    o_ref, kn_ref, vn_ref, kwin, vwin = refs[n_in + 2:]
    j = pl.program_id(1)
    tbq = tb * CHUNK
    gq = cg * CHUNK
    wk = WINDOW + gq

    @pl.when(j == 0)
    def _():
        if has_cache:
            kwin[0:WINDOW, :] = ck_ref[0].astype(BF16)
            vwin[0:WINDOW, :] = cv_ref[0].astype(BF16)
        else:
            kwin[0:WINDOW, :] = jnp.zeros((WINDOW, H_C * HD_C), BF16)
            vwin[0:WINDOW, :] = jnp.zeros((WINDOW, H_C * HD_C), BF16)

    @pl.when(j > 0)
    def _():
        for i in range(WINDOW // tbq):
            kwin[i * tbq:(i + 1) * tbq, :] = kwin[(i + 1) * tbq:(i + 2) * tbq, :]
            vwin[i * tbq:(i + 1) * tbq, :] = vwin[(i + 1) * tbq:(i + 2) * tbq, :]

    vwin[WINDOW:WINDOW + tbq, :] = v_ref[0].astype(BF16)
    kcol = lax.broadcasted_iota(jnp.int32, (2 * gq, wk), 1)
    n_before = jnp.maximum(WINDOW - j * tbq, 0)
    for i in range(H_C // 2):
        cols = slice(i * LANES, (i + 1) * LANES)
        kwin[WINDOW:WINDOW + tbq, cols] = _pair_rms(k_ref[0, :, cols], kg_ref[...]).astype(BF16)
        qn = _pair_rms(q_ref[0, :, cols], qg_ref[...]) * (HD_C ** -0.5 * LOG2E)
        for g in range(tb // cg):
            r0 = g * gq
            s = _dot_nt(_split_halves(qn[r0:r0 + gq]), kwin[r0:r0 + wk, cols]) + bias_ref[i]
            if not has_cache:
                s = jnp.where(kcol >= n_before - r0, s, -jnp.inf)
            m = jnp.max(s, axis=-1, keepdims=True)
            e = jnp.exp2(s - m)
            o = _dot(e, vwin[r0:r0 + wk, cols]) / jnp.sum(e, axis=-1, keepdims=True)
            o_ref[0, r0:r0 + gq, cols] = _join_halves(o, gq).astype(o_ref.dtype)

    @pl.when(j >= first_kept)
    def _():
        rows = pl.ds(pl.multiple_of((j - first_kept) * tbq, tbq), tbq)
        vn_ref[0, 0, rows, :] = v_ref[0]
        for i in range(H_C // 2):
            cols = slice(i * LANES, (i + 1) * LANES)
            kn_ref[0, 0, rows, cols] = _pair_rms(k_ref[0, :, cols], kg_ref[...])


def _band(proj, rel_bias, q_gain, k_gain, cache, tb, cg, keep, slot, n_slots, prev):
    b, l, _ = proj.shape
    d = H_C * HD_C
    tbq = tb * CHUNK
    gq = cg * CHUNK
    wk = WINDOW + gq
    has_cache = cache is not None
    assert keep % tbq == 0 and l % tbq == 0
    first_kept = (l - keep) // tbq
    in_specs = [pl.BlockSpec((1, tbq, d), lambda i, j: (i, j, 0)),
                pl.BlockSpec((1, tbq, d), lambda i, j: (i, j, 1)),
                pl.BlockSpec((1, tbq, d), lambda i, j: (i, j, 2)),
                pl.BlockSpec((H_C // 2, 2 * gq, wk), lambda i, j: (0, 0, 0)),
                pl.BlockSpec((1, LANES), lambda i, j: (0, 0)),
                pl.BlockSpec((1, LANES), lambda i, j: (0, 0))]
    args = [proj, proj, proj, _band_bias(rel_bias, cg), _dup(q_gain.reshape(1, HD_C)), _dup(k_gain.reshape(1, HD_C))]
    if has_cache:
        in_specs += [pl.BlockSpec((1, WINDOW, d), lambda i, j: (i, 0, 0))] * 2
        args += [cache[0].reshape(b, WINDOW, d), cache[1].reshape(b, WINDOW, d)]
    cache_out = jax.ShapeDtypeStruct((n_slots, b, keep, d), F32)
    if prev is None:
        prev = tuple(jnp.zeros(cache_out.shape, F32) for _ in range(2))
    aliases = {len(args): 1, len(args) + 1: 2}
    in_specs += [pl.BlockSpec(memory_space=pl.ANY)] * 2
    args += list(prev)
    cache_spec = pl.BlockSpec((1, 1, keep, d), lambda i, j: (slot, i, 0, 0))
    o_c, kn, vn = pl.pallas_call(
        functools.partial(_band_kernel, tb=tb, cg=cg, has_cache=has_cache, first_kept=first_kept),
        out_shape=(jax.ShapeDtypeStruct((b, l, d), BF16), cache_out, cache_out),
        grid=(b, l // tbq),
        in_specs=in_specs,
        out_specs=(pl.BlockSpec((1, tbq, d), lambda i, j: (i, j, 0)), cache_spec, cache_spec),
        scratch_shapes=[pltpu.VMEM((WINDOW + tbq, d), BF16), pltpu.VMEM((WINDOW + tbq, d), BF16)],
        input_output_aliases=aliases,
        compiler_params=_cparams(("parallel", "arbitrary")),
        name="band",
    )(*args)
    return o_c, (kn, vn)


def _band_bias(rel_bias, cg):
    gq = cg * CHUNK
    wk = WINDOW + gq
    n = wk + gq - 1
    dist = WINDOW + gq - 1 - jnp.arange(n)
    seq = rel_bias[:, jnp.clip(dist, -(CHUNK - 1), REL_CLIP) + (CHUNK - 1)].astype(F32) * LOG2E
    period = jnp.roll(jnp.pad(seq, ((0, 0), (0, 1))), -(gq - 1), axis=1)
    table = jnp.tile(period, (1, gq))[:, :gq * n].reshape(H_C, gq, n)[:, :, :wk]
    qc = jnp.arange(gq) // CHUNK
    kc = jnp.arange(wk) // CHUNK
    inband = jnp.logical_and(kc[None, :] >= qc[:, None], kc[None, :] <= qc[:, None] + BAND_CHUNKS)
    return jnp.where(inband[None], table, -jnp.inf).reshape(H_C // 2, 2 * gq, wk)


def _ffn_kernel(*refs, n_mix, nf, nm, nb, tm, tf, final):
    x_ref = refs[0]
    mix_refs = refs[1:1 + n_mix]
    wo_refs = refs[1 + n_mix:1 + 2 * n_mix]
    rest = refs[1 + 2 * n_mix:]
    if final:
        g_ref, wa_ref, wg_ref, cw_ref, cb_ref, wd_ref, st0_ref, gf_ref = rest[:8]
        rest = rest[8:]
    else:
        g_ref, wa_ref, wg_ref, cw_ref, cb_ref, wd_ref, st0_ref = rest[:7]
        gf_ref = None
        rest = rest[7:]
    out_ref, stout_ref, abuf, cbuf, ubuf = rest
    mi = pl.program_id(1)
    rows = nb * tm
    bsel = (slice(None),) if nb > 1 else (0,)

    def flat(ref):
        return ref[...].reshape(rows, ref.shape[-1]) if nb > 1 else ref[0]

    def per_stream(z):
        return z.reshape(nb, tm, z.shape[-1]) if nb > 1 else z

    xr = flat(x_ref)
    for m_ref, w_ref in zip(mix_refs, wo_refs):
        xr = xr + jnp.dot(flat(m_ref), w_ref[...], preferred_element_type=F32)
    hb = _rms(xr, g_ref[...]).astype(BF16)

    @pl.when(mi == 0)
    def _():
        cbuf[...] = st0_ref[...]

    for f in range(nf):
        cols = slice(f * tf, (f + 1) * tf)
        slot = f % 2
        a = per_stream(jnp.dot(hb, wa_ref[:, cols], preferred_element_type=F32))
        gate = per_stream(jnp.dot(hb, wg_ref[:, cols], preferred_element_type=F32))
        abuf[(slot, *bsel, slice(0, 8))] = cbuf[(*bsel, slice(None), cols)]
        abuf[(slot, *bsel, slice(8, 8 + tm))] = a
        ac = (a * cw_ref[2:3, cols] + abuf[(slot, *bsel, slice(7, 7 + tm))] * cw_ref[1:2, cols]
              + abuf[(slot, *bsel, slice(6, 6 + tm))] * cw_ref[0:1, cols] + cb_ref[:, cols])
        cbuf[(*bsel, slice(None), cols)] = abuf[(slot, *bsel, slice(tm, tm + 8))]
        ubuf[:, cols] = (_silu(ac) * gate).reshape(rows, tf).astype(BF16)

    yv = xr + jnp.dot(ubuf[...], wd_ref[...], preferred_element_type=F32)
    if final:
        yv = _rms(yv, gf_ref[...])
    out_ref[...] = yv.reshape(nb, tm, yv.shape[-1])

    @pl.when(mi == nm - 1)
    def _():
        stout_ref[...] = cbuf[:, 8 - (CONV_F - 1):8, :]


def _ffn_vmem_bytes(rows, d, dff, tf, mix_cols):
    weights = 2 * (3 * d * dff + sum(mix_cols) * d)
    tiles = 2 * (2 * rows * d * 4 + rows * sum(mix_cols) * 2)
    scratch = 2 * (rows + 64) * tf * 4 + 64 * dff * 4
    temps = rows * d * (4 + 4 + 2) + 6 * rows * tf * 4
    return weights + tiles + scratch + temps


def _ffn(x, mixes, wos, gain, wa, wg, conv_w, conv_b, wd, st0, gain_final, nb, tm, tf):
    b, l, d = x.shape
    dff = wa.shape[1]
    nf = dff // tf
    nm = l // tm
    n_mix = len(mixes)
    final = gain_final is not None
    in_specs = [pl.BlockSpec((nb, tm, d), lambda i, m: (i, m, 0))]
    in_specs += [pl.BlockSpec((nb, tm, mx.shape[2]), lambda i, m: (i, m, 0)) for mx in mixes]
    in_specs += [pl.BlockSpec(w.shape, lambda i, m: (0, 0)) for w in wos]
    in_specs += [pl.BlockSpec((1, d), lambda i, m: (0, 0)),
                 pl.BlockSpec((d, dff), lambda i, m: (0, 0)),
                 pl.BlockSpec((d, dff), lambda i, m: (0, 0)),
                 pl.BlockSpec((CONV_F, dff), lambda i, m: (0, 0)),
                 pl.BlockSpec((1, dff), lambda i, m: (0, 0)),
                 pl.BlockSpec((dff, d), lambda i, m: (0, 0)),
                 pl.BlockSpec((nb, 8, dff), lambda i, m: (i, 0, 0))]
    args = [x, *mixes, *wos, gain.reshape(1, d), wa, wg, conv_w, conv_b.reshape(1, dff), wd, st0]
    if final:
        in_specs.append(pl.BlockSpec((1, d), lambda i, m: (0, 0)))
        args.append(gain_final.reshape(1, d))
    return pl.pallas_call(
        functools.partial(_ffn_kernel, n_mix=n_mix, nf=nf, nm=nm, nb=nb, tm=tm, tf=tf, final=final),
        out_shape=(jax.ShapeDtypeStruct((b, l, d), F32),
                   jax.ShapeDtypeStruct((b, CONV_F - 1, dff), F32)),
        grid=(b // nb, nm),
        in_specs=in_specs,
        out_specs=(pl.BlockSpec((nb, tm, d), lambda i, m: (i, m, 0)),
                   pl.BlockSpec((nb, CONV_F - 1, dff), lambda i, m: (i, 0, 0))),
        scratch_shapes=[pltpu.VMEM((2, nb, tm + 8, tf), F32), pltpu.VMEM((nb, 8, dff), F32),
                        pltpu.VMEM((nb * tm, dff), BF16)],
        compiler_params=_cparams(("parallel", "arbitrary"),
                                 min(V7X_VMEM_USABLE, _ffn_vmem_bytes(nb * tm, d, dff, tf, [mx.shape[2] for mx in mixes]))),
        name="ffn",
    )(*args)


def _pad_rows_front(a, rows):
    return jnp.pad(a, ((0, 0), (rows - a.shape[1], 0), (0, 0)))


def _trunk(x, past, w):
    b, l, d = x.shape
    depth = w["norm_mix"].shape[0]
    first = past is None
    n_chunks = l // CHUNK
    tm_proj = min(512, b * l)
    tm_ffn = min(512, l)
    nb_ffn = math.gcd(b, max(1, 512 // tm_ffn))
    tf = 256
    nc = min(8, n_chunks)
    tb = min(8, n_chunks)
    cg = min(2, tb)
    dff = w["ffn_w_a"].shape[2]
    n_even, n_odd = (depth + 1) // 2, depth // 2
    keep = min(WINDOW, l) if first else l
    dsa_new = band_new = None
    out = {k: [] for k in ("dn_S", "dn_conv", "ffn_conv")}
    for layer in range(depth):
        x2d = x.reshape(b * l, d)
        if layer % 2 == 0:
            e = layer // 2
            proj = _norm_proj(x2d, w["norm_mix"][layer], w["w_in_even"][e], tm_proj).reshape(b, l, EVEN_COLS)
            if first:
                conv0 = jnp.zeros((b, 8, C_A), F32)
                s0 = jnp.zeros((b, H_A, DK_A, DV_A), F32)
                cache = None
            else:
                conv0 = _pad_rows_front(past["state_dn_conv"][e], 8)
                s0 = past["state_dn_S"][e]
                cache = (past["cache_dsa_k"][e], past["cache_dsa_v"][e], past["cache_dsa_kidx"][e])
            mix = jnp.zeros((b, l, H_A * DV_A + H_B * HD_B), BF16)
            mix, buf_a, s_a = _deltanet(proj, conv0, s0, w["dn_conv_w"][e], w["dn_a_log"][e],
                                        w["dn_dt_bias"][e], w["dn_o_gain"][e], mix, nc)
            mix, dsa_new = _dsa(proj, w["dsa_q_gain"][e], w["dsa_k_gain"][e], cache, mix, e, n_even, dsa_new)
            mixes = [mix]
            wos = [w["w_out_even"][e]]
            out["dn_S"].append(s_a)
            out["dn_conv"].append(buf_a)
        else:
            jj = layer // 2
            proj = _norm_proj(x2d, w["norm_mix"][layer], w["w_in_odd"][jj], tm_proj).reshape(b, l, 3 * H_C * HD_C)
            cache = None if first else (past["cache_band_k"][jj], past["cache_band_v"][jj])
            o_c, band_new = _band(proj, w["band_rel_bias"][jj], w["band_q_gain"][jj], w["band_k_gain"][jj], cache,
                                  tb, cg, keep, jj, n_odd, band_new)
            mixes = [o_c]
            wos = [w["w_out_odd"][jj]]
        st0 = (jnp.zeros((b, 8, dff), F32) if first else _pad_rows_front(past["state_ffn_conv"][layer], 8))
        gfin = w["norm_final"] if layer == depth - 1 else None
        x, fbuf = _ffn(x, mixes, wos, w["norm_ffn"][layer], w["ffn_w_a"][layer], w["ffn_w_g"][layer],
                       w["ffn_conv_w"][layer], w["ffn_conv_b"][layer], w["ffn_w_down"][layer], st0, gfin,
                       nb_ffn, tm_ffn, tf)
        out["ffn_conv"].append(fbuf)
    states = {k: jnp.stack(v) for k, v in out.items()}
    states["dsa_k"], states["dsa_v"], states["dsa_kidx"] = dsa_new
    states["band_k"], states["band_v"] = (a.reshape(n_odd, b, keep, H_C, HD_C) for a in band_new)
    return x, states


def _prep_weights(norm_mix, norm_ffn, norm_final, w_in_even, dn_conv_w, dn_a_log, dn_dt_bias, dn_o_gain,
                  dsa_q_gain, dsa_k_gain, w_out_even, w_in_odd, band_q_gain, band_k_gain, band_rel_bias,
                  w_out_odd, ffn_w_a, ffn_w_g, ffn_conv_w, ffn_conv_b, ffn_w_down):
    o_ba = 4 * H_A * DK_A
    o_qb = o_ba + 2 * H_A
    o_kb = o_qb + H_B * HD_B
    o_qi = o_kb + 2 * HD_B
    o_ki = o_qi + H_IDX * D_IDX
    o_wi = o_ki + D_IDX
    o_end = o_wi + H_IDX
    wie = w_in_even
    pad = jnp.zeros(wie.shape[:2] + (EVEN_COLS - o_end,), wie.dtype)
    w_even = jnp.concatenate([wie[..., :o_ba], wie[..., o_qb:o_kb], wie[..., o_qi:o_ki], wie[..., o_kb:o_qi],
                              wie[..., o_ki:o_wi], wie[..., o_ba:o_qb], wie[..., o_wi:o_end], pad], axis=-1)
    return dict(norm_mix=norm_mix, norm_ffn=norm_ffn, norm_final=norm_final,
                w_in_even=w_even.astype(BF16), dn_conv_w=dn_conv_w, dn_a_log=dn_a_log, dn_dt_bias=dn_dt_bias,
                dn_o_gain=dn_o_gain, dsa_q_gain=dsa_q_gain, dsa_k_gain=dsa_k_gain,
                w_out_even=w_out_even.astype(BF16), w_in_odd=w_in_odd.astype(BF16),
                band_q_gain=band_q_gain, band_k_gain=band_k_gain, band_rel_bias=band_rel_bias,
                w_out_odd=w_out_odd.astype(BF16),
                ffn_w_a=ffn_w_a.astype(BF16), ffn_w_g=ffn_w_g.astype(BF16), ffn_conv_w=ffn_conv_w,
                ffn_conv_b=ffn_conv_b, ffn_w_down=ffn_w_down.astype(BF16))


def kernel(x_prompt, x_sample, state_dn_S, state_dn_conv, cache_dsa_k, cache_dsa_v, cache_dsa_kidx, cache_band_k, cache_band_v, state_ffn_conv, norm_mix, norm_ffn, norm_final, w_in_even, dn_conv_w, dn_a_log, dn_dt_bias, dn_o_gain, dsa_q_gain, dsa_k_gain, w_out_even, w_in_odd, band_q_gain, band_k_gain, band_rel_bias, w_out_odd, ffn_w_a, ffn_w_g, ffn_conv_w, ffn_conv_b, ffn_w_down):
    assert cache_band_k.shape[2] == WINDOW, "band cache must hold exactly one window"
    w = _prep_weights(norm_mix, norm_ffn, norm_final, w_in_even, dn_conv_w, dn_a_log, dn_dt_bias, dn_o_gain,
                      dsa_q_gain, dsa_k_gain, w_out_even, w_in_odd, band_q_gain, band_k_gain, band_rel_bias,
                      w_out_odd, ffn_w_a, ffn_w_g, ffn_conv_w, ffn_conv_b, ffn_w_down)
    past = dict(state_dn_S=state_dn_S, state_dn_conv=state_dn_conv, cache_dsa_k=cache_dsa_k,
                cache_dsa_v=cache_dsa_v, cache_dsa_kidx=cache_dsa_kidx, cache_band_k=cache_band_k,
                cache_band_v=cache_band_v, state_ffn_conv=state_ffn_conv)
    y_p, sp = _trunk(x_prompt, None, w)
    y_s, ss = _trunk(x_sample, past, w)
    return (y_p, y_s,
            sp["dn_S"], ss["dn_S"], sp["dn_conv"], ss["dn_conv"],
            sp["dsa_k"], ss["dsa_k"], sp["dsa_v"], ss["dsa_v"], sp["dsa_kidx"], ss["dsa_kidx"],
            sp["band_k"], ss["band_k"], sp["band_v"], ss["band_v"],
            sp["ffn_conv"], ss["ffn_conv"])
```

```python
import functools
import math

import jax
import jax.numpy as jnp
from jax import lax
from jax.experimental import pallas as pl
from jax.experimental.pallas import tpu as pltpu

F32 = jnp.float32
BF16 = jnp.bfloat16
EPS = 1e-6
INT_MIN = -(2 ** 31)
MASKED = -1e30
LOG2E = math.log2(math.e)

CHUNK = 64
H_A, DK_A, DV_A, CONV_A = 4, 128, 128, 4
C_A = 3 * H_A * DK_A
H_B, HD_B, H_IDX, D_IDX, TOPK_MAX = 8, 64, 4, 64, 256
H_C, HD_C, BAND_CHUNKS, REL_CLIP = 16, 64, 8, 128
WINDOW = BAND_CHUNKS * CHUNK
CONV_F = 3

EVEN_COLS = 3072
COL_Z = 1536
COL_QB = 2048
COL_QI = 2560
COL_KV = 2816
COL_SM = 2944
SM_BETA, SM_A, SM_WI = 64, 68, 72

LANES = 128
VMEM_LIMIT = 48 * 1024 * 1024
V7X_VMEM_USABLE = 56 * 1024 * 1024


def _cparams(sem, vmem=VMEM_LIMIT):
    return pltpu.CompilerParams(dimension_semantics=sem, vmem_limit_bytes=vmem)


def _dot(a, b):
    return jnp.dot(a.astype(BF16), b.astype(BF16), preferred_element_type=F32)


def _dot_nt(a, b):
    return lax.dot_general(a.astype(BF16), b.astype(BF16), (((1,), (1,)), ((), ())),
                           preferred_element_type=F32)


def _dot_f32(a, b):
    return jnp.dot(a, b, preferred_element_type=F32, precision=lax.Precision.HIGHEST)


def _dot_nt_f32(a, b):
    return lax.dot_general(a, b, (((1,), (1,)), ((), ())), preferred_element_type=F32,
                           precision=lax.Precision.HIGHEST)


def _rms(x, gain):
    return x * lax.rsqrt(jnp.mean(x * x, axis=-1, keepdims=True) + EPS) * gain


def _silu(x):
    return x * jax.nn.sigmoid(x)


def _low_half(shape):
    return lax.broadcasted_iota(jnp.int32, shape, len(shape) - 1) < (LANES // 2)


def _pair_rms(x, gain2):
    lo = _low_half(x.shape)
    sq = x * x
    ms_lo = jnp.sum(jnp.where(lo, sq, 0.0), axis=-1, keepdims=True) * (2.0 / LANES)
    ms_hi = jnp.sum(jnp.where(lo, 0.0, sq), axis=-1, keepdims=True) * (2.0 / LANES)
    return x * jnp.where(lo, lax.rsqrt(ms_lo + EPS), lax.rsqrt(ms_hi + EPS)) * gain2


def _split_halves(x):
    lo = _low_half(x.shape)
    return jnp.concatenate([jnp.where(lo, x, 0.0), jnp.where(lo, 0.0, x)], axis=0)


def _join_halves(o, rows):
    return jnp.where(_low_half((rows, LANES)), o[0:rows], o[rows:2 * rows])


def _exp2_weights(s):
    return jnp.exp2(s - jnp.max(s, axis=-1, keepdims=True)).astype(BF16)


def _dup(x):
    return jnp.concatenate([x, x], axis=1)


def _norm_proj_kernel(x_ref, g_ref, w_ref, o_ref):
    o_ref[...] = jnp.dot(_rms(x_ref[...], g_ref[...]).astype(BF16), w_ref[...], preferred_element_type=F32)


def _norm_proj(x2d, gain, w_bf16, tm):
    m, d = x2d.shape
    n = w_bf16.shape[1]
    return pl.pallas_call(
        _norm_proj_kernel,
        out_shape=jax.ShapeDtypeStruct((m, n), F32),
        grid=(m // tm,),
        in_specs=[pl.BlockSpec((tm, d), lambda i: (i, 0)),
                  pl.BlockSpec((1, d), lambda i: (0, 0)),
                  pl.BlockSpec((d, n), lambda i: (0, 0))],
        out_specs=pl.BlockSpec((tm, n), lambda i: (i, 0)),
        compiler_params=_cparams(("parallel",)),
        name="norm_proj",
    )(x2d, gain.reshape(1, d), w_bf16)


def _tri_inv_lockstep(ms, eye_f):
    ts = [eye_f - m for m in ms]
    pbs = [(-m).astype(BF16) for m in ms]
    for _ in range(5):
        pbs = [jnp.dot(pb, pb, preferred_element_type=F32).astype(BF16) for pb in pbs]
        ts = [t + jnp.dot(t.astype(BF16), pb, preferred_element_type=F32) for t, pb in zip(ts, pbs)]
    return ts


def _deltanet_kernel(qkv_ref, z_ref, sm_ref, conv0_ref, s0_ref, cw_ref, alog_ref, dtb_ref, og_ref,
                     o_ref, convout_ref, sout_ref, xbuf, s_scr, *, nc):
    c = pl.program_id(1)
    t = nc * CHUNK

    @pl.when(c == 0)
    def _():
        xbuf[0:8, :] = conv0_ref[0]
        for h in range(H_A):
            s_scr[:, h * DK_A:(h + 1) * DK_A] = jnp.transpose(s0_ref[0, h])

    x = qkv_ref[0]
    xbuf[8:8 + t, :] = x
    w = cw_ref[...]
    y = (x * w[3:4] + xbuf[7:7 + t, :] * w[2:3] + xbuf[6:6 + t, :] * w[1:2] + xbuf[5:5 + t, :] * w[0:1])
    convout_ref[0] = xbuf[t + 5:t + 8, :]
    xbuf[0:8, :] = xbuf[t:t + 8, :]
    y = _silu(y)

    sm = sm_ref[0]
    beta = jax.nn.sigmoid(sm)
    aa = sm + dtb_ref[...]
    softplus = jnp.maximum(aa, 0.0) + jnp.log1p(jnp.exp(-jnp.abs(aa)))
    g = -jnp.exp(alog_ref[...]) * softplus

    ht = H_A * CHUNK
    hk = H_A * DK_A
    ri = lax.broadcasted_iota(jnp.int32, (ht, ht), 0)
    ci = lax.broadcasted_iota(jnp.int32, (ht, ht), 1)
    same = jnp.right_shift(ri, 6) == jnp.right_shift(ci, 6)
    trilbd = jnp.logical_and(same, ri >= ci)
    strictbd = jnp.logical_and(same, ri > ci)
    eye_f = (ri == ci).astype(F32)
    r2 = lax.broadcasted_iota(jnp.int32, (2 * ht, hk), 0)
    c2 = lax.broadcasted_iota(jnp.int32, (2 * ht, hk), 1)
    bd2 = (jnp.right_shift(r2, 6) & (H_A - 1)) == jnp.right_shift(c2, 7)
    bd1 = bd2[0:ht]
    r64 = lax.broadcasted_iota(jnp.int32, (CHUNK, CHUNK), 0)
    c64 = lax.broadcasted_iota(jnp.int32, (CHUNK, CHUNK), 1)
    tril_f = (r64 >= c64).astype(F32)
    og = og_ref[...]

    def heads_on_rows(a, r0, off, width):
        return jnp.concatenate([a[r0:r0 + CHUNK, off + h * width:off + (h + 1) * width] for h in range(H_A)],
                               axis=0)

    pre = []
    for cidx in range(nc):
        r0 = cidx * CHUNK
        gcum = _dot_f32(tril_f, g[r0:r0 + CHUNK])
        q = heads_on_rows(y, r0, 0, DK_A)
        k = heads_on_rows(y, r0, hk, DK_A)
        v = heads_on_rows(y, r0, 2 * hk, DV_A)
        q = q * lax.rsqrt(jnp.sum(q * q, axis=-1, keepdims=True) + EPS) * (DK_A ** -0.5)
        k = k * lax.rsqrt(jnp.sum(k * k, axis=-1, keepdims=True) + EPS)
        bcol = heads_on_rows(beta, r0, SM_BETA, 1)
        gcol = heads_on_rows(gcum, 0, SM_A, 1)
        glast = [gcum[CHUNK - 1:CHUNK, SM_A + h:SM_A + h + 1] for h in range(H_A)]
        gl_col = jnp.concatenate([jnp.broadcast_to(x_, (CHUNK, 1)) for x_ in glast], axis=0)
        gl_row = jnp.concatenate([jnp.broadcast_to(x_, (1, DK_A)) for x_ in glast], axis=1)
        grow = jnp.transpose(jnp.broadcast_to(gcol, (ht, LANES)))[0:1, :]
        decay = jnp.exp(jnp.where(trilbd, gcol - grow, -jnp.inf))
        kb = k * bcol
        a_full = _dot_nt(jnp.concatenate([kb, q], axis=0), k)
        kd = k * jnp.exp(gl_col - gcol)
        pre.append(dict(
            m=jnp.where(strictbd, a_full[0:ht] * decay, 0.0),
            attn=(a_full[ht:2 * ht] * decay).astype(BF16),
            rhs=jnp.concatenate([v * bcol, kb * jnp.exp(gcol)], axis=1).astype(BF16),
            qg=q * jnp.exp(gcol),
            kd_bd=jnp.where(bd1, jnp.concatenate([kd] * H_A, axis=1), 0.0).astype(BF16),
            decay_row=jnp.exp(gl_row)))

    tmats = _tri_inv_lockstep([c["m"] for c in pre], eye_f)
    uws = [jnp.dot(tm.astype(BF16), c["rhs"], preferred_element_type=F32) for tm, c in zip(tmats, pre)]

    for cidx in range(nc):
        r0 = cidx * CHUNK
        c, uw = pre[cidx], uws[cidx]
        u = uw[:, 0:DV_A]
        wq = jnp.concatenate([uw[:, DV_A:DV_A + DK_A], c["qg"]], axis=0)
        wq_bd = jnp.where(bd2, jnp.concatenate([wq] * H_A, axis=1), 0.0)
        st_old = s_scr[...]
        ws_qs = _dot_nt(wq_bd, st_old)
        v_new = u - ws_qs[0:ht]
        s_scr[...] = st_old * c["decay_row"] + _dot(jnp.transpose(v_new), c["kd_bd"])
        o = ws_qs[ht:2 * ht] + _dot(c["attn"], v_new)
        zz = heads_on_rows(z_ref[0], r0, 0, DV_A)
        res = (_rms(o, og) * _silu(zz)).astype(o_ref.dtype)
        for h in range(H_A):
            o_ref[0, r0:r0 + CHUNK, h * DV_A:(h + 1) * DV_A] = res[h * CHUNK:(h + 1) * CHUNK]

    for h in range(H_A):
        sout_ref[0, h] = jnp.transpose(s_scr[:, h * DK_A:(h + 1) * DK_A])


def _deltanet(proj, conv0, s0, conv_w, a_log, dt_bias, o_gain, nc):
    b, l, _ = proj.shape
    t = nc * CHUNK
    pad = jnp.zeros((1, LANES), F32)
    alog128 = lax.dynamic_update_slice(pad, a_log.reshape(1, H_A), (0, SM_A))
    dtb128 = lax.dynamic_update_slice(pad, dt_bias.reshape(1, H_A), (0, SM_A))
    return pl.pallas_call(
        functools.partial(_deltanet_kernel, nc=nc),
        out_shape=(jax.ShapeDtypeStruct((b, l, H_A * DV_A), BF16),
                   jax.ShapeDtypeStruct((b, CONV_A - 1, C_A), F32),
                   jax.ShapeDtypeStruct((b, H_A, DK_A, DV_A), F32)),
        grid=(b, l // t),
        in_specs=[pl.BlockSpec((1, t, C_A), lambda i, c: (i, c, 0)),
                  pl.BlockSpec((1, t, 512), lambda i, c: (i, c, COL_Z // 512)),
                  pl.BlockSpec((1, t, LANES), lambda i, c: (i, c, COL_SM // LANES)),
                  pl.BlockSpec((1, 8, C_A), lambda i, c: (i, 0, 0)),
                  pl.BlockSpec((1, H_A, DK_A, DV_A), lambda i, c: (i, 0, 0, 0)),
                  pl.BlockSpec((CONV_A, C_A), lambda i, c: (0, 0)),
                  pl.BlockSpec((1, LANES), lambda i, c: (0, 0)),
                  pl.BlockSpec((1, LANES), lambda i, c: (0, 0)),
                  pl.BlockSpec((1, DV_A), lambda i, c: (0, 0))],
        out_specs=(pl.BlockSpec((1, t, H_A * DV_A), lambda i, c: (i, c, 0)),
                   pl.BlockSpec((1, CONV_A - 1, C_A), lambda i, c: (i, 0, 0)),
                   pl.BlockSpec((1, H_A, DK_A, DV_A), lambda i, c: (i, 0, 0, 0))),
        scratch_shapes=[pltpu.VMEM((t + 8, C_A), F32), pltpu.VMEM((DV_A, H_A * DK_A), F32)],
        compiler_params=_cparams(("parallel", "arbitrary")),
        name="deltanet",
    )(proj, proj, proj, conv0, s0, conv_w, alog128, dtb128, o_gain.reshape(1, DV_A))


def _dsa_prep_kernel(*refs, l, p, s_pad):
    n_in = 6 if p else 3
    kv_ref, sm_ref, kg_ref = refs[:3]
    if p:
        ck_ref, cv_ref, cki_ref = refs[3:6]
    kout_ref, vout_ref, kiout_ref, k2_ref, v2_ref, ki2_ref = refs[n_in + 3:]
    kv = kv_ref[0]
    kn = _rms(kv[:, 0:HD_B], kg_ref[...])
    vraw = kv[:, HD_B:2 * HD_B]
    kidx = sm_ref[0][:, 0:D_IDX]
    kout_ref[0, 0] = kn
    vout_ref[0, 0] = vraw
    kiout_ref[0, 0] = kidx

    def with_ones(a):
        return jnp.concatenate([a, jnp.ones_like(a)], axis=1)

    if p:
        k2_ref[0, 0:p, :] = _dup(ck_ref[0]).astype(BF16)
        v2_ref[0, 0:p, :] = with_ones(cv_ref[0]).astype(BF16)
        ki2_ref[0, 0:p, :] = _dup(cki_ref[0]).astype(BF16)
    k2_ref[0, p:p + l, :] = _dup(kn).astype(BF16)
    v2_ref[0, p:p + l, :] = with_ones(vraw).astype(BF16)
    ki2_ref[0, p:p + l, :] = _dup(kidx).astype(BF16)
    if s_pad > p + l:
        zpad = jnp.zeros((s_pad - p - l, LANES), BF16)
        k2_ref[0, p + l:s_pad, :] = zpad
        v2_ref[0, p + l:s_pad, :] = zpad
        ki2_ref[0, p + l:s_pad, :] = zpad


def _colsum(x):
    s, n = x.shape
    return jnp.sum(jnp.sum(x.reshape(s // CHUNK, CHUNK, n), axis=0), axis=0, keepdims=True)


def _dsa_kernel(qb_ref, qi_ref, smq_ref, qg_ref, k2_ref, v2_ref, ki2_ref, _o_in_ref, o_ref, key_scr, eq_scr,
                selt_scr, *, tq, tile0, p, s_keys, k_sel):
    tg = tile0 + pl.program_id(1)

    smq = smq_ref[0]
    sel_r = lax.broadcasted_iota(jnp.int32, (8, LANES), 0)
    sel_c = lax.broadcasted_iota(jnp.int32, (8, LANES), 1)
    pick = (sel_c == sel_r + SM_WI).astype(F32)
    wi_t = _dot_nt_f32(pick, smq) * (H_IDX ** -0.5)
    qi = qi_ref[0]
    ki_all = ki2_ref[0]
    qheads = jnp.concatenate([_split_halves(qi[:, pr * LANES:(pr + 1) * LANES] * (D_IDX ** -0.5))
                              for pr in range(H_IDX // 2)], axis=0)
    score = jnp.zeros((s_keys, tq), F32)
    if tq % LANES == 0:
        d_all = _dot_nt(ki_all, qheads)
        for h in range(H_IDX):
            score = score + jnp.maximum(d_all[:, h * tq:(h + 1) * tq], 0.0) * wi_t[h:h + 1, :]
    else:
        for h in range(H_IDX):
            score = score + jnp.maximum(_dot_nt(ki_all, qheads[h * tq:(h + 1) * tq]), 0.0) * wi_t[h:h + 1, :]

    bits = pltpu.bitcast(score + 0.0, jnp.int32)
    key = bits ^ (jnp.right_shift(bits, 31) & 0x7FFFFFFF)
    kpos = lax.broadcasted_iota(jnp.int32, (s_keys, tq), 0)
    qpos = tg * tq + lax.broadcasted_iota(jnp.int32, (1, tq), 1)
    n_adm = p + (jnp.right_shift(qpos, 6) + 1) * CHUNK
    key_scr[...] = jnp.where(kpos < n_adm, key, INT_MIN)

    kf = float(k_sel)

    def count_ge(cand):
        return _colsum(jnp.where(key_scr[...] >= cand, 1.0, 0.0))

    thr0 = jnp.where(count_ge(jnp.zeros((1, tq), jnp.int32)) >= kf, 0, INT_MIN).astype(jnp.int32)

    def thr_body(i, thr):
        cand = thr | jnp.left_shift(jnp.int32(1), 30 - i)
        return jnp.where(count_ge(cand) >= kf, cand, thr)

    thr = lax.fori_loop(0, 31, thr_body, thr0)

    ge = jnp.logical_and(key_scr[...] >= thr, kpos < n_adm)
    surplus = _colsum(jnp.where(ge, 1.0, 0.0)) - kf
    selt_scr[...] = jnp.where(ge, 0.0, MASKED).astype(BF16)

    @pl.when(jnp.max(surplus) > 0.0)
    def _():
        keys = key_scr[...]
        gt = keys > thr
        eq = jnp.logical_and(keys == thr, kpos < n_adm)
        need = kf - _colsum(jnp.where(gt, 1.0, 0.0))
        eq_scr[...] = jnp.where(eq, 1.0, 0.0)
        nbits = int(s_keys).bit_length()

        def tie_body(i, lim):
            cand = lim | jnp.left_shift(jnp.int32(1), nbits - 1 - i)
            cnt = _colsum(jnp.where(kpos < cand, eq_scr[...], 0.0))
            return jnp.where(cnt <= need, cand, lim)

        lim = lax.fori_loop(0, nbits, tie_body, jnp.zeros((1, tq), jnp.int32))
        keep = jnp.logical_or(gt, jnp.logical_and(eq, kpos < lim))
        selt_scr[...] = jnp.where(keep, 0.0, MASKED).astype(BF16)

    er = lax.broadcasted_iota(jnp.int32, (tq, tq), 0)
    ec = lax.broadcasted_iota(jnp.int32, (tq, tq), 1)
    mask_bias = _dot_nt((er == ec).astype(BF16), selt_scr[...])

    qb = qb_ref[0]
    k_all = k2_ref[0]
    v_all = v2_ref[0]
    lo = _low_half((tq, LANES))
    outs = []
    for pr in range(H_B // 2):
        qn = _pair_rms(qb[:, pr * LANES:(pr + 1) * LANES], qg_ref[...]) * (HD_B ** -0.5 * LOG2E)
        s = _dot_nt(_split_halves(qn), k_all)
        s = (s.reshape(2, tq, s_keys) + mask_bias[None]).reshape(2 * tq, s_keys)
        r = _dot(_exp2_weights(s), v_all)
        o = r / pltpu.roll(r, LANES // 2, 1)
        outs.append(jnp.where(lo, o[0:tq], pltpu.roll(o[tq:2 * tq], LANES // 2, 1)))
    o_ref[0] = jnp.concatenate(outs, axis=1).astype(o_ref.dtype)


def _dsa(proj, q_gain, k_gain, cache, slot, n_slots, prev):
    b, l, _ = proj.shape
    p = 0 if cache is None else cache[0].shape[1]
    tq = min(2 * LANES, l)
    s_k = p + l
    s_pad = -(-s_k // LANES) * LANES
    k_sel = min(TOPK_MAX, s_k // 4)

    kv_specs = [pl.BlockSpec((1, l, LANES), lambda i: (i, 0, COL_KV // LANES)),
                pl.BlockSpec((1, l, LANES), lambda i: (i, 0, COL_SM // LANES)),
                pl.BlockSpec((1, HD_B), lambda i: (0, 0))]
    args = [proj, proj, k_gain.reshape(1, HD_B)]
    if p:
        kv_specs += [pl.BlockSpec((1, p, HD_B), lambda i: (i, 0, 0))] * 3
        args += list(cache)
    row_out = jax.ShapeDtypeStruct((n_slots, b, l, HD_B), F32)
    if prev is None:
        prev = tuple(jnp.zeros(row_out.shape, F32) for _ in range(3))
    aliases = {len(args) + n: n for n in range(3)}
    kv_specs += [pl.BlockSpec(memory_space=pl.ANY)] * 3
    args += list(prev)
    dup_out = jax.ShapeDtypeStruct((b, s_pad, LANES), BF16)
    row_spec = pl.BlockSpec((1, 1, l, HD_B), lambda i: (slot, i, 0, 0))
    dup_spec = pl.BlockSpec((1, s_pad, LANES), lambda i: (i, 0, 0))
    k_b, v_b, ki_b, k2, v2, ki2 = pl.pallas_call(
        functools.partial(_dsa_prep_kernel, l=l, p=p, s_pad=s_pad),
        out_shape=(row_out, row_out, row_out, dup_out, dup_out, dup_out),
        grid=(b,),
        in_specs=kv_specs,
        out_specs=(row_spec, row_spec, row_spec, dup_spec, dup_spec, dup_spec),
        input_output_aliases=aliases,
        compiler_params=_cparams(("parallel",)),
        name="dsa_prep",
    )(*args)

    n_tiles = l // tq
    tiles_per_class = 1 if p == 0 else n_tiles
    qg2 = _dup(q_gain.reshape(1, HD_B))
    o_b = jnp.zeros((b, l, H_B * HD_B), BF16)
    for tile0 in range(0, n_tiles, tiles_per_class):
        nt = min(tiles_per_class, n_tiles - tile0)
        s_keys = min(s_pad, -(-(p + (tile0 + nt) * tq) // LANES) * LANES)
        o_b = pl.pallas_call(
            functools.partial(_dsa_kernel, tq=tq, tile0=tile0, p=p, s_keys=s_keys, k_sel=k_sel),
            out_shape=jax.ShapeDtypeStruct(o_b.shape, o_b.dtype),
            grid=(b, nt),
            in_specs=[pl.BlockSpec((1, tq, 512), lambda i, t, t0=tile0: (i, t0 + t, COL_QB // 512)),
                      pl.BlockSpec((1, tq, 256), lambda i, t, t0=tile0: (i, t0 + t, COL_QI // 256)),
                      pl.BlockSpec((1, tq, LANES), lambda i, t, t0=tile0: (i, t0 + t, COL_SM // LANES)),
                      pl.BlockSpec((1, LANES), lambda i, t: (0, 0)),
                      pl.BlockSpec((1, s_keys, LANES), lambda i, t: (i, 0, 0)),
                      pl.BlockSpec((1, s_keys, LANES), lambda i, t: (i, 0, 0)),
                      pl.BlockSpec((1, s_keys, LANES), lambda i, t: (i, 0, 0)),
                      pl.BlockSpec(memory_space=pl.ANY)],
            out_specs=pl.BlockSpec((1, tq, H_B * HD_B), lambda i, t, t0=tile0: (i, t0 + t, 0)),
            scratch_shapes=[pltpu.VMEM((s_keys, tq), jnp.int32), pltpu.VMEM((s_keys, tq), F32),
                            pltpu.VMEM((s_keys, tq), BF16)],
            input_output_aliases={7: 0},
            compiler_params=_cparams(("parallel", "arbitrary")),
            name="dsa",
        )(proj, proj, proj, qg2, k2, v2, ki2, o_b)
    return o_b, (k_b, v_b, ki_b)


def _band_kernel(*refs, tb, cg, has_cache, first_kept):
    n_in = 8 if has_cache else 6
    q_ref, k_ref, v_ref, bias_ref, qg_ref, kg_ref = refs[:6]
    if has_cache:
        ck_ref, cv_ref = refs[6:8]
    o_ref, kn_ref, vn_ref, kwin, vwin = refs[n_in + 2:]
    j = pl.program_id(1)
    tbq = tb * CHUNK
    gq = cg * CHUNK
    wk = WINDOW + gq

    @pl.when(j == 0)
    def _():
        if has_cache:
            kwin[0:WINDOW, :] = ck_ref[0].astype(BF16)
            vwin[0:WINDOW, :] = cv_ref[0].astype(BF16)
        else:
            kwin[0:WINDOW, :] = jnp.zeros((WINDOW, H_C * HD_C), BF16)
            vwin[0:WINDOW, :] = jnp.zeros((WINDOW, H_C * HD_C), BF16)

    @pl.when(j > 0)
    def _():
        for i in range(WINDOW // tbq):
            kwin[i * tbq:(i + 1) * tbq, :] = kwin[(i + 1) * tbq:(i + 2) * tbq, :]
            vwin[i * tbq:(i + 1) * tbq, :] = vwin[(i + 1) * tbq:(i + 2) * tbq, :]

    vwin[WINDOW:WINDOW + tbq, :] = v_ref[0].astype(BF16)
    kcol = lax.broadcasted_iota(jnp.int32, (2 * gq, wk), 1)
    n_before = jnp.maximum(WINDOW - j * tbq, 0)
    for i in range(H_C // 2):
        cols = slice(i * LANES, (i + 1) * LANES)
        kwin[WINDOW:WINDOW + tbq, cols] = _pair_rms(k_ref[0, :, cols], kg_ref[...]).astype(BF16)
        qn = _pair_rms(q_ref[0, :, cols], qg_ref[...]) * (HD_C ** -0.5 * LOG2E)
        for g in range(tb // cg):
            r0 = g * gq
            s = _dot_nt(_split_halves(qn[r0:r0 + gq]), kwin[r0:r0 + wk, cols]) + bias_ref[i]
            if not has_cache:
                s = jnp.where(kcol >= n_before - r0, s, -jnp.inf)
            m = jnp.max(s, axis=-1, keepdims=True)
            e = jnp.exp2(s - m)
            o = _dot(e, vwin[r0:r0 + wk, cols]) / jnp.sum(e, axis=-1, keepdims=True)
            o_ref[0, r0:r0 + gq, cols] = _join_halves(o, gq).astype(o_ref.dtype)

    @pl.when(j >= first_kept)
    def _():
        rows = pl.ds(pl.multiple_of((j - first_kept) * tbq, tbq), tbq)
        vn_ref[0, 0, rows, :] = v_ref[0]
        for i in range(H_C // 2):
            cols = slice(i * LANES, (i + 1) * LANES)
            kn_ref[0, 0, rows, cols] = _pair_rms(k_ref[0, :, cols], kg_ref[...])


def _band(proj, rel_bias, q_gain, k_gain, cache, tb, cg, keep, slot, n_slots, prev):
    b, l, _ = proj.shape
    d = H_C * HD_C
    tbq = tb * CHUNK
    gq = cg * CHUNK
    wk = WINDOW + gq
    has_cache = cache is not None
    assert keep % tbq == 0 and l % tbq == 0
    first_kept = (l - keep) // tbq
    in_specs = [pl.BlockSpec((1, tbq, d), lambda i, j: (i, j, 0)),
                pl.BlockSpec((1, tbq, d), lambda i, j: (i, j, 1)),
                pl.BlockSpec((1, tbq, d), lambda i, j: (i, j, 2)),
                pl.BlockSpec((H_C // 2, 2 * gq, wk), lambda i, j: (0, 0, 0)),
                pl.BlockSpec((1, LANES), lambda i, j: (0, 0)),
                pl.BlockSpec((1, LANES), lambda i, j: (0, 0))]
    args = [proj, proj, proj, _band_bias(rel_bias, cg), _dup(q_gain.reshape(1, HD_C)), _dup(k_gain.reshape(1, HD_C))]
    if has_cache:
        in_specs += [pl.BlockSpec((1, WINDOW, d), lambda i, j: (i, 0, 0))] * 2
        args += [cache[0].reshape(b, WINDOW, d), cache[1].reshape(b, WINDOW, d)]
    cache_out = jax.ShapeDtypeStruct((n_slots, b, keep, d), F32)
    if prev is None:
        prev = tuple(jnp.zeros(cache_out.shape, F32) for _ in range(2))
    aliases = {len(args): 1, len(args) + 1: 2}
    in_specs += [pl.BlockSpec(memory_space=pl.ANY)] * 2
    args += list(prev)
    cache_spec = pl.BlockSpec((1, 1, keep, d), lambda i, j: (slot, i, 0, 0))
    o_c, kn, vn = pl.pallas_call(
        functools.partial(_band_kernel, tb=tb, cg=cg, has_cache=has_cache, first_kept=first_kept),
        out_shape=(jax.ShapeDtypeStruct((b, l, d), BF16), cache_out, cache_out),
        grid=(b, l // tbq),
        in_specs=in_specs,
        out_specs=(pl.BlockSpec((1, tbq, d), lambda i, j: (i, j, 0)), cache_spec, cache_spec),
        scratch_shapes=[pltpu.VMEM((WINDOW + tbq, d), BF16), pltpu.VMEM((WINDOW + tbq, d), BF16)],
        input_output_aliases=aliases,
        compiler_params=_cparams(("parallel", "arbitrary")),
        name="band",
    )(*args)
    return o_c, (kn, vn)


def _band_bias(rel_bias, cg):
    gq = cg * CHUNK
    wk = WINDOW + gq
    n = wk + gq - 1
    dist = WINDOW + gq - 1 - jnp.arange(n)
    seq = rel_bias[:, jnp.clip(dist, -(CHUNK - 1), REL_CLIP) + (CHUNK - 1)].astype(F32) * LOG2E
    period = jnp.roll(jnp.pad(seq, ((0, 0), (0, 1))), -(gq - 1), axis=1)
    table = jnp.tile(period, (1, gq))[:, :gq * n].reshape(H_C, gq, n)[:, :, :wk]
    qc = jnp.arange(gq) // CHUNK
    kc = jnp.arange(wk) // CHUNK
    inband = jnp.logical_and(kc[None, :] >= qc[:, None], kc[None, :] <= qc[:, None] + BAND_CHUNKS)
    return jnp.where(inband[None], table, -jnp.inf).reshape(H_C // 2, 2 * gq, wk)


def _ffn_kernel(*refs, n_mix, nf, nm, nb, tm, tf, final):
    x_ref = refs[0]
    mix_refs = refs[1:1 + n_mix]
    wo_refs = refs[1 + n_mix:1 + 2 * n_mix]
    rest = refs[1 + 2 * n_mix:]
    if final:
        g_ref, wa_ref, wg_ref, cw_ref, cb_ref, wd_ref, st0_ref, gf_ref = rest[:8]
        rest = rest[8:]
    else:
        g_ref, wa_ref, wg_ref, cw_ref, cb_ref, wd_ref, st0_ref = rest[:7]
        gf_ref = None
        rest = rest[7:]
    out_ref, stout_ref, abuf, cbuf, ubuf = rest
    mi = pl.program_id(1)
    rows = nb * tm
    bsel = (slice(None),) if nb > 1 else (0,)

    def flat(ref):
        return ref[...].reshape(rows, ref.shape[-1]) if nb > 1 else ref[0]

    def per_stream(z):
        return z.reshape(nb, tm, z.shape[-1]) if nb > 1 else z

    xr = flat(x_ref)
    for m_ref, w_ref in zip(mix_refs, wo_refs):
        xr = xr + jnp.dot(flat(m_ref), w_ref[...], preferred_element_type=F32)
    hb = _rms(xr, g_ref[...]).astype(BF16)

    @pl.when(mi == 0)
    def _():
        cbuf[...] = st0_ref[...]

    for f in range(nf):
        cols = slice(f * tf, (f + 1) * tf)
        slot = f % 2
        a = per_stream(jnp.dot(hb, wa_ref[:, cols], preferred_element_type=F32))
        gate = per_stream(jnp.dot(hb, wg_ref[:, cols], preferred_element_type=F32))
        abuf[(slot, *bsel, slice(0, 8))] = cbuf[(*bsel, slice(None), cols)]
        abuf[(slot, *bsel, slice(8, 8 + tm))] = a
        ac = (a * cw_ref[2:3, cols] + abuf[(slot, *bsel, slice(7, 7 + tm))] * cw_ref[1:2, cols]
              + abuf[(slot, *bsel, slice(6, 6 + tm))] * cw_ref[0:1, cols] + cb_ref[:, cols])
        cbuf[(*bsel, slice(None), cols)] = abuf[(slot, *bsel, slice(tm, tm + 8))]
        ubuf[:, cols] = (_silu(ac) * gate).reshape(rows, tf).astype(BF16)

    yv = xr + jnp.dot(ubuf[...], wd_ref[...], preferred_element_type=F32)
    if final:
        yv = _rms(yv, gf_ref[...])
    out_ref[...] = yv.reshape(nb, tm, yv.shape[-1])

    @pl.when(mi == nm - 1)
    def _():
        stout_ref[...] = cbuf[:, 8 - (CONV_F - 1):8, :]


def _ffn_vmem_bytes(rows, d, dff, tf, mix_cols):
    weights = 2 * (3 * d * dff + sum(mix_cols) * d)
    tiles = 2 * (2 * rows * d * 4 + rows * sum(mix_cols) * 2)
    scratch = 2 * (rows + 64) * tf * 4 + 64 * dff * 4
    temps = rows * d * (4 + 4 + 2) + 6 * rows * tf * 4
    return weights + tiles + scratch + temps


def _ffn(x, mixes, wos, gain, wa, wg, conv_w, conv_b, wd, st0, gain_final, nb, tm, tf):
    b, l, d = x.shape
    dff = wa.shape[1]
    nf = dff // tf
    nm = l // tm
    n_mix = len(mixes)
    final = gain_final is not None
    in_specs = [pl.BlockSpec((nb, tm, d), lambda i, m: (i, m, 0))]
    in_specs += [pl.BlockSpec((nb, tm, mx.shape[2]), lambda i, m: (i, m, 0)) for mx in mixes]
    in_specs += [pl.BlockSpec(w.shape, lambda i, m: (0, 0)) for w in wos]
    in_specs += [pl.BlockSpec((1, d), lambda i, m: (0, 0)),
                 pl.BlockSpec((d, dff), lambda i, m: (0, 0)),
                 pl.BlockSpec((d, dff), lambda i, m: (0, 0)),
                 pl.BlockSpec((CONV_F, dff), lambda i, m: (0, 0)),
                 pl.BlockSpec((1, dff), lambda i, m: (0, 0)),
                 pl.BlockSpec((dff, d), lambda i, m: (0, 0)),
                 pl.BlockSpec((nb, 8, dff), lambda i, m: (i, 0, 0))]
    args = [x, *mixes, *wos, gain.reshape(1, d), wa, wg, conv_w, conv_b.reshape(1, dff), wd, st0]
    if final:
        in_specs.append(pl.BlockSpec((1, d), lambda i, m: (0, 0)))
        args.append(gain_final.reshape(1, d))
    return pl.pallas_call(
        functools.partial(_ffn_kernel, n_mix=n_mix, nf=nf, nm=nm, nb=nb, tm=tm, tf=tf, final=final),
        out_shape=(jax.ShapeDtypeStruct((b, l, d), F32),
                   jax.ShapeDtypeStruct((b, CONV_F - 1, dff), F32)),
        grid=(b // nb, nm),
        in_specs=in_specs,
        out_specs=(pl.BlockSpec((nb, tm, d), lambda i, m: (i, m, 0)),
                   pl.BlockSpec((nb, CONV_F - 1, dff), lambda i, m: (i, 0, 0))),
        scratch_shapes=[pltpu.VMEM((2, nb, tm + 8, tf), F32), pltpu.VMEM((nb, 8, dff), F32),
                        pltpu.VMEM((nb * tm, dff), BF16)],
        compiler_params=_cparams(("parallel", "arbitrary"),
                                 min(V7X_VMEM_USABLE, _ffn_vmem_bytes(nb * tm, d, dff, tf, [mx.shape[2] for mx in mixes]))),
        name="ffn",
    )(*args)


def _pad_rows_front(a, rows):
    return jnp.pad(a, ((0, 0), (rows - a.shape[1], 0), (0, 0)))


def _trunk(x, past, w):
    b, l, d = x.shape
    depth = w["norm_mix"].shape[0]
    first = past is None
    n_chunks = l // CHUNK
    tm_proj = min(512, b * l)
    tm_ffn = min(512, l)
    nb_ffn = math.gcd(b, max(1, 512 // tm_ffn))
    tf = 256
    nc = min(8, n_chunks)
    tb = min(8, n_chunks)
    cg = min(2, tb)
    dff = w["ffn_w_a"].shape[2]
    n_even, n_odd = (depth + 1) // 2, depth // 2
    keep = min(WINDOW, l) if first else l
    dsa_new = band_new = None
    out = {k: [] for k in ("dn_S", "dn_conv", "ffn_conv")}
    for layer in range(depth):
        x2d = x.reshape(b * l, d)
        if layer % 2 == 0:
            e = layer // 2
            proj = _norm_proj(x2d, w["norm_mix"][layer], w["w_in_even"][e], tm_proj).reshape(b, l, EVEN_COLS)
            if first:
                conv0 = jnp.zeros((b, 8, C_A), F32)
                s0 = jnp.zeros((b, H_A, DK_A, DV_A), F32)
                cache = None
            else:
                conv0 = _pad_rows_front(past["state_dn_conv"][e], 8)
                s0 = past["state_dn_S"][e]
                cache = (past["cache_dsa_k"][e], past["cache_dsa_v"][e], past["cache_dsa_kidx"][e])
            o_a, buf_a, s_a = _deltanet(proj, conv0, s0, w["dn_conv_w"][e], w["dn_a_log"][e],
                                        w["dn_dt_bias"][e], w["dn_o_gain"][e], nc)
            o_b, dsa_new = _dsa(proj, w["dsa_q_gain"][e], w["dsa_k_gain"][e], cache, e, n_even, dsa_new)
            mixes = [o_a, o_b]
            wos = [w["w_out_even"][e][:H_A * DV_A], w["w_out_even"][e][H_A * DV_A:]]
            out["dn_S"].append(s_a)
            out["dn_conv"].append(buf_a)
        else:
            jj = layer // 2
            proj = _norm_proj(x2d, w["norm_mix"][layer], w["w_in_odd"][jj], tm_proj).reshape(b, l, 3 * H_C * HD_C)
            cache = None if first else (past["cache_band_k"][jj], past["cache_band_v"][jj])
            o_c, band_new = _band(proj, w["band_rel_bias"][jj], w["band_q_gain"][jj], w["band_k_gain"][jj], cache,
                                  tb, cg, keep, jj, n_odd, band_new)
            mixes = [o_c]
            wos = [w["w_out_odd"][jj]]
        st0 = (jnp.zeros((b, 8, dff), F32) if first else _pad_rows_front(past["state_ffn_conv"][layer], 8))
        gfin = w["norm_final"] if layer == depth - 1 else None
        x, fbuf = _ffn(x, mixes, wos, w["norm_ffn"][layer], w["ffn_w_a"][layer], w["ffn_w_g"][layer],
                       w["ffn_conv_w"][layer], w["ffn_conv_b"][layer], w["ffn_w_down"][layer], st0, gfin,
                       nb_ffn, tm_ffn, tf)
        out["ffn_conv"].append(fbuf)
    states = {k: jnp.stack(v) for k, v in out.items()}
    states["dsa_k"], states["dsa_v"], states["dsa_kidx"] = dsa_new
    states["band_k"], states["band_v"] = (a.reshape(n_odd, b, keep, H_C, HD_C) for a in band_new)
    return x, states


def _prep_weights(norm_mix, norm_ffn, norm_final, w_in_even, dn_conv_w, dn_a_log, dn_dt_bias, dn_o_gain,
                  dsa_q_gain, dsa_k_gain, w_out_even, w_in_odd, band_q_gain, band_k_gain, band_rel_bias,
                  w_out_odd, ffn_w_a, ffn_w_g, ffn_conv_w, ffn_conv_b, ffn_w_down):
    o_ba = 4 * H_A * DK_A
    o_qb = o_ba + 2 * H_A
    o_kb = o_qb + H_B * HD_B
    o_qi = o_kb + 2 * HD_B
    o_ki = o_qi + H_IDX * D_IDX
    o_wi = o_ki + D_IDX
    o_end = o_wi + H_IDX
    wie = w_in_even
    pad = jnp.zeros(wie.shape[:2] + (EVEN_COLS - o_end,), wie.dtype)
    w_even = jnp.concatenate([wie[..., :o_ba], wie[..., o_qb:o_kb], wie[..., o_qi:o_ki], wie[..., o_kb:o_qi],
                              wie[..., o_ki:o_wi], wie[..., o_ba:o_qb], wie[..., o_wi:o_end], pad], axis=-1)
    return dict(norm_mix=norm_mix, norm_ffn=norm_ffn, norm_final=norm_final,
                w_in_even=w_even.astype(BF16), dn_conv_w=dn_conv_w, dn_a_log=dn_a_log, dn_dt_bias=dn_dt_bias,
                dn_o_gain=dn_o_gain, dsa_q_gain=dsa_q_gain, dsa_k_gain=dsa_k_gain,
                w_out_even=w_out_even.astype(BF16), w_in_odd=w_in_odd.astype(BF16),
                band_q_gain=band_q_gain, band_k_gain=band_k_gain, band_rel_bias=band_rel_bias,
                w_out_odd=w_out_odd.astype(BF16),
                ffn_w_a=ffn_w_a.astype(BF16), ffn_w_g=ffn_w_g.astype(BF16), ffn_conv_w=ffn_conv_w,
                ffn_conv_b=ffn_conv_b, ffn_w_down=ffn_w_down.astype(BF16))


def kernel(x_prompt, x_sample, state_dn_S, state_dn_conv, cache_dsa_k, cache_dsa_v, cache_dsa_kidx, cache_band_k, cache_band_v, state_ffn_conv, norm_mix, norm_ffn, norm_final, w_in_even, dn_conv_w, dn_a_log, dn_dt_bias, dn_o_gain, dsa_q_gain, dsa_k_gain, w_out_even, w_in_odd, band_q_gain, band_k_gain, band_rel_bias, w_out_odd, ffn_w_a, ffn_w_g, ffn_conv_w, ffn_conv_b, ffn_w_down):
    assert cache_band_k.shape[2] == WINDOW, "band cache must hold exactly one window"
    w = _prep_weights(norm_mix, norm_ffn, norm_final, w_in_even, dn_conv_w, dn_a_log, dn_dt_bias, dn_o_gain,
                      dsa_q_gain, dsa_k_gain, w_out_even, w_in_odd, band_q_gain, band_k_gain, band_rel_bias,
                      w_out_odd, ffn_w_a, ffn_w_g, ffn_conv_w, ffn_conv_b, ffn_w_down)
    past = dict(state_dn_S=state_dn_S, state_dn_conv=state_dn_conv, cache_dsa_k=cache_dsa_k,
                cache_dsa_v=cache_dsa_v, cache_dsa_kidx=cache_dsa_kidx, cache_band_k=cache_band_k,
                cache_band_v=cache_band_v, state_ffn_conv=state_ffn_conv)
    y_p, sp = _trunk(x_prompt, None, w)
    y_s, ss = _trunk(x_sample, past, w)
    return (y_p, y_s,
            sp["dn_S"], ss["dn_S"], sp["dn_conv"], ss["dn_conv"],
            sp["dsa_k"], ss["dsa_k"], sp["dsa_v"], ss["dsa_v"], sp["dsa_kidx"], ss["dsa_kidx"],
            sp["band_k"], ss["band_k"], sp["band_v"], ss["band_v"],
            sp["ffn_conv"], ss["ffn_conv"])
```
